```python
import math
import jax
import jax.numpy as jnp
from jax import lax
import numpy as np

D_MODEL = 2048
BATCH = 4
SEQ = 4096
DEPTH = 2

CHUNK = 64
Q_BLOCK = 128
EPS = 1e-6
N_BRANCH = 3
BRANCH_WIDTH = D_MODEL // 2
A_NOPE = 128
A_ROPE = 64
A_V = 128
A_HEADS = BRANCH_WIDTH // A_V
A_Q_RANK = D_MODEL // 4
A_KV_RANK = D_MODEL // 8
A_ROPE_THETA = 10000.0
B_DK = 64
B_DV = 128
B_HEADS = BRANCH_WIDTH // B_DV
B_ROT = B_DK // 4
B_ROPE_THETA = 500000.0
C_DH = 128
C_HEADS = BRANCH_WIDTH // C_DH
C_LEFT_CHUNKS = 8
C_BAND = (C_LEFT_CHUNKS + 1) * CHUNK
C_REL_MAX = 128
C_N_REL = (CHUNK - 1) + C_REL_MAX + 1

IN_SPLITS = (A_Q_RANK, A_KV_RANK, A_ROPE, BRANCH_WIDTH,
             B_HEADS * 2 * B_DK, B_HEADS * 2 * B_DK, B_HEADS * B_DV, BRANCH_WIDTH,
             C_HEADS * C_DH, C_HEADS * C_DH, C_HEADS * C_DH, BRANCH_WIDTH,
             N_BRANCH * D_MODEL)
D_IN = sum(IN_SPLITS)
SPLIT_POINTS = tuple(int(v) for v in np.cumsum(IN_SPLITS)[:-1])

kernel_name = "hybrid_mla_diff_chunkband_block"


def _rmsnorm(x, g):
    xf = x.astype(jnp.float32)
    y = xf * lax.rsqrt(jnp.mean(xf * xf, axis=-1, keepdims=True) + EPS)
    return (y * g.astype(jnp.float32)).astype(x.dtype)


def _rope_tables(s, dim, theta):
    inv = 1.0 / (jnp.float32(theta) ** (jnp.arange(0, dim, 2, dtype=jnp.float32) / dim))
    ang = jnp.arange(s, dtype=jnp.float32)[:, None] * inv[None, :]
    return jnp.cos(ang), jnp.sin(ang)


def _rope(x, cos, sin):
    half = x.shape[-1] // 2
    shape = (1, x.shape[1]) + (1,) * (x.ndim - 3) + (half,)
    c = cos.reshape(shape).astype(x.dtype)
    sn = sin.reshape(shape).astype(x.dtype)
    x1, x2 = x[..., :half], x[..., half:]
    return jnp.concatenate([x1 * c - x2 * sn, x2 * c + x1 * sn], axis=-1)


def _partial_rope(x, cos, sin):
    return jnp.concatenate([_rope(x[..., :B_ROT], cos, sin), x[..., B_ROT:]], axis=-1)


def _to_blocks(t, size):
    b, s = t.shape[0], t.shape[1]
    return jnp.moveaxis(t.reshape((b, s // size, size) + t.shape[2:]), 1, 0)


def _from_blocks(t):
    t = jnp.moveaxis(t, 0, 1)
    return t.reshape((t.shape[0], t.shape[1] * t.shape[2]) + t.shape[3:])


def _chunk_causal_mask(q_start, q_len, k_len):
    q_chunk = (q_start + jnp.arange(q_len)) // CHUNK
    k_chunk = jnp.arange(k_len) // CHUNK
    return k_chunk[None, :] <= q_chunk[:, None]


def _mla_attention(qn, qr, kn, kr, v):
    s_len = kn.shape[1]
    scale = (A_NOPE + A_ROPE) ** -0.5

    def block(args):
        qn_b, qr_b, i = args
        sc = (jnp.einsum('bqhd,bkhd->bhqk', qn_b, kn)
              + jnp.einsum('bqhr,bkr->bhqk', qr_b, kr)).astype(jnp.float32) * scale
        mask = _chunk_causal_mask(i * Q_BLOCK, Q_BLOCK, s_len)
        p = jax.nn.softmax(jnp.where(mask, sc, -jnp.inf), axis=-1)
        return jnp.einsum('bhqk,bkhd->bqhd', p.astype(v.dtype), v)

    n_blocks = s_len // Q_BLOCK
    out = lax.map(block, (_to_blocks(qn, Q_BLOCK), _to_blocks(qr, Q_BLOCK), jnp.arange(n_blocks)))
    return _from_blocks(out)


def _diff_attention(q, k, v, lam):
    s_len = k.shape[1]
    scale = B_DK ** -0.5

    def block(args):
        q_b, i = args
        sc = jnp.einsum('bqhcd,bkhcd->bhcqk', q_b, k).astype(jnp.float32) * scale
        mask = _chunk_causal_mask(i * Q_BLOCK, Q_BLOCK, s_len)
        p = jax.nn.softmax(jnp.where(mask, sc, -jnp.inf), axis=-1)
        a = p[:, :, 0] - lam * p[:, :, 1]
        return jnp.einsum('bhqk,bkhe->bqhe', a.astype(v.dtype), v)

    n_blocks = s_len // Q_BLOCK
    out = lax.map(block, (_to_blocks(q, Q_BLOCK), jnp.arange(n_blocks)))
    return _from_blocks(out)


def _chunk_band_attention(q, k, v, rel_bias):
    s_len = q.shape[1]
    pad = C_LEFT_CHUNKS * CHUNK
    widths = ((0, 0), (pad, 0), (0, 0), (0, 0))
    k_pad = jnp.pad(k, widths)
    v_pad = jnp.pad(v, widths)
    q_idx = jnp.arange(CHUNK)[:, None]
    k_idx = jnp.arange(C_BAND)[None, :]
    rel = jnp.clip(pad + q_idx - k_idx, -(CHUNK - 1), C_REL_MAX) + (CHUNK - 1)
    bias = rel_bias.astype(jnp.float32)[:, rel]
    key_offset = jnp.arange(C_BAND) - pad
    scale = C_DH ** -0.5

    def one_chunk(c):
        start = c * CHUNK
        q_c = lax.dynamic_slice_in_dim(q, start, CHUNK, axis=1)
        k_c = lax.dynamic_slice_in_dim(k_pad, start, C_BAND, axis=1)
        v_c = lax.dynamic_slice_in_dim(v_pad, start, C_BAND, axis=1)
        sc = jnp.einsum('bqhd,bkhd->bhqk', q_c, k_c).astype(jnp.float32) * scale + bias
        valid = (start + key_offset) >= 0
        p = jax.nn.softmax(jnp.where(valid, sc, -jnp.inf), axis=-1)
        return jnp.einsum('bhqk,bkhd->bqhd', p.astype(v.dtype), v_c)

    out = lax.map(one_chunk, jnp.arange(s_len // CHUNK))
    return _from_blocks(out)


def _hybrid_layer(x, layer_idx, rope_a, rope_b, g_pre, w_in, a_g_cq, a_g_ckv, a_w_uq, a_w_ukv,
                  a_g_q, a_g_k, b_g_q, b_g_k, b_lam, b_g_sub, c_g_q, c_g_k, c_rel_bias,
                  w_branch, w_out):
    b, s, _ = x.shape
    h = _rmsnorm(x, g_pre)
    u = h @ w_in
    (a_cq, a_ckv, a_kr, a_z, b_q, b_k, b_v, b_z,
     c_q, c_k, c_v, c_z, gate_logits) = jnp.split(u, SPLIT_POINTS, axis=-1)

    qa = (_rmsnorm(a_cq, a_g_cq) @ a_w_uq).reshape(b, s, A_HEADS, A_NOPE + A_ROPE)
    kva = (_rmsnorm(a_ckv, a_g_ckv) @ a_w_ukv).reshape(b, s, A_HEADS, A_NOPE + A_V)
    qn = _rmsnorm(qa[..., :A_NOPE], a_g_q[:A_NOPE])
    qr = _rope(_rmsnorm(qa[..., A_NOPE:], a_g_q[A_NOPE:]), *rope_a)
    kn = _rmsnorm(kva[..., :A_NOPE], a_g_k[:A_NOPE])
    kr = _rope(_rmsnorm(a_kr, a_g_k[A_NOPE:]), *rope_a)
    o_a = _mla_attention(qn, qr, kn, kr, kva[..., A_NOPE:]).reshape(b, s, BRANCH_WIDTH)

    bq = _partial_rope(_rmsnorm(b_q.reshape(b, s, B_HEADS, 2, B_DK), b_g_q), *rope_b)
    bk = _partial_rope(_rmsnorm(b_k.reshape(b, s, B_HEADS, 2, B_DK), b_g_k), *rope_b)
    bv = b_v.reshape(b, s, B_HEADS, B_DV)
    lam_init = 0.8 - 0.6 * math.exp(-0.3 * layer_idx)
    lf = b_lam.astype(jnp.float32)
    lam = jnp.exp(jnp.sum(lf[0] * lf[1])) - jnp.exp(jnp.sum(lf[2] * lf[3])) + lam_init
    ob = _rmsnorm(_diff_attention(bq, bk, bv, lam), b_g_sub) * (1.0 - lam_init)
    o_b = ob.reshape(b, s, BRANCH_WIDTH)

    cq = _rmsnorm(c_q.reshape(b, s, C_HEADS, C_DH), c_g_q)
    ck = _rmsnorm(c_k.reshape(b, s, C_HEADS, C_DH), c_g_k)
    cv = c_v.reshape(b, s, C_HEADS, C_DH)
    o_c = _chunk_band_attention(cq, ck, cv, c_rel_bias).reshape(b, s, BRANCH_WIDTH)

    gates = jax.nn.sigmoid(gate_logits.astype(jnp.float32)).astype(x.dtype).reshape(b, s, N_BRANCH, D_MODEL)
    branches = (o_a * jax.nn.silu(a_z), o_b * jax.nn.silu(b_z), o_c * jax.nn.silu(c_z))
    y = gates[:, :, 0] * (branches[0] @ w_branch[0])
    for n in range(1, N_BRANCH):
        y = y + gates[:, :, n] * (branches[n] @ w_branch[n])
    return y @ w_out


def setup_inputs(seed: int = 0) -> dict:
    key = jax.random.key(seed)
    ks = jax.random.split(key, 18)

    def nrm(k, shape, scale):
        return jax.random.normal(k, shape, jnp.float32) * scale

    def gain(k, shape):
        return 1.0 + 0.02 * jax.random.normal(k, shape, jnp.float32)

    return {
        "x": nrm(ks[0], (BATCH, SEQ, D_MODEL), 1.0),
        "g_pre": gain(ks[1], (DEPTH, D_MODEL)),
        "w_in": nrm(ks[2], (DEPTH, D_MODEL, D_IN), D_MODEL ** -0.5),
        "a_g_cq": gain(ks[3], (DEPTH, A_Q_RANK)),
        "a_g_ckv": gain(ks[4], (DEPTH, A_KV_RANK)),
        "a_w_uq": nrm(ks[5], (DEPTH, A_Q_RANK, A_HEADS * (A_NOPE + A_ROPE)), A_Q_RANK ** -0.5),
        "a_w_ukv": nrm(ks[6], (DEPTH, A_KV_RANK, A_HEADS * (A_NOPE + A_V)), A_KV_RANK ** -0.5),
        "a_g_q": gain(ks[7], (DEPTH, A_NOPE + A_ROPE)),
        "a_g_k": gain(ks[8], (DEPTH, A_NOPE + A_ROPE)),
        "b_g_q": gain(ks[9], (DEPTH, B_DK)),
        "b_g_k": gain(ks[10], (DEPTH, B_DK)),
        "b_lam": nrm(ks[11], (DEPTH, 4, B_DK), 0.1),
        "b_g_sub": gain(ks[12], (DEPTH, B_DV)),
        "c_g_q": gain(ks[13], (DEPTH, C_DH)),
        "c_g_k": gain(ks[14], (DEPTH, C_DH)),
        "c_rel_bias": nrm(ks[15], (DEPTH, C_HEADS, C_N_REL), 0.2),
        "w_branch": nrm(ks[16], (DEPTH, N_BRANCH, BRANCH_WIDTH, D_MODEL), BRANCH_WIDTH ** -0.5),
        "w_out": nrm(ks[17], (DEPTH, D_MODEL, D_MODEL), D_MODEL ** -0.5),
    }


def reference(x, g_pre, w_in, a_g_cq, a_g_ckv, a_w_uq, a_w_ukv, a_g_q, a_g_k, b_g_q, b_g_k,
              b_lam, b_g_sub, c_g_q, c_g_k, c_rel_bias, w_branch, w_out):
    s = x.shape[1]
    rope_a = _rope_tables(s, A_ROPE, A_ROPE_THETA)
    rope_b = _rope_tables(s, B_ROT, B_ROPE_THETA)
    for l in range(DEPTH):
        x = x + _hybrid_layer(x, l, rope_a, rope_b, g_pre[l], w_in[l], a_g_cq[l], a_g_ckv[l],
                              a_w_uq[l], a_w_ukv[l], a_g_q[l], a_g_k[l], b_g_q[l], b_g_k[l],
                              b_lam[l], b_g_sub[l], c_g_q[l], c_g_k[l], c_rel_bias[l],
                              w_branch[l], w_out[l])
    return x
```

```python
import functools
import math

import jax
import jax.numpy as jnp
from jax import lax
from jax.experimental import pallas as pl
from jax.experimental.pallas import tpu as pltpu

F32 = jnp.float32
BF16 = jnp.bfloat16

EPS = 1e-6
CHUNK = 64
CHUNK_SHIFT = 6
MASKED = -1e30

A_NOPE, A_ROPE, A_V = 128, 64, 128
A_ROPE_THETA = 10000.0
B_DK, B_DV = 64, 128
B_ROT = B_DK // 4
B_ROPE_THETA = 500000.0
C_DH = 128
C_LEFT_CHUNKS = 8
C_REL_MAX = 128
N_BRANCH = 3

LANES = 128
V7X_VMEM_BUDGET = 56 * 2**20


def _params(n_axes, block_bytes, temp_bytes=0):
    need = 2 * block_bytes + temp_bytes + (4 << 20)
    return pltpu.CompilerParams(
        dimension_semantics=("parallel",) * n_axes,
        vmem_limit_bytes=int(min(max(need, 16 << 20), V7X_VMEM_BUDGET)),
    )


def _sigmoid(z):
    return 1.0 / (1.0 + jnp.exp(-z))


def _rms(x, g, n):
    ss = jnp.sum(x * x, axis=-1, keepdims=True) * (1.0 / n)
    return x * lax.rsqrt(ss + EPS) * g


def _rmsnorm_kernel(x_ref, g_ref, o_ref):
    x = x_ref[...]
    o_ref[...] = _rms(x, g_ref[...], x.shape[-1]).astype(o_ref.dtype)


def _rmsnorm(x, g, tm=512):
    t, d = x.shape
    return pl.pallas_call(
        _rmsnorm_kernel,
        out_shape=jax.ShapeDtypeStruct((t, d), BF16),
        grid=(t // tm,),
        in_specs=[pl.BlockSpec((tm, d), lambda i: (i, 0)), pl.BlockSpec((1, d), lambda i: (0, 0))],
        out_specs=pl.BlockSpec((tm, d), lambda i: (i, 0)),
        compiler_params=_params(1, tm * d * 6, tm * d * 8),
        name="pre_rmsnorm",
    )(x, g)


def _mm_kernel(a_ref, b_ref, o_ref):
    o_ref[...] = jnp.dot(a_ref[...], b_ref[...], preferred_element_type=F32).astype(o_ref.dtype)


def _matmul(a, b, out_dtype, tm, tn, name):
    m, k = a.shape
    n = b.shape[1]
    blocks = tm * k * 2 + k * tn * 2 + tm * tn * jnp.dtype(out_dtype).itemsize
    return pl.pallas_call(
        _mm_kernel,
        out_shape=jax.ShapeDtypeStruct((m, n), out_dtype),
        grid=(m // tm, n // tn),
        in_specs=[pl.BlockSpec((tm, k), lambda i, j: (i, 0)), pl.BlockSpec((k, tn), lambda i, j: (0, j))],
        out_specs=pl.BlockSpec((tm, tn), lambda i, j: (i, j)),
        compiler_params=_params(2, blocks, tm * tn * 4),
        name=name,
    )(a, b)


def _mm_res_kernel(a_ref, b_ref, x_ref, o_ref):
    o_ref[...] = x_ref[...] + jnp.dot(a_ref[...], b_ref[...], preferred_element_type=F32)


def _matmul_residual(a, b, x, tm, tn):
    m, k = a.shape
    n = b.shape[1]
    blocks = tm * k * 2 + k * tn * 2 + 2 * tm * tn * 4
    return pl.pallas_call(
        _mm_res_kernel,
        out_shape=jax.ShapeDtypeStruct((m, n), F32),
        grid=(m // tm, n // tn),
        in_specs=[
            pl.BlockSpec((tm, k), lambda i, j: (i, 0)),
            pl.BlockSpec((k, tn), lambda i, j: (0, j)),
            pl.BlockSpec((tm, tn), lambda i, j: (i, j)),
        ],
        out_specs=pl.BlockSpec((tm, tn), lambda i, j: (i, j)),
        compiler_params=_params(2, blocks, tm * tn * 4),
        name="out_proj_residual",
    )(a, b, x)


def _merge_kernel(oa_ref, ob_ref, oc_ref, w_ref, ga_ref, gb_ref, gc_ref, y_ref):
    y = None
    for n, (o_ref, g_ref) in enumerate(((oa_ref, ga_ref), (ob_ref, gb_ref), (oc_ref, gc_ref))):
        gate = _sigmoid(g_ref[...].astype(F32))
        term = gate * jnp.dot(o_ref[...], w_ref[n], preferred_element_type=F32)
        y = term if y is None else y + term
    y_ref[...] = y.astype(y_ref.dtype)


def _merge(o_a, o_b, o_c, w_branch, u, gate_off, tm, tn):
    t, bw = o_a.shape
    d = w_branch.shape[2]
    nj = d // tn
    g0 = gate_off // tn
    branch_spec = pl.BlockSpec((tm, bw), lambda i, j: (i, 0))
    gate_specs = [pl.BlockSpec((tm, tn), functools.partial(lambda i, j, n: (i, g0 + n * nj + j), n=n))
                  for n in range(N_BRANCH)]
    blocks = 3 * tm * bw * 2 + N_BRANCH * bw * tn * 2 + 3 * tm * tn * 2 + tm * tn * 2
    return pl.pallas_call(
        _merge_kernel,
        out_shape=jax.ShapeDtypeStruct((t, d), BF16),
        grid=(t // tm, nj),
        in_specs=[branch_spec, branch_spec, branch_spec,
                  pl.BlockSpec((N_BRANCH, bw, tn), lambda i, j: (0, 0, j))] + gate_specs,
        out_specs=pl.BlockSpec((tm, tn), lambda i, j: (i, j)),
        compiler_params=_params(2, blocks, 3 * tm * tn * 4),
        name="gated_merge",
    )(o_a, o_b, o_c, w_branch, u, u, u)


def _flash(q, k_ref, v_ref, q_start, tq, tk):
    rows = q.shape[0]

    def scores(j):
        ks = k_ref[pl.ds(pl.multiple_of(j * tk, tk), tk), :]
        return lax.dot_general(q, ks, (((1,), (1,)), ((), ())), preferred_element_type=F32)

    def weighted_values(p, j):
        vs = v_ref[pl.ds(pl.multiple_of(j * tk, tk), tk), :]
        return jnp.dot(p.astype(BF16), vs, preferred_element_type=F32)

    jd = q_start // tk
    row = lax.broadcasted_iota(jnp.int32, (rows, tk), 0)
    if rows != tq:
        row = jnp.where(row >= tq, row - tq, row)
    col = lax.broadcasted_iota(jnp.int32, (rows, tk), 1)
    visible = ((jd * tk + col) >> CHUNK_SHIFT) <= ((q_start + row) >> CHUNK_SHIFT)
    s = jnp.where(visible, scores(jd), MASKED)
    m = jnp.max(s, axis=-1, keepdims=True)
    p = jnp.exp(s - m)
    l = jnp.sum(p, axis=-1, keepdims=True)
    acc = weighted_values(p, jd)

    def body(j, carry):
        m, l, acc = carry
        s = scores(j)
        m_new = jnp.maximum(m, jnp.max(s, axis=-1, keepdims=True))
        alpha = jnp.exp(m - m_new)
        p = jnp.exp(s - m_new)
        l = alpha * l + jnp.sum(p, axis=-1, keepdims=True)
        acc = alpha * acc + weighted_values(p, j)
        return m_new, l, acc

    _, l, acc = lax.fori_loop(0, jd, body, (m, l, acc))
    return l, acc


def _prep_a_kernel(cq_ref, ckv_ref, kr_ref, gcq_ref, gckv_ref, wuq_ref, wukv_ref, gq_ref, gk_ref, rope_ref,
                   q_ref, k_ref, v_ref, *, heads, scale):
    rope = rope_ref[...]
    cos, sin_lo, sin_hi = rope[:, :LANES], rope[:, LANES:2 * LANES], rope[:, 2 * LANES:]

    def rotate(y):
        return y * cos + pltpu.roll(y, LANES - A_ROPE // 2, 1) * sin_lo + pltpu.roll(y, A_ROPE // 2, 1) * sin_hi

    cq = cq_ref[...].astype(F32)
    cq = _rms(cq, gcq_ref[...], cq.shape[-1]).astype(BF16)
    qa = jnp.dot(cq, wuq_ref[...], preferred_element_type=F32)
    ckv = ckv_ref[...].astype(F32)
    ckv = _rms(ckv, gckv_ref[...], ckv.shape[-1]).astype(BF16)
    kva = jnp.dot(ckv, wukv_ref[...], preferred_element_type=F32)
    gq = gq_ref[...]
    gk = gk_ref[...]
    k_rope = rotate(_rms(kr_ref[...], gk[:, LANES:], A_ROPE)).astype(BF16)
    hw = 2 * LANES
    for h in range(heads):
        q_nope = _rms(qa[:, h * hw:h * hw + LANES], gq[:, :LANES], A_NOPE)
        q_rope = rotate(_rms(qa[:, h * hw + LANES:(h + 1) * hw], gq[:, LANES:], A_ROPE))
        q_ref[:, h * hw:h * hw + LANES] = (q_nope * scale).astype(BF16)
        q_ref[:, h * hw + LANES:(h + 1) * hw] = (q_rope * scale).astype(BF16)
        k_nope = _rms(kva[:, h * hw:h * hw + LANES], gk[:, :LANES], A_NOPE)
        k_ref[:, h * hw:h * hw + LANES] = k_nope.astype(BF16)
        k_ref[:, h * hw + LANES:(h + 1) * hw] = k_rope
        v_ref[:, h * LANES:(h + 1) * LANES] = kva[:, h * hw + LANES:(h + 1) * hw].astype(BF16)


def _prep_a(u, ukr, off_cq, off_ckv, g_cq, g_ckv, w_uq, w_ukv, g_q, g_k, rope, seq, heads, t_rows):
    t = u.shape[0]
    q_rank, kv_rank = w_uq.shape[0], w_ukv.shape[0]
    hw = 2 * LANES
    n_seq = seq // t_rows
    scale = (A_NOPE + A_ROPE) ** -0.5
    row = lambda i: (i, 0)
    const = lambda i: (0, 0)
    blocks = (t_rows * (q_rank + kv_rank) * 2 + t_rows * LANES * 4 + (q_rank + kv_rank) * heads * hw * 2
              + t_rows * 3 * LANES * 4 + t_rows * heads * (2 * hw + LANES) * 2)
    return pl.pallas_call(
        functools.partial(_prep_a_kernel, heads=heads, scale=scale),
        out_shape=(jax.ShapeDtypeStruct((t, heads * hw), BF16),
                   jax.ShapeDtypeStruct((t, heads * hw), BF16),
                   jax.ShapeDtypeStruct((t, heads * LANES), BF16)),
        grid=(t // t_rows,),
        in_specs=[
            pl.BlockSpec((t_rows, q_rank), lambda i: (i, off_cq // q_rank)),
            pl.BlockSpec((t_rows, kv_rank), lambda i: (i, off_ckv // kv_rank)),
            pl.BlockSpec((t_rows, LANES), row),
            pl.BlockSpec((1, q_rank), const),
            pl.BlockSpec((1, kv_rank), const),
            pl.BlockSpec((q_rank, heads * hw), const),
            pl.BlockSpec((kv_rank, heads * hw), const),
            pl.BlockSpec((1, hw), const),
            pl.BlockSpec((1, hw), const),
            pl.BlockSpec((t_rows, 3 * LANES), lambda i: (i % n_seq, 0)),
        ],
        out_specs=(pl.BlockSpec((t_rows, heads * hw), row),
                   pl.BlockSpec((t_rows, heads * hw), row),
                   pl.BlockSpec((t_rows, heads * LANES), row)),
        compiler_params=_params(1, blocks, 4 * t_rows * heads * hw * 4),
        name="mla_prep",
    )(u, u, ukr, g_cq, g_ckv, w_uq, w_ukv, g_q, g_k, rope)


def _attn_a_kernel(q_ref, k_ref, v_ref, z_ref, o_ref, *, tq, tk):
    q_start = pl.program_id(2) * tq
    l, acc = _flash(q_ref[...], k_ref, v_ref, q_start, tq, tk)
    z = z_ref[...].astype(F32)
    o_ref[...] = ((acc / l) * (z * _sigmoid(z))).astype(o_ref.dtype)


def _attn_a(q, k, v, u, z_off, batch, seq, heads, tq, tk):
    t = q.shape[0]
    nq = seq // tq
    dk = q.shape[1] // heads
    zb = z_off // LANES
    blocks = tq * dk * 2 + seq * dk * 2 + seq * LANES * 2 + 2 * tq * LANES * 2
    return pl.pallas_call(
        functools.partial(_attn_a_kernel, tq=tq, tk=tk),
        out_shape=jax.ShapeDtypeStruct((t, heads * LANES), BF16),
        grid=(batch, heads, nq),
        in_specs=[
            pl.BlockSpec((tq, dk), lambda b, h, i: (b * nq + i, h)),
            pl.BlockSpec((seq, dk), lambda b, h, i: (b, h)),
            pl.BlockSpec((seq, LANES), lambda b, h, i: (b, h)),
            pl.BlockSpec((tq, LANES), lambda b, h, i: (b * nq + i, zb + h)),
        ],
        out_specs=pl.BlockSpec((tq, LANES), lambda b, h, i: (b * nq + i, h)),
        compiler_params=_params(3, blocks, 6 * tq * tk * 4),
        name="mla_attention",
    )(q, k, v, u)


def _prep_b_kernel(q_ref, k_ref, gq_ref, gk_ref, rope_ref, qo_ref, ko_ref, *, heads, scale):
    rope = rope_ref[...]
    cos, sin_lo, sin_hi = rope[:, :LANES], rope[:, LANES:2 * LANES], rope[:, 2 * LANES:]
    first_map = lax.broadcasted_iota(jnp.int32, (1, LANES), 1) < B_DK

    def norm_rotate(x, g):
        xsq = x * x
        ss_lo = jnp.sum(jnp.where(first_map, xsq, 0.0), axis=-1, keepdims=True)
        ss_hi = jnp.sum(jnp.where(first_map, 0.0, xsq), axis=-1, keepdims=True)
        y = x * lax.rsqrt(jnp.where(first_map, ss_lo, ss_hi) * (1.0 / B_DK) + EPS) * g
        return y * cos + pltpu.roll(y, LANES - B_ROT // 2, 1) * sin_lo + pltpu.roll(y, B_ROT // 2, 1) * sin_hi

    gq = gq_ref[...]
    gk = gk_ref[...]
    for h in range(heads):
        sl = slice(h * LANES, (h + 1) * LANES)
        qo_ref[:, sl] = (norm_rotate(q_ref[:, sl].astype(F32), gq) * scale).astype(BF16)
        ko_ref[:, sl] = norm_rotate(k_ref[:, sl].astype(F32), gk).astype(BF16)


def _prep_b(u, off_q, off_k, g_q, g_k, rope, seq, heads, t_rows):
    t = u.shape[0]
    w = heads * LANES
    n_seq = seq // t_rows
    const = lambda i: (0, 0)
    blocks = 4 * t_rows * w * 2 + t_rows * 3 * LANES * 4
    return pl.pallas_call(
        functools.partial(_prep_b_kernel, heads=heads, scale=B_DK ** -0.5),
        out_shape=(jax.ShapeDtypeStruct((t, w), BF16), jax.ShapeDtypeStruct((t, w), BF16)),
        grid=(t // t_rows,),
        in_specs=[
            pl.BlockSpec((t_rows, w), lambda i: (i, off_q // w)),
            pl.BlockSpec((t_rows, w), lambda i: (i, off_k // w)),
            pl.BlockSpec((1, LANES), const),
            pl.BlockSpec((1, LANES), const),
            pl.BlockSpec((t_rows, 3 * LANES), lambda i: (i % n_seq, 0)),
        ],
        out_specs=(pl.BlockSpec((t_rows, w), lambda i: (i, 0)), pl.BlockSpec((t_rows, w), lambda i: (i, 0))),
        compiler_params=_params(1, blocks, 8 * t_rows * LANES * 4),
        name="diff_prep",
    )(u, u, g_q, g_k, rope)


def _attn_b_kernel(q_ref, k_ref, v_ref, z_ref, lam_ref, gsub_ref, o_ref, *, tq, tk, lam_init):
    q_start = pl.program_id(2) * tq
    q = q_ref[...]
    first_map = lax.broadcasted_iota(jnp.int32, q.shape, 1) < B_DK
    zeros = jnp.zeros_like(q)
    q2 = jnp.concatenate([jnp.where(first_map, q, zeros), jnp.where(first_map, zeros, q)], axis=0)
    l, acc = _flash(q2, k_ref, v_ref, q_start, tq, tk)
    o = acc / l
    lf = lam_ref[...]
    lam = (jnp.exp(jnp.sum(lf[0:1] * lf[1:2], axis=-1, keepdims=True))
           - jnp.exp(jnp.sum(lf[2:3] * lf[3:4], axis=-1, keepdims=True)) + lam_init)
    a = o[:tq] - lam * o[tq:]
    y = _rms(a, gsub_ref[...], B_DV) * (1.0 - lam_init)
    z = z_ref[...].astype(F32)
    o_ref[...] = (y * (z * _sigmoid(z))).astype(o_ref.dtype)


def _attn_b(q, k, u, v_off, z_off, lam, g_sub, lam_init, batch, seq, heads, tq, tk):
    t = q.shape[0]
    nq = seq // tq
    vb, zb = v_off // LANES, z_off // LANES
    blocks = 3 * tq * LANES * 2 + 2 * seq * LANES * 2
    return pl.pallas_call(
        functools.partial(_attn_b_kernel, tq=tq, tk=tk, lam_init=lam_init),
        out_shape=jax.ShapeDtypeStruct((t, heads * LANES), BF16),
        grid=(batch, heads, nq),
        in_specs=[
            pl.BlockSpec((tq, LANES), lambda b, h, i: (b * nq + i, h)),
            pl.BlockSpec((seq, LANES), lambda b, h, i: (b, h)),
            pl.BlockSpec((seq, LANES), lambda b, h, i: (b, vb + h)),
            pl.BlockSpec((tq, LANES), lambda b, h, i: (b * nq + i, zb + h)),
            pl.BlockSpec(lam.shape, lambda b, h, i: (0, 0)),
            pl.BlockSpec((1, LANES), lambda b, h, i: (0, 0)),
        ],
        out_specs=pl.BlockSpec((tq, LANES), lambda b, h, i: (b * nq + i, h)),
        compiler_params=_params(3, blocks, 12 * tq * tk * 4),
        name="diff_attention",
    )(q, k, u, u, lam, g_sub)


def _prep_c_kernel(q_ref, k_ref, v_ref, gq_ref, gk_ref, qo_ref, ko_ref, vo_ref, *, heads, scale):
    is_pad = pl.program_id(1) == 0
    gq = gq_ref[...]
    gk = gk_ref[...]
    for h in range(heads):
        sl = slice(h * LANES, (h + 1) * LANES)
        qo_ref[:, sl] = (_rms(q_ref[:, sl].astype(F32), gq, C_DH) * scale).astype(BF16)
        kn = _rms(k_ref[:, sl].astype(F32), gk, C_DH).astype(BF16)
        ko_ref[:, sl] = jnp.where(is_pad, jnp.zeros_like(kn), kn)
    v = v_ref[...]
    vo_ref[...] = jnp.where(is_pad, jnp.zeros_like(v), v)


def _prep_c(u, off_q, off_k, off_v, g_q, g_k, batch, seq, heads, pad):
    t = u.shape[0]
    w = heads * LANES
    n_seq = seq // pad
    src = lambda col: (lambda b, r: (b * n_seq + jnp.maximum(r - 1, 0), col))
    dst = lambda b, r: (b * (n_seq + 1) + r, 0)
    const = lambda b, r: (0, 0)
    padded = jax.ShapeDtypeStruct((batch * (seq + pad), w), BF16)
    return pl.pallas_call(
        functools.partial(_prep_c_kernel, heads=heads, scale=C_DH ** -0.5),
        out_shape=(jax.ShapeDtypeStruct((t, w), BF16), padded, padded),
        grid=(batch, n_seq + 1),
        in_specs=[
            pl.BlockSpec((pad, w), src(off_q // w)),
            pl.BlockSpec((pad, w), src(off_k // w)),
            pl.BlockSpec((pad, w), src(off_v // w)),
            pl.BlockSpec((1, LANES), const),
            pl.BlockSpec((1, LANES), const),
        ],
        out_specs=(pl.BlockSpec((pad, w), src(0)), pl.BlockSpec((pad, w), dst), pl.BlockSpec((pad, w), dst)),
        compiler_params=pltpu.CompilerParams(
            dimension_semantics=("parallel", "arbitrary"),
            vmem_limit_bytes=int(2 * 6 * pad * w * 2 + (16 << 20)),
        ),
        name="band_prep",
    )(u, u, u, g_q, g_k)


def _attn_c_kernel(q_ref, k_ref, v_ref, z_ref, bias_ref, o_ref, *, tq, tw, pad):
    start = pl.multiple_of(pl.program_id(2) * tq, tq)
    ks = k_ref[pl.ds(start, tw), :]
    vs = v_ref[pl.ds(start, tw), :]
    s = lax.dot_general(q_ref[...], ks, (((1,), (1,)), ((), ())), preferred_element_type=F32) + bias_ref[0]
    col = lax.broadcasted_iota(jnp.int32, (tq, tw), 1)
    s = jnp.where(col + start >= pad, s, MASKED)
    m = jnp.max(s, axis=-1, keepdims=True)
    p = jnp.exp(s - m)
    l = jnp.sum(p, axis=-1, keepdims=True)
    o = jnp.dot(p.astype(BF16), vs, preferred_element_type=F32) / l
    z = z_ref[...].astype(F32)
    o_ref[...] = (o * (z * _sigmoid(z))).astype(o_ref.dtype)


def _attn_c(q, k, v, u, z_off, bias, batch, seq, heads, tq, pad):
    t = q.shape[0]
    nq = seq // tq
    tw = tq + pad
    zb = z_off // LANES
    blocks = 3 * tq * LANES * 2 + 2 * (seq + pad) * LANES * 2 + tq * tw * 4
    return pl.pallas_call(
        functools.partial(_attn_c_kernel, tq=tq, tw=tw, pad=pad),
        out_shape=jax.ShapeDtypeStruct((t, heads * LANES), BF16),
        grid=(batch, heads, nq),
        in_specs=[
            pl.BlockSpec((tq, LANES), lambda b, h, i: (b * nq + i, h)),
            pl.BlockSpec((seq + pad, LANES), lambda b, h, i: (b, h)),
            pl.BlockSpec((seq + pad, LANES), lambda b, h, i: (b, h)),
            pl.BlockSpec((tq, LANES), lambda b, h, i: (b * nq + i, zb + h)),
            pl.BlockSpec((1, tq, tw), lambda b, h, i: (h, 0, 0)),
        ],
        out_specs=pl.BlockSpec((tq, LANES), lambda b, h, i: (b * nq + i, h)),
        compiler_params=_params(3, blocks, 6 * tq * tw * 4),
        name="band_attention",
    )(q, k, v, u, bias)


def _rope_table(seq, dim, theta, group):
    half = dim // 2
    inv = 1.0 / (jnp.float32(theta) ** (jnp.arange(0, dim, 2, dtype=F32) / dim))
    ang = jnp.arange(seq, dtype=F32)[:, None] * inv[None, :]
    cos, sin = jnp.cos(ang), jnp.sin(ang)
    lane = jnp.arange(LANES) % group
    idx = lane % half
    in_lo = (lane < half)[None, :]
    in_hi = ((lane >= half) & (lane < dim))[None, :]
    c = jnp.where(in_lo | in_hi, cos[:, idx], 1.0 if group < LANES else 0.0)
    s_lo = jnp.where(in_lo, -sin[:, idx], 0.0)
    s_hi = jnp.where(in_hi, sin[:, idx], 0.0)
    return jnp.concatenate([c, s_lo, s_hi], axis=1)


def _band_bias(rel_bias, tq, pad):
    r = jnp.arange(tq)[:, None]
    c = jnp.arange(tq + pad)[None, :]
    rel = jnp.clip(r - (c - pad), -(CHUNK - 1), C_REL_MAX) + (CHUNK - 1)
    qc, kc = r // CHUNK, c // CHUNK - pad // CHUNK
    in_band = (kc <= qc) & (kc >= qc - C_LEFT_CHUNKS)
    return jnp.where(in_band[None], rel_bias.astype(F32)[:, rel], MASKED)


def _row(v):
    return v.astype(F32).reshape(1, -1)


def _pad_lanes(v, width):
    return jnp.pad(v, ((0, 0),) * (v.ndim - 1) + ((0, width - v.shape[-1]),))


def _layer(x2, layer_idx, batch, seq, rope_a, rope_b, g_pre, w_in, a_g_cq, a_g_ckv, a_w_uq, a_w_ukv,
           a_g_q, a_g_k, b_g_q, b_g_k, b_lam, b_g_sub, c_g_q, c_g_k, c_rel_bias, w_branch, w_out):
    d = x2.shape[1]
    bw = w_branch.shape[1]
    q_rank, kv_rank = a_w_uq.shape[0], a_w_ukv.shape[0]
    a_heads, b_heads, c_heads = bw // A_V, bw // B_DV, bw // C_DH

    widths = (q_rank, kv_rank, A_ROPE, bw, bw, bw, bw, bw, bw, bw, bw, bw, N_BRANCH * d)
    starts = [0]
    for wd in widths:
        starts.append(starts[-1] + wd)
    (s_cq, s_ckv, s_kr, s_az, s_bq, s_bk, s_bv, s_bz, s_cq_, s_ck, s_cv, s_cz, s_g) = starts[:-1]
    cols = lambda s, wd: w_in[:, s:s + wd]
    order = [(s_bq, bw), (s_bk, bw), (s_bv, bw), (s_bz, bw), (s_cq_, bw), (s_ck, bw), (s_cv, bw), (s_cz, bw),
             (s_az, bw), (s_g, N_BRANCH * d), (s_cq, q_rank), (s_ckv, kv_rank)]
    w_main = jnp.concatenate([cols(s, wd) for s, wd in order], axis=1).astype(BF16)
    offs = [0]
    for _, wd in order:
        offs.append(offs[-1] + wd)
    (o_bq, o_bk, o_bv, o_bz, o_cq, o_ck, o_cv, o_cz, o_az, o_g, o_acq, o_ackv) = offs[:-1]
    w_kr = _pad_lanes(cols(s_kr, A_ROPE), LANES).astype(BF16)

    h = _rmsnorm(x2, _row(g_pre))
    n_main = w_main.shape[1]
    tn_main = max(c for c in (2304, 1792, 1152, 896, 768, 512, 256, 128) if n_main % c == 0)
    u = _matmul(h, w_main, BF16, 1024, tn_main, "in_proj")
    ukr = _matmul(h, w_kr, F32, 1024, LANES, "in_proj_rope_key")

    hw = 2 * LANES
    w_uq = _pad_lanes(a_w_uq.reshape(q_rank, a_heads, A_NOPE + A_ROPE), hw).reshape(q_rank, a_heads * hw)
    g_q = jnp.concatenate([_row(a_g_q[:A_NOPE]), _pad_lanes(_row(a_g_q[A_NOPE:]), LANES)], axis=1)
    g_k = jnp.concatenate([_row(a_g_k[:A_NOPE]), _pad_lanes(_row(a_g_k[A_NOPE:]), LANES)], axis=1)
    qa, ka, va = _prep_a(u, ukr, o_acq, o_ackv, _row(a_g_cq), _row(a_g_ckv), w_uq.astype(BF16),
                         a_w_ukv.astype(BF16), g_q, g_k, rope_a, seq, a_heads, 512)
    o_a = _attn_a(qa, ka, va, u, o_az, batch, seq, a_heads, 512, 512)

    qb, kb = _prep_b(u, o_bq, o_bk, jnp.tile(_row(b_g_q), (1, 2)), jnp.tile(_row(b_g_k), (1, 2)), rope_b,
                     seq, b_heads, 512)
    lam_init = 0.8 - 0.6 * math.exp(-0.3 * layer_idx)
    o_b = _attn_b(qb, kb, u, o_bv, o_bz, b_lam.astype(F32), _row(b_g_sub), lam_init, batch, seq, b_heads, 256, 512)

    pad = C_LEFT_CHUNKS * CHUNK
    tq_c = 256
    qc, kc, vc = _prep_c(u, o_cq, o_ck, o_cv, _row(c_g_q), _row(c_g_k), batch, seq, c_heads, pad)
    o_c = _attn_c(qc, kc, vc, u, o_cz, _band_bias(c_rel_bias, tq_c, pad), batch, seq, c_heads, tq_c, pad)

    y = _merge(o_a, o_b, o_c, w_branch.astype(BF16), u, o_g, 1024, 1024)
    return _matmul_residual(y, w_out.astype(BF16), x2, 1024, 1024)


def kernel(x, g_pre, w_in, a_g_cq, a_g_ckv, a_w_uq, a_w_ukv, a_g_q, a_g_k, b_g_q, b_g_k, b_lam, b_g_sub,
           c_g_q, c_g_k, c_rel_bias, w_branch, w_out):
    batch, seq, d = x.shape
    rope_a = _rope_table(seq, A_ROPE, A_ROPE_THETA, LANES)
    rope_b = _rope_table(seq, B_ROT, B_ROPE_THETA, B_DK)
    x2 = x.reshape(batch * seq, d)
    for l in range(g_pre.shape[0]):
        x2 = _layer(x2, l, batch, seq, rope_a, rope_b, g_pre[l], w_in[l], a_g_cq[l], a_g_ckv[l], a_w_uq[l],
                    a_w_ukv[l], a_g_q[l], a_g_k[l], b_g_q[l], b_g_k[l], b_lam[l], b_g_sub[l], c_g_q[l],
                    c_g_k[l], c_rel_bias[l], w_branch[l], w_out[l])
    return x2.reshape(batch, seq, d)
```

```python
import functools
import math

import jax
import jax.numpy as jnp
from jax import lax
from jax.experimental import pallas as pl
from jax.experimental.pallas import tpu as pltpu

F32 = jnp.float32
BF16 = jnp.bfloat16

EPS = 1e-6
CHUNK = 64
MASKED = -1e30
LOG2E = math.log2(math.e)

A_NOPE, A_ROPE, A_V = 128, 64, 128
A_ROPE_THETA = 10000.0
B_DK, B_DV = 64, 128
B_ROT = B_DK // 4
B_ROPE_THETA = 500000.0
C_DH = 128
C_LEFT_CHUNKS = 8
C_REL_MAX = 128
N_BRANCH = 3

LANES = 128
V7X_VMEM_BUDGET = 56 * 2**20


def _params(semantics, block_bytes, temp_bytes=0):
    need = 2 * block_bytes + temp_bytes + (4 << 20)
    return pltpu.CompilerParams(
        dimension_semantics=semantics,
        vmem_limit_bytes=int(min(max(need, 16 << 20), V7X_VMEM_BUDGET)),
    )


def _sigmoid(z):
    return 1.0 / (1.0 + jnp.exp(-z))


def _rms(x, g, n):
    ss = jnp.sum(x * x, axis=-1, keepdims=True) * (1.0 / n)
    return x * lax.rsqrt(ss + EPS) * g


def _rmsnorm_kernel(x_ref, g_ref, o_ref):
    x = x_ref[...]
    o_ref[...] = _rms(x, g_ref[...], x.shape[-1]).astype(o_ref.dtype)


def _rmsnorm(x, g, tm=512):
    t, d = x.shape
    return pl.pallas_call(
        _rmsnorm_kernel,
        out_shape=jax.ShapeDtypeStruct((t, d), BF16),
        grid=(t // tm,),
        in_specs=[pl.BlockSpec((tm, d), lambda i: (i, 0)), pl.BlockSpec((1, d), lambda i: (0, 0))],
        out_specs=pl.BlockSpec((tm, d), lambda i: (i, 0)),
        compiler_params=_params(("parallel",), tm * d * 6, tm * d * 8),
        name="pre_rmsnorm",
    )(x, g)


def _mm_kernel(a_ref, b_ref, o_ref):
    o_ref[...] = jnp.dot(a_ref[...], b_ref[...], preferred_element_type=F32).astype(o_ref.dtype)


def _matmul(a, b, out_dtype, tm, tn, name):
    m, k = a.shape
    n = b.shape[1]
    blocks = tm * k * 2 + k * tn * 2 + tm * tn * jnp.dtype(out_dtype).itemsize
    return pl.pallas_call(
        _mm_kernel,
        out_shape=jax.ShapeDtypeStruct((m, n), out_dtype),
        grid=(m // tm, n // tn),
        in_specs=[pl.BlockSpec((tm, k), lambda i, j: (i, 0)), pl.BlockSpec((k, tn), lambda i, j: (0, j))],
        out_specs=pl.BlockSpec((tm, tn), lambda i, j: (i, j)),
        compiler_params=_params(("parallel", "parallel"), blocks, tm * tn * 4),
        name=name,
    )(a, b)


def _mm_res_kernel(a_ref, b_ref, x_ref, o_ref):
    o_ref[...] = x_ref[...] + jnp.dot(a_ref[...], b_ref[...], preferred_element_type=F32)


def _matmul_residual(a, b, x, tm, tn):
    m, k = a.shape
    n = b.shape[1]
    blocks = tm * k * 2 + k * tn * 2 + 2 * tm * tn * 4
    return pl.pallas_call(
        _mm_res_kernel,
        out_shape=jax.ShapeDtypeStruct((m, n), F32),
        grid=(m // tm, n // tn),
        in_specs=[
            pl.BlockSpec((tm, k), lambda i, j: (i, 0)),
            pl.BlockSpec((k, tn), lambda i, j: (0, j)),
            pl.BlockSpec((tm, tn), lambda i, j: (i, j)),
        ],
        out_specs=pl.BlockSpec((tm, tn), lambda i, j: (i, j)),
        compiler_params=_params(("parallel", "parallel"), blocks, tm * tn * 4),
        name="out_proj_residual",
    )(a, b, x)


def _merge_kernel(oa_ref, ob_ref, oc_ref, w_ref, ga_ref, gb_ref, gc_ref, y_ref):
    y = None
    for n, (o_ref, g_ref) in enumerate(((oa_ref, ga_ref), (ob_ref, gb_ref), (oc_ref, gc_ref))):
        gate = _sigmoid(g_ref[...].astype(F32))
        term = gate * jnp.dot(o_ref[...], w_ref[n], preferred_element_type=F32)
        y = term if y is None else y + term
    y_ref[...] = y.astype(y_ref.dtype)


def _merge(o_a, o_b, o_c, w_branch, u, gate_off, tm, tn):
    t, bw = o_a.shape
    d = w_branch.shape[2]
    nj = d // tn
    g0 = gate_off // tn
    branch_spec = pl.BlockSpec((tm, bw), lambda i, j: (i, 0))
    gate_specs = [pl.BlockSpec((tm, tn), functools.partial(lambda i, j, n: (i, g0 + n * nj + j), n=n))
                  for n in range(N_BRANCH)]
    blocks = 3 * tm * bw * 2 + N_BRANCH * bw * tn * 2 + 3 * tm * tn * 2 + tm * tn * 2
    return pl.pallas_call(
        _merge_kernel,
        out_shape=jax.ShapeDtypeStruct((t, d), BF16),
        grid=(t // tm, nj),
        in_specs=[branch_spec, branch_spec, branch_spec,
                  pl.BlockSpec((N_BRANCH, bw, tn), lambda i, j: (0, 0, j))] + gate_specs,
        out_specs=pl.BlockSpec((tm, tn), lambda i, j: (i, j)),
        compiler_params=_params(("parallel", "parallel"), blocks, 3 * tm * tn * 4),
        name="gated_merge",
    )(o_a, o_b, o_c, w_branch, u, u, u)


def _causal_tile_mask(t, cols):
    key_chunk = jnp.arange(t)[:, None] // CHUNK
    query_chunk = (jnp.arange(cols)[None, :] % t) // CHUNK
    boundary = jnp.where(key_chunk <= query_chunk, 0.0, MASKED).astype(F32)
    return jnp.stack([jnp.zeros_like(boundary), boundary])


def _flash_loop(load_q, k_ref, vt_ref, mask_ref, s_refs, p_refs, acc_ref, l_ref, *, t, n_tiles, cols):
    def scores(qi, j):
        k = k_ref[pl.ds(pl.multiple_of(j * t, t), t), :]
        s = lax.dot_general(k, load_q(qi), (((1,), (1,)), ((), ())), preferred_element_type=F32)
        return s + mask_ref[(j == qi).astype(jnp.int32)]

    def step(half, carry):
        qi, j, m, l, acc, alpha_prev, qi_prev, j_prev = carry
        last = j == qi
        qi_next = jnp.where(last, jnp.minimum(qi + 1, n_tiles - 1), qi)
        j_next = jnp.where(last, 0, j + 1)
        s_refs[1 - half][...] = scores(qi_next, j_next)
        s = s_refs[half][...]
        m_new = jnp.maximum(m, jnp.max(s, axis=0, keepdims=True))
        alpha = jnp.exp2(m - m_new)
        p = jnp.exp2(s - m_new)
        p_refs[half][...] = p.astype(BF16)
        l_new = alpha * l + jnp.sum(p, axis=0, keepdims=True)
        acc = alpha_prev * acc + jnp.dot(vt_ref[0, 0, j_prev], p_refs[1 - half][...],
                                         preferred_element_type=F32)
        acc_ref[qi_prev] = acc
        l_ref[qi_prev] = l
        return qi_next, j_next, jnp.where(last, MASKED, m_new), l_new, acc, alpha, qi, j

    zero = jnp.int32(0)
    s_refs[0][...] = scores(zero, zero)
    p_refs[1][...] = jnp.zeros(p_refs[1].shape, BF16)
    row = jnp.zeros((1, cols), F32)
    carry = (zero, zero, row + MASKED, row, jnp.zeros(acc_ref.shape[1:], F32), row, zero, zero)
    n_steps = n_tiles * (n_tiles + 1) // 2 + 1
    carry = lax.fori_loop(0, n_steps // 2, lambda _, c: step(1, step(0, c)), carry)
    if n_steps % 2:
        step(0, carry)


def _silu_gate_store(o_ref, z_ref, rows, y):
    z = z_ref[rows, :].astype(F32)
    o_ref[rows, :] = (y * (z * _sigmoid(z))).astype(o_ref.dtype)


def _prep_a_kernel(cq_ref, ckv_ref, kr_ref, gcq_ref, gckv_ref, wuq_ref, wukv_ref, gq_ref, gk_ref, rope_ref,
                   q_ref, k_ref, vt_ref, *, heads, scale):
    rope = rope_ref[...]
    cos, sin_lo, sin_hi = rope[:, :LANES], rope[:, LANES:2 * LANES], rope[:, 2 * LANES:]

    def rotate(y):
        return y * cos + pltpu.roll(y, LANES - A_ROPE // 2, 1) * sin_lo + pltpu.roll(y, A_ROPE // 2, 1) * sin_hi

    cq = cq_ref[...].astype(F32)
    cq = _rms(cq, gcq_ref[...], cq.shape[-1]).astype(BF16)
    qa = jnp.dot(cq, wuq_ref[...], preferred_element_type=F32)
    ckv = ckv_ref[...].astype(F32)
    ckv = _rms(ckv, gckv_ref[...], ckv.shape[-1]).astype(BF16)
    kva = jnp.dot(ckv, wukv_ref[...], preferred_element_type=F32)
    gq = gq_ref[...]
    gk = gk_ref[...]
    k_rope = rotate(_rms(kr_ref[...], gk[:, LANES:], A_ROPE)).astype(BF16)
    hw = 2 * LANES
    for h in range(heads):
        q_nope = _rms(qa[:, h * hw:h * hw + LANES], gq[:, :LANES], A_NOPE)
        q_rope = rotate(_rms(qa[:, h * hw + LANES:(h + 1) * hw], gq[:, LANES:], A_ROPE))
        q_ref[:, h * hw:h * hw + LANES] = (q_nope * scale).astype(BF16)
        q_ref[:, h * hw + LANES:(h + 1) * hw] = (q_rope * scale).astype(BF16)
        k_nope = _rms(kva[:, h * hw:h * hw + LANES], gk[:, :LANES], A_NOPE)
        k_ref[:, h * hw:h * hw + LANES] = k_nope.astype(BF16)
        k_ref[:, h * hw + LANES:(h + 1) * hw] = k_rope
        vt_ref[0, h, 0] = kva[:, h * hw + LANES:(h + 1) * hw].T.astype(BF16)


def _prep_a(u, ukr, off_cq, off_ckv, g_cq, g_ckv, w_uq, w_ukv, g_q, g_k, rope, batch, seq, heads, t_rows):
    t = u.shape[0]
    q_rank, kv_rank = w_uq.shape[0], w_ukv.shape[0]
    hw = 2 * LANES
    n_seq = seq // t_rows
    scale = (A_NOPE + A_ROPE) ** -0.5 * LOG2E
    row = lambda i: (i, 0)
    const = lambda i: (0, 0)
    blocks = (t_rows * (q_rank + kv_rank) * 2 + t_rows * LANES * 4 + (q_rank + kv_rank) * heads * hw * 2
              + t_rows * 3 * LANES * 4 + t_rows * heads * (2 * hw + LANES) * 2)
    return pl.pallas_call(
        functools.partial(_prep_a_kernel, heads=heads, scale=scale),
        out_shape=(jax.ShapeDtypeStruct((t, heads * hw), BF16),
                   jax.ShapeDtypeStruct((t, heads * hw), BF16),
                   jax.ShapeDtypeStruct((batch, heads, n_seq, A_V, t_rows), BF16)),
        grid=(t // t_rows,),
        in_specs=[
            pl.BlockSpec((t_rows, q_rank), lambda i: (i, off_cq // q_rank)),
            pl.BlockSpec((t_rows, kv_rank), lambda i: (i, off_ckv // kv_rank)),
            pl.BlockSpec((t_rows, LANES), row),
            pl.BlockSpec((1, q_rank), const),
            pl.BlockSpec((1, kv_rank), const),
            pl.BlockSpec((q_rank, heads * hw), const),
            pl.BlockSpec((kv_rank, heads * hw), const),
            pl.BlockSpec((1, hw), const),
            pl.BlockSpec((1, hw), const),
            pl.BlockSpec((t_rows, 3 * LANES), lambda i: (i % n_seq, 0)),
        ],
        out_specs=(pl.BlockSpec((t_rows, heads * hw), row),
                   pl.BlockSpec((t_rows, heads * hw), row),
                   pl.BlockSpec((1, heads, 1, A_V, t_rows), lambda i: (i // n_seq, 0, i % n_seq, 0, 0))),
        compiler_params=_params(("parallel",), blocks, 4 * t_rows * heads * hw * 4),
        name="mla_prep",
    )(u, u, ukr, g_cq, g_ckv, w_uq, w_ukv, g_q, g_k, rope)


def _attn_a_kernel(q_ref, k_ref, vt_ref, z_ref, mask_ref, o_ref, s0_ref, s1_ref, p0_ref, p1_ref, acc_ref, l_ref,
                   *, t, n_tiles):
    def load_q(qi):
        return q_ref[pl.ds(pl.multiple_of(qi * t, t), t), :]

    _flash_loop(load_q, k_ref, vt_ref, mask_ref, (s0_ref, s1_ref), (p0_ref, p1_ref), acc_ref, l_ref,
                t=t, n_tiles=n_tiles, cols=t)
    for qi in range(n_tiles):
        _silu_gate_store(o_ref, z_ref, slice(qi * t, (qi + 1) * t), (acc_ref[qi] * (1.0 / l_ref[qi])).T)


def _attn_a(q, k, vt, u, z_off, batch, seq, heads, t):
    n_tiles = seq // t
    dk = q.shape[1] // heads
    zb = z_off // LANES
    mask = _causal_tile_mask(t, t)
    seq_head = lambda b, h: (b, h)
    blocks = 2 * seq * dk * 2 + 3 * seq * LANES * 2 + 2 * t * t * 4
    return pl.pallas_call(
        functools.partial(_attn_a_kernel, t=t, n_tiles=n_tiles),
        out_shape=jax.ShapeDtypeStruct((batch * seq, heads * LANES), BF16),
        grid=(batch, heads),
        in_specs=[
            pl.BlockSpec((seq, dk), seq_head),
            pl.BlockSpec((seq, dk), seq_head),
            pl.BlockSpec((1, 1, n_tiles, A_V, t), lambda b, h: (b, h, 0, 0, 0)),
            pl.BlockSpec((seq, LANES), lambda b, h: (b, zb + h)),
            pl.BlockSpec(mask.shape, lambda b, h: (0, 0, 0)),
        ],
        out_specs=pl.BlockSpec((seq, LANES), seq_head),
        scratch_shapes=[pltpu.VMEM((t, t), F32), pltpu.VMEM((t, t), F32),
                        pltpu.VMEM((t, t), BF16), pltpu.VMEM((t, t), BF16),
                        pltpu.VMEM((n_tiles, A_V, t), F32), pltpu.VMEM((n_tiles, 1, t), F32)],
        compiler_params=_params(("parallel", "parallel"), blocks, 8 * t * t * 4 + n_tiles * A_V * t * 4),
        name="mla_attention",
    )(q, k, vt, u, mask)


def _prep_b_kernel(q_ref, k_ref, v_ref, gq_ref, gk_ref, rope_ref, qo_ref, ko_ref, vt_ref, *, heads, scale):
    rope = rope_ref[...]
    cos, sin_lo, sin_hi = rope[:, :LANES], rope[:, LANES:2 * LANES], rope[:, 2 * LANES:]
    first_map = lax.broadcasted_iota(jnp.int32, (1, LANES), 1) < B_DK

    def norm_rotate(x, g):
        xsq = x * x
        ss_lo = jnp.sum(jnp.where(first_map, xsq, 0.0), axis=-1, keepdims=True)
        ss_hi = jnp.sum(jnp.where(first_map, 0.0, xsq), axis=-1, keepdims=True)
        y = x * lax.rsqrt(jnp.where(first_map, ss_lo, ss_hi) * (1.0 / B_DK) + EPS) * g
        return y * cos + pltpu.roll(y, LANES - B_ROT // 2, 1) * sin_lo + pltpu.roll(y, B_ROT // 2, 1) * sin_hi

    gq = gq_ref[...]
    gk = gk_ref[...]
    rows = q_ref.shape[0]
    for h in range(heads):
        sl = slice(h * LANES, (h + 1) * LANES)
        q = (norm_rotate(q_ref[:, sl].astype(F32), gq) * scale).astype(BF16)
        zeros = jnp.zeros_like(q)
        qo_ref[:rows, sl] = jnp.where(first_map, q, zeros)
        qo_ref[rows:, sl] = jnp.where(first_map, zeros, q)
        ko_ref[:, sl] = norm_rotate(k_ref[:, sl].astype(F32), gk).astype(BF16)
        vt_ref[0, h, 0] = v_ref[:, sl].astype(F32).T.astype(BF16)


def _prep_b(u, off_q, off_k, off_v, g_q, g_k, rope, batch, seq, heads, t_rows):
    t = u.shape[0]
    w = heads * LANES
    n_seq = seq // t_rows
    const = lambda i: (0, 0)
    blocks = 7 * t_rows * w * 2 + t_rows * 3 * LANES * 4
    return pl.pallas_call(
        functools.partial(_prep_b_kernel, heads=heads, scale=B_DK ** -0.5 * LOG2E),
        out_shape=(jax.ShapeDtypeStruct((2 * t, w), BF16), jax.ShapeDtypeStruct((t, w), BF16),
                   jax.ShapeDtypeStruct((batch, heads, n_seq, B_DV, t_rows), BF16)),
        grid=(t // t_rows,),
        in_specs=[
            pl.BlockSpec((t_rows, w), lambda i: (i, off_q // w)),
            pl.BlockSpec((t_rows, w), lambda i: (i, off_k // w)),
            pl.BlockSpec((t_rows, w), lambda i: (i, off_v // w)),
            pl.BlockSpec((1, LANES), const),
            pl.BlockSpec((1, LANES), const),
            pl.BlockSpec((t_rows, 3 * LANES), lambda i: (i % n_seq, 0)),
        ],
        out_specs=(pl.BlockSpec((2 * t_rows, w), lambda i: (i, 0)), pl.BlockSpec((t_rows, w), lambda i: (i, 0)),
                   pl.BlockSpec((1, heads, 1, B_DV, t_rows), lambda i: (i // n_seq, 0, i % n_seq, 0, 0))),
        compiler_params=_params(("parallel",), blocks, 12 * t_rows * LANES * 4),
        name="diff_prep",
    )(u, u, u, g_q, g_k, rope)


def _attn_b_kernel(q_ref, k_ref, vt_ref, z_ref, mask_ref, lam_ref, gsub_ref, o_ref, s0_ref, s1_ref, p0_ref, p1_ref,
                   acc_ref, l_ref, *, t, n_tiles, lam_init):
    def load_q(qi):
        return q_ref[pl.ds(pl.multiple_of(qi * 2 * t, 2 * t), 2 * t), :]

    _flash_loop(load_q, k_ref, vt_ref, mask_ref, (s0_ref, s1_ref), (p0_ref, p1_ref), acc_ref, l_ref,
                t=t, n_tiles=n_tiles, cols=2 * t)
    lf = lam_ref[...]
    lam = (jnp.exp(jnp.sum(lf[0:1] * lf[1:2], axis=-1, keepdims=True))
           - jnp.exp(jnp.sum(lf[2:3] * lf[3:4], axis=-1, keepdims=True)) + lam_init)
    gsub = gsub_ref[...]
    for qi in range(n_tiles):
        o = acc_ref[qi] * (1.0 / l_ref[qi])
        a = o[:, :t] - lam * o[:, t:]
        ss = jnp.sum(a * a, axis=0, keepdims=True) * (1.0 / B_DV)
        y = a * lax.rsqrt(ss + EPS) * gsub * (1.0 - lam_init)
        _silu_gate_store(o_ref, z_ref, slice(qi * t, (qi + 1) * t), y.T)


def _attn_b(q, k, vt, u, z_off, lam, g_sub, lam_init, batch, seq, heads, t):
    n_tiles = seq // t
    zb = z_off // LANES
    mask = _causal_tile_mask(t, 2 * t)
    seq_head = lambda b, h: (b, h)
    blocks = seq * 2 * LANES * 2 + 4 * seq * LANES * 2 + 2 * t * 2 * t * 4
    return pl.pallas_call(
        functools.partial(_attn_b_kernel, t=t, n_tiles=n_tiles, lam_init=lam_init),
        out_shape=jax.ShapeDtypeStruct((batch * seq, heads * LANES), BF16),
        grid=(batch, heads),
        in_specs=[
            pl.BlockSpec((2 * seq, LANES), seq_head),
            pl.BlockSpec((seq, LANES), seq_head),
            pl.BlockSpec((1, 1, n_tiles, B_DV, t), lambda b, h: (b, h, 0, 0, 0)),
            pl.BlockSpec((seq, LANES), lambda b, h: (b, zb + h)),
            pl.BlockSpec(mask.shape, lambda b, h: (0, 0, 0)),
            pl.BlockSpec(lam.shape, lambda b, h: (0, 0)),
            pl.BlockSpec((B_DV, 1), lambda b, h: (0, 0)),
        ],
        out_specs=pl.BlockSpec((seq, LANES), seq_head),
        scratch_shapes=[pltpu.VMEM((t, 2 * t), F32), pltpu.VMEM((t, 2 * t), F32),
                        pltpu.VMEM((t, 2 * t), BF16), pltpu.VMEM((t, 2 * t), BF16),
                        pltpu.VMEM((n_tiles, B_DV, 2 * t), F32), pltpu.VMEM((n_tiles, 1, 2 * t), F32)],
        compiler_params=_params(("parallel", "parallel"), blocks, 16 * t * t * 4 + n_tiles * B_DV * 2 * t * 4),
        name="diff_attention",
    )(q, k, vt, u, mask, lam, g_sub)


def _prep_c_kernel(q_ref, k_ref, v_ref, gq_ref, gk_ref, qo_ref, ko_ref, vo_ref, *, heads, scale):
    is_pad = pl.program_id(1) == 0
    gq = gq_ref[...]
    gk = gk_ref[...]
    for h in range(heads):
        sl = slice(h * LANES, (h + 1) * LANES)
        qo_ref[:, sl] = (_rms(q_ref[:, sl].astype(F32), gq, C_DH) * scale).astype(BF16)
        kn = _rms(k_ref[:, sl].astype(F32), gk, C_DH).astype(BF16)
        ko_ref[:, sl] = jnp.where(is_pad, jnp.zeros_like(kn), kn)
    v = v_ref[...]
    vo_ref[...] = jnp.where(is_pad, jnp.zeros_like(v), v)


def _prep_c(u, off_q, off_k, off_v, g_q, g_k, batch, seq, heads, pad):
    t = u.shape[0]
    w = heads * LANES
    n_seq = seq // pad
    src = lambda col: (lambda b, r: (b * n_seq + jnp.maximum(r - 1, 0), col))
    dst = lambda b, r: (b * (n_seq + 1) + r, 0)
    const = lambda b, r: (0, 0)
    padded = jax.ShapeDtypeStruct((batch * (seq + pad), w), BF16)
    return pl.pallas_call(
        functools.partial(_prep_c_kernel, heads=heads, scale=C_DH ** -0.5 * LOG2E),
        out_shape=(jax.ShapeDtypeStruct((t, w), BF16), padded, padded),
        grid=(batch, n_seq + 1),
        in_specs=[
            pl.BlockSpec((pad, w), src(off_q // w)),
            pl.BlockSpec((pad, w), src(off_k // w)),
            pl.BlockSpec((pad, w), src(off_v // w)),
            pl.BlockSpec((1, LANES), const),
            pl.BlockSpec((1, LANES), const),
        ],
        out_specs=(pl.BlockSpec((pad, w), src(0)), pl.BlockSpec((pad, w), dst), pl.BlockSpec((pad, w), dst)),
        compiler_params=_params(("parallel", "arbitrary"), 6 * pad * w * 2, 4 * pad * LANES * 4),
        name="band_prep",
    )(u, u, u, g_q, g_k)


def _attn_c_kernel(q_ref, k_ref, v_ref, z_ref, rel_ref, o_ref, bias_ref, *, tq, tw, pad):
    qi = pl.program_id(2)

    @pl.when(qi == 0)
    def _():
        width = rel_ref.shape[-1]
        toeplitz = pltpu.roll(jnp.broadcast_to(rel_ref[0], (tq, width)), 0, 1, stride=1, stride_axis=0)
        q_chunk = lax.broadcasted_iota(jnp.int32, (tq, tw), 0) // CHUNK
        k_chunk = lax.broadcasted_iota(jnp.int32, (tq, tw), 1) // CHUNK - pad // CHUNK
        in_band = (k_chunk <= q_chunk) & (k_chunk >= q_chunk - C_LEFT_CHUNKS)
        bias_ref[...] = jnp.where(in_band, toeplitz[:, :tw] * LOG2E, MASKED)

    start = pl.multiple_of(qi * tq, tq)
    ks = k_ref[pl.ds(start, tw), :]
    vs = v_ref[pl.ds(start, tw), :]
    s = lax.dot_general(q_ref[...], ks, (((1,), (1,)), ((), ())), preferred_element_type=F32) + bias_ref[...]
    col = lax.broadcasted_iota(jnp.int32, (tq, tw), 1)
    s = jnp.where(col + start >= pad, s, MASKED)
    m = jnp.max(s, axis=-1, keepdims=True)
    p = jnp.exp2(s - m)
    l = jnp.sum(p, axis=-1, keepdims=True)
    o = jnp.dot(p.astype(BF16), vs, preferred_element_type=F32) * (1.0 / l)
    z = z_ref[...].astype(F32)
    o_ref[...] = (o * (z * _sigmoid(z))).astype(o_ref.dtype)


def _attn_c(q, k, v, u, z_off, rel_rows, batch, seq, heads, tq, pad):
    t = q.shape[0]
    nq = seq // tq
    tw = tq + pad
    zb = z_off // LANES
    blocks = 3 * tq * LANES * 2 + 2 * (seq + pad) * LANES * 2
    return pl.pallas_call(
        functools.partial(_attn_c_kernel, tq=tq, tw=tw, pad=pad),
        out_shape=jax.ShapeDtypeStruct((t, heads * LANES), BF16),
        grid=(batch, heads, nq),
        in_specs=[
            pl.BlockSpec((tq, LANES), lambda b, h, i: (b * nq + i, h)),
            pl.BlockSpec((seq + pad, LANES), lambda b, h, i: (b, h)),
            pl.BlockSpec((seq + pad, LANES), lambda b, h, i: (b, h)),
            pl.BlockSpec((tq, LANES), lambda b, h, i: (b * nq + i, zb + h)),
            pl.BlockSpec((1, 1, rel_rows.shape[-1]), lambda b, h, i: (h, 0, 0)),
        ],
        out_specs=pl.BlockSpec((tq, LANES), lambda b, h, i: (b * nq + i, h)),
        scratch_shapes=[pltpu.VMEM((tq, tw), F32)],
        compiler_params=_params(("parallel", "parallel", "arbitrary"), blocks, 8 * tq * tw * 4),
        name="band_attention",
    )(q, k, v, u, rel_rows)


def _rope_table(seq, dim, theta, group):
    half = dim // 2
    inv = 1.0 / (jnp.float32(theta) ** (jnp.arange(0, dim, 2, dtype=F32) / dim))
    ang = jnp.arange(seq, dtype=F32)[:, None] * inv[None, :]
    cos, sin = jnp.cos(ang), jnp.sin(ang)
    lane = jnp.arange(LANES) % group
    idx = lane % half
    in_lo = (lane < half)[None, :]
    in_hi = ((lane >= half) & (lane < dim))[None, :]
    c = jnp.where(in_lo | in_hi, cos[:, idx], 1.0 if group < LANES else 0.0)
    s_lo = jnp.where(in_lo, -sin[:, idx], 0.0)
    s_hi = jnp.where(in_hi, sin[:, idx], 0.0)
    return jnp.concatenate([c, s_lo, s_hi], axis=1)


def _band_rel_rows(rel_bias, tq, pad):
    width = pl.next_power_of_2(2 * tq + pad)
    e = jnp.arange(width)
    e = jnp.where(e < tq + pad, e, e - width)
    rel = jnp.clip(pad - e, -(CHUNK - 1), C_REL_MAX) + (CHUNK - 1)
    return rel_bias.astype(F32)[:, None, rel]


def _row(v):
    return v.astype(F32).reshape(1, -1)


def _pad_lanes(v, width):
    return jnp.pad(v, ((0, 0),) * (v.ndim - 1) + ((0, width - v.shape[-1]),))


def _layer(x2, layer_idx, batch, seq, rope_a, rope_b, g_pre, w_in, a_g_cq, a_g_ckv, a_w_uq, a_w_ukv,
           a_g_q, a_g_k, b_g_q, b_g_k, b_lam, b_g_sub, c_g_q, c_g_k, c_rel_bias, w_branch, w_out):
    d = x2.shape[1]
    bw = w_branch.shape[1]
    q_rank, kv_rank = a_w_uq.shape[0], a_w_ukv.shape[0]
    a_heads, b_heads, c_heads = bw // A_V, bw // B_DV, bw // C_DH

    s_kr = q_rank + kv_rank
    s_az = s_kr + A_ROPE
    w_main = jnp.concatenate([w_in[:, s_az:], w_in[:, :s_kr]], axis=1).astype(BF16)
    o_az, o_bq, o_bk, o_bv, o_bz, o_cq, o_ck, o_cv, o_cz, o_g = (n * bw for n in range(10))
    o_acq = o_g + N_BRANCH * d
    o_ackv = o_acq + q_rank
    w_kr = _pad_lanes(w_in[:, s_kr:s_az], LANES).astype(BF16)

    h = _rmsnorm(x2, _row(g_pre))
    n_main = w_main.shape[1]
    tn_main = max(c for c in (2304, 1792, 1152, 896, 768, 512, 256, 128) if n_main % c == 0)
    u = _matmul(h, w_main, BF16, 1024, tn_main, "in_proj")
    ukr = _matmul(h, w_kr, F32, 1024, LANES, "in_proj_rope_key")

    hw = 2 * LANES
    t_a = min(512, seq)
    w_uq = _pad_lanes(a_w_uq.reshape(q_rank, a_heads, A_NOPE + A_ROPE), hw).reshape(q_rank, a_heads * hw)
    g_q = jnp.concatenate([_row(a_g_q[:A_NOPE]), _pad_lanes(_row(a_g_q[A_NOPE:]), LANES)], axis=1)
    g_k = jnp.concatenate([_row(a_g_k[:A_NOPE]), _pad_lanes(_row(a_g_k[A_NOPE:]), LANES)], axis=1)
    qa, ka, vta = _prep_a(u, ukr, o_acq, o_ackv, _row(a_g_cq), _row(a_g_ckv), w_uq.astype(BF16),
                          a_w_ukv.astype(BF16), g_q, g_k, rope_a, batch, seq, a_heads, t_a)
    o_a = _attn_a(qa, ka, vta, u, o_az, batch, seq, a_heads, t_a)

    t_b = min(256, seq)
    qb, kb, vtb = _prep_b(u, o_bq, o_bk, o_bv, jnp.tile(_row(b_g_q), (1, 2)), jnp.tile(_row(b_g_k), (1, 2)),
                          rope_b, batch, seq, b_heads, t_b)
    lam_init = 0.8 - 0.6 * math.exp(-0.3 * layer_idx)
    o_b = _attn_b(qb, kb, vtb, u, o_bz, b_lam.astype(F32), b_g_sub.astype(F32).reshape(B_DV, 1), lam_init,
                  batch, seq, b_heads, t_b)

    pad = C_LEFT_CHUNKS * CHUNK
    tq_c = 256
    qc, kc, vc = _prep_c(u, o_cq, o_ck, o_cv, _row(c_g_q), _row(c_g_k), batch, seq, c_heads, pad)
    o_c = _attn_c(qc, kc, vc, u, o_cz, _band_rel_rows(c_rel_bias, tq_c, pad), batch, seq, c_heads, tq_c, pad)

    y = _merge(o_a, o_b, o_c, w_branch.astype(BF16), u, o_g, 1024, 1024)
    return _matmul_residual(y, w_out.astype(BF16), x2, 1024, 1024)


def kernel(x, g_pre, w_in, a_g_cq, a_g_ckv, a_w_uq, a_w_ukv, a_g_q, a_g_k, b_g_q, b_g_k, b_lam, b_g_sub,
           c_g_q, c_g_k, c_rel_bias, w_branch, w_out):
    batch, seq, d = x.shape
    rope_a = _rope_table(seq, A_ROPE, A_ROPE_THETA, LANES)
    rope_b = _rope_table(seq, B_ROT, B_ROPE_THETA, B_DK)
    x2 = x.reshape(batch * seq, d)
    for l in range(g_pre.shape[0]):
        x2 = _layer(x2, l, batch, seq, rope_a, rope_b, g_pre[l], w_in[l], a_g_cq[l], a_g_ckv[l], a_w_uq[l],
                    a_w_ukv[l], a_g_q[l], a_g_k[l], b_g_q[l], b_g_k[l], b_lam[l], b_g_sub[l], c_g_q[l],
                    c_g_k[l], c_rel_bias[l], w_branch[l], w_out[l])
    return x2.reshape(batch, seq, d)
```

```python
import functools
import math

import jax
import jax.numpy as jnp
from jax import lax
from jax.experimental import pallas as pl
from jax.experimental.pallas import tpu as pltpu

F32 = jnp.float32
BF16 = jnp.bfloat16

EPS = 1e-6
CHUNK = 64
MASKED = -1e30
LOG2E = math.log2(math.e)

A_NOPE, A_ROPE, A_V = 128, 64, 128
A_ROPE_THETA = 10000.0
B_DK, B_DV = 64, 128
B_ROT = B_DK // 4
B_ROPE_THETA = 500000.0
C_DH = 128
C_LEFT_CHUNKS = 8
C_REL_MAX = 128
N_BRANCH = 3

STEPS_PER_ITERATION = 8
LANES = 128
V7X_VMEM_BUDGET = 56 * 2**20


def _params(semantics, block_bytes, temp_bytes=0):
    need = 2 * block_bytes + temp_bytes + (4 << 20)
    return pltpu.CompilerParams(
        dimension_semantics=semantics,
        vmem_limit_bytes=int(min(max(need, 16 << 20), V7X_VMEM_BUDGET)),
    )


def _sigmoid(z):
    return 1.0 / (1.0 + jnp.exp(-z))


def _rms(x, g, n):
    ss = jnp.sum(x * x, axis=-1, keepdims=True) * (1.0 / n)
    return x * lax.rsqrt(ss + EPS) * g


def _rmsnorm_kernel(x_ref, g_ref, o_ref):
    x = x_ref[...]
    o_ref[...] = _rms(x, g_ref[...], x.shape[-1]).astype(o_ref.dtype)


def _rmsnorm(x, g, tm=512):
    t, d = x.shape
    return pl.pallas_call(
        _rmsnorm_kernel,
        out_shape=jax.ShapeDtypeStruct((t, d), BF16),
        grid=(t // tm,),
        in_specs=[pl.BlockSpec((tm, d), lambda i: (i, 0)), pl.BlockSpec((1, d), lambda i: (0, 0))],
        out_specs=pl.BlockSpec((tm, d), lambda i: (i, 0)),
        compiler_params=_params(("parallel",), tm * d * 6, tm * d * 8),
        name="pre_rmsnorm",
    )(x, g)


def _mm_kernel(a_ref, b_ref, o_ref):
    o_ref[...] = jnp.dot(a_ref[...], b_ref[...], preferred_element_type=F32).astype(o_ref.dtype)


def _matmul(a, b, out_dtype, tm, tn, name):
    m, k = a.shape
    n = b.shape[1]
    blocks = tm * k * 2 + k * tn * 2 + tm * tn * jnp.dtype(out_dtype).itemsize
    return pl.pallas_call(
        _mm_kernel,
        out_shape=jax.ShapeDtypeStruct((m, n), out_dtype),
        grid=(m // tm, n // tn),
        in_specs=[pl.BlockSpec((tm, k), lambda i, j: (i, 0)), pl.BlockSpec((k, tn), lambda i, j: (0, j))],
        out_specs=pl.BlockSpec((tm, tn), lambda i, j: (i, j)),
        compiler_params=_params(("parallel", "parallel"), blocks, tm * tn * 4),
        name=name,
    )(a, b)


def _mm_res_kernel(a_ref, b_ref, x_ref, o_ref):
    o_ref[...] = x_ref[...] + jnp.dot(a_ref[...], b_ref[...], preferred_element_type=F32)


def _matmul_residual(a, b, x, tm, tn):
    m, k = a.shape
    n = b.shape[1]
    blocks = tm * k * 2 + k * tn * 2 + 2 * tm * tn * 4
    return pl.pallas_call(
        _mm_res_kernel,
        out_shape=jax.ShapeDtypeStruct((m, n), F32),
        grid=(m // tm, n // tn),
        in_specs=[
            pl.BlockSpec((tm, k), lambda i, j: (i, 0)),
            pl.BlockSpec((k, tn), lambda i, j: (0, j)),
            pl.BlockSpec((tm, tn), lambda i, j: (i, j)),
        ],
        out_specs=pl.BlockSpec((tm, tn), lambda i, j: (i, j)),
        compiler_params=_params(("parallel", "parallel"), blocks, tm * tn * 4),
        name="out_proj_residual",
    )(a, b, x)


def _merge_kernel(oa_ref, ob_ref, oc_ref, w_ref, ga_ref, gb_ref, gc_ref, y_ref):
    y = None
    for n, (o_ref, g_ref) in enumerate(((oa_ref, ga_ref), (ob_ref, gb_ref), (oc_ref, gc_ref))):
        gate = _sigmoid(g_ref[...].astype(F32))
        term = gate * jnp.dot(o_ref[...], w_ref[n], preferred_element_type=F32)
        y = term if y is None else y + term
    y_ref[...] = y.astype(y_ref.dtype)


def _merge(o_a, o_b, o_c, w_branch, u, gate_off, tm, tn):
    t, bw = o_a.shape
    d = w_branch.shape[2]
    nj = d // tn
    g0 = gate_off // tn
    branch_spec = pl.BlockSpec((tm, bw), lambda i, j: (i, 0))
    gate_specs = [pl.BlockSpec((tm, tn), functools.partial(lambda i, j, n: (i, g0 + n * nj + j), n=n))
                  for n in range(N_BRANCH)]
    blocks = 3 * tm * bw * 2 + N_BRANCH * bw * tn * 2 + 3 * tm * tn * 2 + tm * tn * 2
    return pl.pallas_call(
        _merge_kernel,
        out_shape=jax.ShapeDtypeStruct((t, d), BF16),
        grid=(t // tm, nj),
        in_specs=[branch_spec, branch_spec, branch_spec,
                  pl.BlockSpec((N_BRANCH, bw, tn), lambda i, j: (0, 0, j))] + gate_specs,
        out_specs=pl.BlockSpec((tm, tn), lambda i, j: (i, j)),
        compiler_params=_params(("parallel", "parallel"), blocks, 3 * tm * tn * 4),
        name="gated_merge",
    )(o_a, o_b, o_c, w_branch, u, u, u)


def _chunk_indicator(pos0, rows, first_lane):
    chunk = (pos0 + lax.broadcasted_iota(jnp.int32, (rows, LANES), 0)) // CHUNK
    lane = lax.broadcasted_iota(jnp.int32, (rows, LANES), 1)
    return jnp.where(lane - first_lane == chunk, 1.0, 0.0)


def _chunk_mask_rows(pos0, cols, first_row):
    chunk = (pos0 + lax.broadcasted_iota(jnp.int32, (LANES, cols), 1)) // CHUNK
    row = lax.broadcasted_iota(jnp.int32, (LANES, cols), 0)
    return jnp.where((row >= first_row) & (row - first_row > chunk), MASKED, 0.0)


def _flash_loop(qt_ref, k_ref, vt_ref, s_refs, p_refs, acc_ref, l_ref, *, tk, ratio):
    n_q, _, cols = acc_ref.shape

    def scores(qi, j):
        k = k_ref[pl.ds(pl.multiple_of(j * tk, tk), tk), :]
        return jnp.dot(k, qt_ref[0, 0, qi], preferred_element_type=F32)

    def step(half, carry):
        qi, j, m, l, alpha_prev, qi_prev, j_prev = carry
        last = j == qi // ratio
        qi_next = jnp.where(last, jnp.minimum(qi + 1, n_q - 1), qi)
        j_next = jnp.where(last, 0, j + 1)
        s_refs[1 - half][...] = scores(qi_next, j_next)
        s = s_refs[half][...]
        m_new = jnp.maximum(m, jnp.max(s, axis=0, keepdims=True))
        alpha = jnp.exp2(m - m_new)
        p = jnp.exp2(s - m_new)
        p_refs[half][...] = p.astype(BF16)
        l_new = alpha * l + jnp.sum(p, axis=0, keepdims=True)
        half_cols = cols // 2
        for c in range(2):
            cs = slice(c * half_cols, (c + 1) * half_cols)
            acc = alpha_prev[:, cs] * acc_ref[qi_prev, :, cs] + jnp.dot(
                vt_ref[0, 0, j_prev], p_refs[1 - half][:, cs], preferred_element_type=F32)
            acc_ref[qi_prev, :, cs] = acc
        l_ref[qi_prev] = l
        return qi_next, j_next, jnp.where(last, MASKED, m_new), l_new, alpha, qi, j

    def steps(count, carry):
        for i in range(count):
            carry = step(i % 2, carry)
        return carry

    zero = jnp.int32(0)
    s_refs[0][...] = scores(zero, zero)
    p_refs[1][...] = jnp.zeros(p_refs[1].shape, BF16)
    row = jnp.zeros((1, cols), F32)
    acc_ref[...] = jnp.zeros(acc_ref.shape, F32)
    carry = (zero, zero, row + MASKED, row, row, zero, zero)
    n_steps = sum(qi // ratio + 1 for qi in range(n_q)) + 1
    carry = lax.fori_loop(0, n_steps // STEPS_PER_ITERATION, lambda _, c: steps(STEPS_PER_ITERATION, c), carry)
    steps(n_steps % STEPS_PER_ITERATION, carry)


def _silu_gate_store(o_ref, z_ref, rows, y):
    z = z_ref[rows, :].astype(F32)
    o_ref[rows, :] = (y * (z * _sigmoid(z))).astype(o_ref.dtype)


def _prep_a_kernel(cq_ref, ckv_ref, kr_ref, gcq_ref, gckv_ref, wuq_ref, wukv_ref, gq_ref, gk_ref, rope_ref,
                   qt_ref, k_ref, vt_ref, *, heads, scale, n_seq):
    rows = cq_ref.shape[0]
    pos0 = (pl.program_id(0) % n_seq) * rows
    rope = rope_ref[...]
    cos, sin_lo, sin_hi = rope[:, :LANES], rope[:, LANES:2 * LANES], rope[:, 2 * LANES:]

    def rotate(y):
        return y * cos + pltpu.roll(y, LANES - A_ROPE // 2, 1) * sin_lo + pltpu.roll(y, A_ROPE // 2, 1) * sin_hi

    cq = cq_ref[...].astype(F32)
    cq = _rms(cq, gcq_ref[...], cq.shape[-1]).astype(BF16)
    qa = jnp.dot(cq, wuq_ref[...], preferred_element_type=F32)
    ckv = ckv_ref[...].astype(F32)
    ckv = _rms(ckv, gckv_ref[...], ckv.shape[-1]).astype(BF16)
    kva = jnp.dot(ckv, wukv_ref[...], preferred_element_type=F32)
    gq = gq_ref[...]
    gk = gk_ref[...]
    k_rope = (rotate(_rms(kr_ref[...], gk[:, LANES:], A_ROPE)) + _chunk_indicator(pos0, rows, A_ROPE)).astype(BF16)
    mask_rows = _chunk_mask_rows(pos0, rows, A_ROPE)
    hw = 2 * LANES
    for h in range(heads):
        q_nope = _rms(qa[:, h * hw:h * hw + LANES], gq[:, :LANES], A_NOPE)
        q_rope = rotate(_rms(qa[:, h * hw + LANES:(h + 1) * hw], gq[:, LANES:], A_ROPE))
        qt_ref[0, h, 0, :LANES, :] = (q_nope * scale).T.astype(BF16)
        qt_ref[0, h, 0, LANES:, :] = ((q_rope * scale).T + mask_rows).astype(BF16)
        k_nope = _rms(kva[:, h * hw:h * hw + LANES], gk[:, :LANES], A_NOPE)
        k_ref[:, h * hw:h * hw + LANES] = k_nope.astype(BF16)
        k_ref[:, h * hw + LANES:(h + 1) * hw] = k_rope
        vt_ref[0, h, 0] = kva[:, h * hw + LANES:(h + 1) * hw].T.astype(BF16)


def _prep_a(u, ukr, off_cq, off_ckv, g_cq, g_ckv, w_uq, w_ukv, g_q, g_k, rope, batch, seq, heads, t_rows):
    t = u.shape[0]
    q_rank, kv_rank = w_uq.shape[0], w_ukv.shape[0]
    hw = 2 * LANES
    n_seq = seq // t_rows
    assert seq // CHUNK <= LANES - A_ROPE
    scale = (A_NOPE + A_ROPE) ** -0.5 * LOG2E
    row = lambda i: (i, 0)
    const = lambda i: (0, 0)
    tile = lambda i: (i // n_seq, 0, i % n_seq, 0, 0)
    blocks = (t_rows * (q_rank + kv_rank) * 2 + t_rows * LANES * 4 + (q_rank + kv_rank) * heads * hw * 2
              + t_rows * 3 * LANES * 4 + t_rows * heads * (2 * hw + LANES) * 2)
    return pl.pallas_call(
        functools.partial(_prep_a_kernel, heads=heads, scale=scale, n_seq=n_seq),
        out_shape=(jax.ShapeDtypeStruct((batch, heads, n_seq, hw, t_rows), BF16),
                   jax.ShapeDtypeStruct((t, heads * hw), BF16),
                   jax.ShapeDtypeStruct((batch, heads, n_seq, A_V, t_rows), BF16)),
        grid=(t // t_rows,),
        in_specs=[
            pl.BlockSpec((t_rows, q_rank), lambda i: (i, off_cq // q_rank)),
            pl.BlockSpec((t_rows, kv_rank), lambda i: (i, off_ckv // kv_rank)),
            pl.BlockSpec((t_rows, LANES), row),
            pl.BlockSpec((1, q_rank), const),
            pl.BlockSpec((1, kv_rank), const),
            pl.BlockSpec((q_rank, heads * hw), const),
            pl.BlockSpec((kv_rank, heads * hw), const),
            pl.BlockSpec((1, hw), const),
            pl.BlockSpec((1, hw), const),
            pl.BlockSpec((t_rows, 3 * LANES), lambda i: (i % n_seq, 0)),
        ],
        out_specs=(pl.BlockSpec((1, heads, 1, hw, t_rows), tile),
                   pl.BlockSpec((t_rows, heads * hw), row),
                   pl.BlockSpec((1, heads, 1, A_V, t_rows), tile)),
        compiler_params=_params(("parallel",), blocks, 4 * t_rows * heads * hw * 4),
        name="mla_prep",
    )(u, u, ukr, g_cq, g_ckv, w_uq, w_ukv, g_q, g_k, rope)


def _attn_a_kernel(qt_ref, k_ref, vt_ref, z_ref, o_ref, s0_ref, s1_ref, p0_ref, p1_ref, acc_ref, l_ref, *, t):
    _flash_loop(qt_ref, k_ref, vt_ref, (s0_ref, s1_ref), (p0_ref, p1_ref), acc_ref, l_ref, tk=t, ratio=1)
    for qi in range(acc_ref.shape[0]):
        _silu_gate_store(o_ref, z_ref, slice(qi * t, (qi + 1) * t), (acc_ref[qi] * (1.0 / l_ref[qi])).T)


def _attn_a(qt, k, vt, u, z_off, batch, seq, heads, t):
    n_tiles = seq // t
    dk = k.shape[1] // heads
    zb = z_off // LANES
    seq_head = lambda b, h: (b, h)
    tiles = lambda b, h: (b, h, 0, 0, 0)
    blocks = 2 * seq * dk * 2 + 3 * seq * LANES * 2
    scratch = 3 * t * t * 4 + n_tiles * A_V * t * 4
    return pl.pallas_call(
        functools.partial(_attn_a_kernel, t=t),
        out_shape=jax.ShapeDtypeStruct((batch * seq, heads * LANES), BF16),
        grid=(batch, heads),
        in_specs=[
            pl.BlockSpec((1, 1, n_tiles, dk, t), tiles),
            pl.BlockSpec((seq, dk), seq_head),
            pl.BlockSpec((1, 1, n_tiles, A_V, t), tiles),
            pl.BlockSpec((seq, LANES), lambda b, h: (b, zb + h)),
        ],
        out_specs=pl.BlockSpec((seq, LANES), seq_head),
        scratch_shapes=[pltpu.VMEM((t, t), F32), pltpu.VMEM((t, t), F32),
                        pltpu.VMEM((t, t), BF16), pltpu.VMEM((t, t), BF16),
                        pltpu.VMEM((n_tiles, A_V, t), F32), pltpu.VMEM((n_tiles, 1, t), F32)],
        compiler_params=_params(("parallel", "parallel"), blocks, scratch + 6 * t * t * 4),
        name="mla_attention",
    )(qt, k, vt, u)


def _prep_b_kernel(q_ref, k_ref, v_ref, gq_ref, gk_ref, rope_ref, qt_ref, ko_ref, vt_ref, *, heads, scale, n_seq,
                   tq):
    rows = q_ref.shape[0]
    pos0 = (pl.program_id(0) % n_seq) * rows
    rope = rope_ref[...]
    cos, sin_lo, sin_hi = rope[:, :LANES], rope[:, LANES:2 * LANES], rope[:, 2 * LANES:]
    first_map = lax.broadcasted_iota(jnp.int32, (1, LANES), 1) < B_DK

    def norm_rotate(x, g):
        xsq = x * x
        ss_lo = jnp.sum(jnp.where(first_map, xsq, 0.0), axis=-1, keepdims=True)
        ss_hi = jnp.sum(jnp.where(first_map, 0.0, xsq), axis=-1, keepdims=True)
        y = x * lax.rsqrt(jnp.where(first_map, ss_lo, ss_hi) * (1.0 / B_DK) + EPS) * g
        return y * cos + pltpu.roll(y, LANES - B_ROT // 2, 1) * sin_lo + pltpu.roll(y, B_ROT // 2, 1) * sin_hi

    gq = gq_ref[...]
    gk = gk_ref[...]
    indicator = _chunk_indicator(pos0, rows, 0).astype(BF16)
    mask_rows = _chunk_mask_rows(pos0, rows, 0).astype(BF16)
    for h in range(heads):
        sl = slice(h * LANES, (h + 1) * LANES)
        q = norm_rotate(q_ref[:, sl].astype(F32), gq) * scale
        q_lo = jnp.where(first_map, q, 0.0).T.astype(BF16)
        q_hi = jnp.where(first_map, 0.0, q).T.astype(BF16)
        for a in range(rows // tq):
            cols = slice(a * tq, (a + 1) * tq)
            qt_ref[0, h, a, :LANES, :tq] = q_lo[:, cols]
            qt_ref[0, h, a, :LANES, tq:] = q_hi[:, cols]
            qt_ref[0, h, a, LANES:, :tq] = mask_rows[:, cols]
            qt_ref[0, h, a, LANES:, tq:] = mask_rows[:, cols]
        ko_ref[:, 2 * h * LANES:(2 * h + 1) * LANES] = norm_rotate(k_ref[:, sl].astype(F32), gk).astype(BF16)
        ko_ref[:, (2 * h + 1) * LANES:(2 * h + 2) * LANES] = indicator
        vt_ref[0, h, 0] = v_ref[:, sl].astype(F32).T.astype(BF16)


def _prep_b(u, off_q, off_k, off_v, g_q, g_k, rope, batch, seq, heads, tq, tk):
    t = u.shape[0]
    w = heads * LANES
    n_seq = seq // tk
    assert seq // CHUNK <= LANES
    const = lambda i: (0, 0)
    tile = lambda i: (i // n_seq, 0, i % n_seq, 0, 0)
    blocks = 9 * tk * w * 2 + tk * 3 * LANES * 4
    return pl.pallas_call(
        functools.partial(_prep_b_kernel, heads=heads, scale=B_DK ** -0.5 * LOG2E, n_seq=n_seq, tq=tq),
        out_shape=(jax.ShapeDtypeStruct((batch, heads, seq // tq, 2 * LANES, 2 * tq), BF16),
                   jax.ShapeDtypeStruct((t, 2 * w), BF16),
                   jax.ShapeDtypeStruct((batch, heads, n_seq, B_DV, tk), BF16)),
        grid=(t // tk,),
        in_specs=[
            pl.BlockSpec((tk, w), lambda i: (i, off_q // w)),
            pl.BlockSpec((tk, w), lambda i: (i, off_k // w)),
            pl.BlockSpec((tk, w), lambda i: (i, off_v // w)),
            pl.BlockSpec((1, LANES), const),
            pl.BlockSpec((1, LANES), const),
            pl.BlockSpec((tk, 3 * LANES), lambda i: (i % n_seq, 0)),
        ],
        out_specs=(pl.BlockSpec((1, heads, tk // tq, 2 * LANES, 2 * tq), tile),
                   pl.BlockSpec((tk, 2 * w), lambda i: (i, 0)),
                   pl.BlockSpec((1, heads, 1, B_DV, tk), tile)),
        compiler_params=_params(("parallel",), blocks, 16 * tk * LANES * 4),
        name="diff_prep",
    )(u, u, u, g_q, g_k, rope)


def _attn_b_kernel(qt_ref, k_ref, vt_ref, z_ref, lam_ref, gsub_ref, o_ref, s0_ref, s1_ref, p0_ref, p1_ref,
                   acc_ref, l_ref, *, tq, tk, lam_init):
    _flash_loop(qt_ref, k_ref, vt_ref, (s0_ref, s1_ref), (p0_ref, p1_ref), acc_ref, l_ref, tk=tk, ratio=tk // tq)
    lf = lam_ref[...]
    lam = (jnp.exp(jnp.sum(lf[0:1] * lf[1:2], axis=-1, keepdims=True))
           - jnp.exp(jnp.sum(lf[2:3] * lf[3:4], axis=-1, keepdims=True)) + lam_init)
    gsub = gsub_ref[...]
    for qi in range(acc_ref.shape[0]):
        o = acc_ref[qi] * (1.0 / l_ref[qi])
        a = o[:, :tq] - lam * o[:, tq:]
        ss = jnp.sum(a * a, axis=0, keepdims=True) * (1.0 / B_DV)
        y = a * lax.rsqrt(ss + EPS) * gsub * (1.0 - lam_init)
        _silu_gate_store(o_ref, z_ref, slice(qi * tq, (qi + 1) * tq), y.T)


def _attn_b(qt, k, vt, u, z_off, lam, g_sub, lam_init, batch, seq, heads, tq, tk):
    n_q, n_k = seq // tq, seq // tk
    zb = z_off // LANES
    cols = 2 * tq
    seq_head = lambda b, h: (b, h)
    tiles = lambda b, h: (b, h, 0, 0, 0)
    blocks = 2 * seq * 2 * LANES * 2 + seq * 2 * LANES * 2 + 3 * seq * LANES * 2
    scratch = 3 * tk * cols * 4 + n_q * B_DV * cols * 4
    return pl.pallas_call(
        functools.partial(_attn_b_kernel, tq=tq, tk=tk, lam_init=lam_init),
        out_shape=jax.ShapeDtypeStruct((batch * seq, heads * LANES), BF16),
        grid=(batch, heads),
        in_specs=[
            pl.BlockSpec((1, 1, n_q, 2 * LANES, cols), tiles),
            pl.BlockSpec((seq, 2 * LANES), seq_head),
            pl.BlockSpec((1, 1, n_k, B_DV, tk), tiles),
            pl.BlockSpec((seq, LANES), lambda b, h: (b, zb + h)),
            pl.BlockSpec(lam.shape, lambda b, h: (0, 0)),
            pl.BlockSpec((B_DV, 1), lambda b, h: (0, 0)),
        ],
        out_specs=pl.BlockSpec((seq, LANES), seq_head),
        scratch_shapes=[pltpu.VMEM((tk, cols), F32), pltpu.VMEM((tk, cols), F32),
                        pltpu.VMEM((tk, cols), BF16), pltpu.VMEM((tk, cols), BF16),
                        pltpu.VMEM((n_q, B_DV, cols), F32), pltpu.VMEM((n_q, 1, cols), F32)],
        compiler_params=_params(("parallel", "parallel"), blocks, scratch + 6 * tk * cols * 4),
        name="diff_attention",
    )(qt, k, vt, u, lam, g_sub)


C_PAD_SLOT = 64


def _prep_c_kernel(q_ref, k_ref, v_ref, gq_ref, gk_ref, qt_ref, ko_ref, vt_ref, *, heads, scale, tq):
    rows = q_ref.shape[0]
    r = pl.program_id(1)
    is_pad = r == 0
    pos0 = jnp.maximum(r - 1, 0) * rows
    gq = gq_ref[...]
    gk = gk_ref[...]
    lane = lax.broadcasted_iota(jnp.int32, (rows, LANES), 1)
    indicator = jnp.where(is_pad, jnp.where(lane == C_PAD_SLOT, 1.0, 0.0), _chunk_indicator(pos0, rows, 0))
    indicator = indicator.astype(BF16)
    q_chunk = (pos0 + lax.broadcasted_iota(jnp.int32, (LANES, rows), 1)) // CHUNK
    slot = lax.broadcasted_iota(jnp.int32, (LANES, rows), 0)
    out_of_band = ((slot < C_PAD_SLOT) & ((slot > q_chunk) | (slot < q_chunk - C_LEFT_CHUNKS))) | (slot == C_PAD_SLOT)
    mask_rows = jnp.where(out_of_band, MASKED, 0.0).astype(BF16)
    for h in range(heads):
        sl = slice(h * LANES, (h + 1) * LANES)
        qt = (_rms(q_ref[:, sl].astype(F32), gq, C_DH) * scale).T.astype(BF16)
        for a in range(rows // tq):
            qt_ref[0, h, a, :LANES, :] = qt[:, a * tq:(a + 1) * tq]
            qt_ref[0, h, a, LANES:, :] = mask_rows[:, a * tq:(a + 1) * tq]
            vt = v_ref[a * tq:(a + 1) * tq, sl].astype(F32).T.astype(BF16)
            vt_ref[0, h, a] = jnp.where(is_pad, jnp.zeros_like(vt), vt)
        kn = _rms(k_ref[:, sl].astype(F32), gk, C_DH).astype(BF16)
        ko_ref[:, 2 * h * LANES:(2 * h + 1) * LANES] = jnp.where(is_pad, jnp.zeros_like(kn), kn)
        ko_ref[:, (2 * h + 1) * LANES:(2 * h + 2) * LANES] = indicator


def _prep_c(u, off_q, off_k, off_v, g_q, g_k, batch, seq, heads, pad, tq):
    t = u.shape[0]
    w = heads * LANES
    n_seq = seq // pad
    assert seq // CHUNK <= C_PAD_SLOT
    src = lambda col: (lambda b, r: (b * n_seq + jnp.maximum(r - 1, 0), col))
    const = lambda b, r: (0, 0)
    per = pad // tq
    return pl.pallas_call(
        functools.partial(_prep_c_kernel, heads=heads, scale=C_DH ** -0.5 * LOG2E, tq=tq),
        out_shape=(jax.ShapeDtypeStruct((batch, heads, seq // tq, 2 * LANES, tq), BF16),
                   jax.ShapeDtypeStruct((batch * (seq + pad), 2 * w), BF16),
                   jax.ShapeDtypeStruct((batch, heads, (seq + pad) // tq, C_DH, tq), BF16)),
        grid=(batch, n_seq + 1),
        in_specs=[
            pl.BlockSpec((pad, w), src(off_q // w)),
            pl.BlockSpec((pad, w), src(off_k // w)),
            pl.BlockSpec((pad, w), src(off_v // w)),
            pl.BlockSpec((1, LANES), const),
            pl.BlockSpec((1, LANES), const),
        ],
        out_specs=(pl.BlockSpec((1, heads, per, 2 * LANES, tq), lambda b, r: (b, 0, jnp.maximum(r - 1, 0), 0, 0)),
                   pl.BlockSpec((pad, 2 * w), lambda b, r: (b * (n_seq + 1) + r, 0)),
                   pl.BlockSpec((1, heads, per, C_DH, tq), lambda b, r: (b, 0, r, 0, 0))),
        compiler_params=_params(("parallel", "arbitrary"), 9 * pad * w * 2, 16 * pad * LANES * 4),
        name="band_prep",
    )(u, u, u, g_q, g_k)


def _attn_c_kernel(qt_ref, k_ref, vt_ref, z_ref, rel_ref, o_ref, s0_ref, s1_ref, p0_ref, p1_ref, bias_ref, *, tq, tw):
    n_q = qt_ref.shape[2]
    s_refs, p_refs = (s0_ref, s1_ref), (p0_ref, p1_ref)
    width = rel_ref.shape[-1]
    toeplitz = pltpu.roll(jnp.broadcast_to(rel_ref[0], (tq, width)), 0, 1, stride=1, stride_axis=0)
    bias_ref[...] = (toeplitz[:, :tw] * LOG2E).T

    def scores(i):
        return jnp.dot(k_ref[i * tq:i * tq + tw, :], qt_ref[0, 0, i], preferred_element_type=F32) + bias_ref[...]

    s_refs[0][...] = scores(0)
    for i in range(n_q + 1):
        half = i % 2
        if i + 1 < n_q:
            s_refs[1 - half][...] = scores(i + 1)
        if i < n_q:
            s = s_refs[half][...]
            p = jnp.exp2(s - jnp.max(s, axis=0, keepdims=True))
            p_refs[half][...] = p.astype(BF16)
            l = jnp.sum(p, axis=0, keepdims=True)
        if i > 0:
            acc = None
            for a in range(tw // tq):
                part = jnp.dot(vt_ref[0, 0, i - 1 + a], p_refs[1 - half][a * tq:(a + 1) * tq, :],
                               preferred_element_type=F32)
                acc = part if acc is None else acc + part
            _silu_gate_store(o_ref, z_ref, slice((i - 1) * tq, i * tq), (acc * (1.0 / l_prev)).T)
        l_prev = l


def _attn_c(qt, k, vt, u, z_off, rel_rows, batch, seq, heads, tq, pad):
    n_q = seq // tq
    tw = tq + pad
    zb = z_off // LANES
    tiles = lambda b, h: (b, h, 0, 0, 0)
    blocks = seq * 2 * LANES * 2 + (seq + pad) * 3 * LANES * 2 + 2 * seq * LANES * 2
    return pl.pallas_call(
        functools.partial(_attn_c_kernel, tq=tq, tw=tw),
        out_shape=jax.ShapeDtypeStruct((batch * seq, heads * LANES), BF16),
        grid=(batch, heads),
        in_specs=[
            pl.BlockSpec((1, 1, n_q, 2 * LANES, tq), tiles),
            pl.BlockSpec((seq + pad, 2 * LANES), lambda b, h: (b, h)),
            pl.BlockSpec((1, 1, (seq + pad) // tq, C_DH, tq), tiles),
            pl.BlockSpec((seq, LANES), lambda b, h: (b, zb + h)),
            pl.BlockSpec((1, 1, rel_rows.shape[-1]), lambda b, h: (h, 0, 0)),
        ],
        out_specs=pl.BlockSpec((seq, LANES), lambda b, h: (b, h)),
        scratch_shapes=[pltpu.VMEM((tw, tq), F32), pltpu.VMEM((tw, tq), F32),
                        pltpu.VMEM((tw, tq), BF16), pltpu.VMEM((tw, tq), BF16), pltpu.VMEM((tw, tq), F32)],
        compiler_params=_params(("parallel", "parallel"), blocks, 12 * tw * tq * 4),
        name="band_attention",
    )(qt, k, vt, u, rel_rows)


def _rope_table(seq, dim, theta, group):
    half = dim // 2
    inv = 1.0 / (jnp.float32(theta) ** (jnp.arange(0, dim, 2, dtype=F32) / dim))
    ang = jnp.arange(seq, dtype=F32)[:, None] * inv[None, :]
    cos, sin = jnp.cos(ang), jnp.sin(ang)
    lane = jnp.arange(LANES) % group
    idx = lane % half
    in_lo = (lane < half)[None, :]
    in_hi = ((lane >= half) & (lane < dim))[None, :]
    c = jnp.where(in_lo | in_hi, cos[:, idx], 1.0 if group < LANES else 0.0)
    s_lo = jnp.where(in_lo, -sin[:, idx], 0.0)
    s_hi = jnp.where(in_hi, sin[:, idx], 0.0)
    return jnp.concatenate([c, s_lo, s_hi], axis=1)


def _band_rel_rows(rel_bias, tq, pad):
    width = pl.next_power_of_2(2 * tq + pad)
    e = jnp.arange(width)
    e = jnp.where(e < tq + pad, e, e - width)
    rel = jnp.clip(pad - e, -(CHUNK - 1), C_REL_MAX) + (CHUNK - 1)
    return rel_bias.astype(F32)[:, None, rel]


def _row(v):
    return v.astype(F32).reshape(1, -1)


def _pad_lanes(v, width):
    return jnp.pad(v, ((0, 0),) * (v.ndim - 1) + ((0, width - v.shape[-1]),))


def _layer(x2, layer_idx, batch, seq, rope_a, rope_b, g_pre, w_in, a_g_cq, a_g_ckv, a_w_uq, a_w_ukv,
           a_g_q, a_g_k, b_g_q, b_g_k, b_lam, b_g_sub, c_g_q, c_g_k, c_rel_bias, w_branch, w_out):
    d = x2.shape[1]
    bw = w_branch.shape[1]
    q_rank, kv_rank = a_w_uq.shape[0], a_w_ukv.shape[0]
    a_heads, b_heads, c_heads = bw // A_V, bw // B_DV, bw // C_DH

    s_kr = q_rank + kv_rank
    s_az = s_kr + A_ROPE
    w_main = jnp.concatenate([w_in[:, s_az:], w_in[:, :s_kr]], axis=1).astype(BF16)
    o_az, o_bq, o_bk, o_bv, o_bz, o_cq, o_ck, o_cv, o_cz, o_g = (n * bw for n in range(10))
    o_acq = o_g + N_BRANCH * d
    o_ackv = o_acq + q_rank
    w_kr = _pad_lanes(w_in[:, s_kr:s_az], LANES).astype(BF16)

    h = _rmsnorm(x2, _row(g_pre))
    n_main = w_main.shape[1]
    tn_main = max(c for c in (2304, 1792, 1152, 896, 768, 512, 256, 128) if n_main % c == 0)
    u = _matmul(h, w_main, BF16, 1024, tn_main, "in_proj")
    ukr = _matmul(h, w_kr, F32, 1024, LANES, "in_proj_rope_key")

    hw = 2 * LANES
    t_a = min(512, seq)
    w_uq = _pad_lanes(a_w_uq.reshape(q_rank, a_heads, A_NOPE + A_ROPE), hw).reshape(q_rank, a_heads * hw)
    g_q = jnp.concatenate([_row(a_g_q[:A_NOPE]), _pad_lanes(_row(a_g_q[A_NOPE:]), LANES)], axis=1)
    g_k = jnp.concatenate([_row(a_g_k[:A_NOPE]), _pad_lanes(_row(a_g_k[A_NOPE:]), LANES)], axis=1)
    qa, ka, vta = _prep_a(u, ukr, o_acq, o_ackv, _row(a_g_cq), _row(a_g_ckv), w_uq.astype(BF16),
                          a_w_ukv.astype(BF16), g_q, g_k, rope_a, batch, seq, a_heads, t_a)
    o_a = _attn_a(qa, ka, vta, u, o_az, batch, seq, a_heads, t_a)

    tq_b, tk_b = min(256, seq), min(512, seq)
    qb, kb, vtb = _prep_b(u, o_bq, o_bk, o_bv, jnp.tile(_row(b_g_q), (1, 2)), jnp.tile(_row(b_g_k), (1, 2)),
                          rope_b, batch, seq, b_heads, tq_b, tk_b)
    lam_init = 0.8 - 0.6 * math.exp(-0.3 * layer_idx)
    o_b = _attn_b(qb, kb, vtb, u, o_bz, b_lam.astype(F32), b_g_sub.astype(F32).reshape(B_DV, 1), lam_init,
                  batch, seq, b_heads, tq_b, tk_b)

    pad = C_LEFT_CHUNKS * CHUNK
    tq_c = min(256, seq)
    qc, kc, vtc = _prep_c(u, o_cq, o_ck, o_cv, _row(c_g_q), _row(c_g_k), batch, seq, c_heads, pad, tq_c)
    o_c = _attn_c(qc, kc, vtc, u, o_cz, _band_rel_rows(c_rel_bias, tq_c, pad), batch, seq, c_heads, tq_c, pad)

    y = _merge(o_a, o_b, o_c, w_branch.astype(BF16), u, o_g, 1024, 1024)
    return _matmul_residual(y, w_out.astype(BF16), x2, 1024, 1024)


def kernel(x, g_pre, w_in, a_g_cq, a_g_ckv, a_w_uq, a_w_ukv, a_g_q, a_g_k, b_g_q, b_g_k, b_lam, b_g_sub,
           c_g_q, c_g_k, c_rel_bias, w_branch, w_out):
    batch, seq, d = x.shape
    rope_a = _rope_table(seq, A_ROPE, A_ROPE_THETA, LANES)
    rope_b = _rope_table(seq, B_ROT, B_ROPE_THETA, B_DK)
    x2 = x.reshape(batch * seq, d)
    for l in range(g_pre.shape[0]):
        x2 = _layer(x2, l, batch, seq, rope_a, rope_b, g_pre[l], w_in[l], a_g_cq[l], a_g_ckv[l], a_w_uq[l],
                    a_w_ukv[l], a_g_q[l], a_g_k[l], b_g_q[l], b_g_k[l], b_lam[l], b_g_sub[l], c_g_q[l],
                    c_g_k[l], c_rel_bias[l], w_branch[l], w_out[l])
    return x2.reshape(batch, seq, d)
```

```python
import functools
import math

import jax
import jax.numpy as jnp
from jax import lax
from jax.experimental import pallas as pl
from jax.experimental.pallas import tpu as pltpu

F32 = jnp.float32
BF16 = jnp.bfloat16

EPS = 1e-6
CHUNK = 64
MASKED = -1e30
LOG2E = math.log2(math.e)

A_NOPE, A_ROPE, A_V = 128, 64, 128
A_ROPE_THETA = 10000.0
B_DK, B_DV = 64, 128
B_ROT = B_DK // 4
B_ROPE_THETA = 500000.0
C_DH = 128
C_LEFT_CHUNKS = 8
C_REL_MAX = 128
N_BRANCH = 3

STEPS_PER_ITERATION = 8
LANES = 128
V7X_VMEM_BUDGET = 56 * 2**20


def _params(semantics, block_bytes, temp_bytes=0):
    need = 2 * block_bytes + temp_bytes + (4 << 20)
    return pltpu.CompilerParams(
        dimension_semantics=semantics,
        vmem_limit_bytes=int(min(max(need, 16 << 20), V7X_VMEM_BUDGET)),
    )


def _sigmoid(z):
    return 1.0 / (1.0 + jnp.exp(-z))


def _rms(x, g, n):
    ss = jnp.sum(x * x, axis=-1, keepdims=True) * (1.0 / n)
    return x * lax.rsqrt(ss + EPS) * g


def _rmsnorm_kernel(x_ref, g_ref, o_ref, ot_ref):
    x = x_ref[...]
    y = _rms(x, g_ref[...], x.shape[-1])
    o_ref[...] = y.astype(o_ref.dtype)
    ot_ref[...] = y.T.astype(ot_ref.dtype)


def _rmsnorm(x, g, tm=512):
    t, d = x.shape
    return pl.pallas_call(
        _rmsnorm_kernel,
        out_shape=(jax.ShapeDtypeStruct((t, d), BF16), jax.ShapeDtypeStruct((d, t), BF16)),
        grid=(t // tm,),
        in_specs=[pl.BlockSpec((tm, d), lambda i: (i, 0)), pl.BlockSpec((1, d), lambda i: (0, 0))],
        out_specs=(pl.BlockSpec((tm, d), lambda i: (i, 0)), pl.BlockSpec((d, tm), lambda i: (0, i))),
        compiler_params=_params(("parallel",), tm * d * 8, tm * d * 12),
        name="pre_rmsnorm",
    )(x, g)


def _mm_kernel(a_ref, b_ref, o_ref):
    o_ref[...] = jnp.dot(a_ref[...], b_ref[...], preferred_element_type=F32).astype(o_ref.dtype)


def _matmul(a, b, out_dtype, tm, tn, name):
    m, k = a.shape
    n = b.shape[1]
    blocks = tm * k * 2 + k * tn * 2 + tm * tn * jnp.dtype(out_dtype).itemsize
    return pl.pallas_call(
        _mm_kernel,
        out_shape=jax.ShapeDtypeStruct((m, n), out_dtype),
        grid=(m // tm, n // tn),
        in_specs=[pl.BlockSpec((tm, k), lambda i, j: (i, 0)), pl.BlockSpec((k, tn), lambda i, j: (0, j))],
        out_specs=pl.BlockSpec((tm, tn), lambda i, j: (i, j)),
        compiler_params=_params(("parallel", "parallel"), blocks, tm * tn * 4),
        name=name,
    )(a, b)


def _mm_res_kernel(a_ref, b_ref, x_ref, o_ref):
    o_ref[...] = x_ref[...] + jnp.dot(a_ref[...], b_ref[...], preferred_element_type=F32)


def _matmul_residual(a, b, x, tm, tn):
    m, k = a.shape
    n = b.shape[1]
    blocks = tm * k * 2 + k * tn * 2 + 2 * tm * tn * 4
    return pl.pallas_call(
        _mm_res_kernel,
        out_shape=jax.ShapeDtypeStruct((m, n), F32),
        grid=(m // tm, n // tn),
        in_specs=[
            pl.BlockSpec((tm, k), lambda i, j: (i, 0)),
            pl.BlockSpec((k, tn), lambda i, j: (0, j)),
            pl.BlockSpec((tm, tn), lambda i, j: (i, j)),
        ],
        out_specs=pl.BlockSpec((tm, tn), lambda i, j: (i, j)),
        compiler_params=_params(("parallel", "parallel"), blocks, tm * tn * 4),
        name="out_proj_residual",
    )(a, b, x)


def _merge_kernel(oa_ref, ob_ref, oc_ref, w_ref, ga_ref, gb_ref, gc_ref, y_ref):
    y = None
    for n, (o_ref, g_ref) in enumerate(((oa_ref, ga_ref), (ob_ref, gb_ref), (oc_ref, gc_ref))):
        gate = _sigmoid(g_ref[...].astype(F32))
        term = gate * jnp.dot(o_ref[...], w_ref[n], preferred_element_type=F32)
        y = term if y is None else y + term
    y_ref[...] = y.astype(y_ref.dtype)


def _merge(o_a, o_b, o_c, w_branch, u, gate_off, tm, tn):
    t, bw = o_a.shape
    d = w_branch.shape[2]
    nj = d // tn
    g0 = gate_off // tn
    branch_spec = pl.BlockSpec((tm, bw), lambda i, j: (i, 0))
    gate_specs = [pl.BlockSpec((tm, tn), functools.partial(lambda i, j, n: (i, g0 + n * nj + j), n=n))
                  for n in range(N_BRANCH)]
    blocks = 3 * tm * bw * 2 + N_BRANCH * bw * tn * 2 + 3 * tm * tn * 2 + tm * tn * 2
    return pl.pallas_call(
        _merge_kernel,
        out_shape=jax.ShapeDtypeStruct((t, d), BF16),
        grid=(t // tm, nj),
        in_specs=[branch_spec, branch_spec, branch_spec,
                  pl.BlockSpec((N_BRANCH, bw, tn), lambda i, j: (0, 0, j))] + gate_specs,
        out_specs=pl.BlockSpec((tm, tn), lambda i, j: (i, j)),
        compiler_params=_params(("parallel", "parallel"), blocks, 3 * tm * tn * 4),
        name="gated_merge",
    )(o_a, o_b, o_c, w_branch, u, u, u)


def _chunk_indicator(pos0, rows, first_lane):
    chunk = (pos0 + lax.broadcasted_iota(jnp.int32, (rows, LANES), 0)) // CHUNK
    lane = lax.broadcasted_iota(jnp.int32, (rows, LANES), 1)
    return jnp.where(lane - first_lane == chunk, 1.0, 0.0)


def _chunk_mask_rows(pos0, cols, first_row):
    chunk = (pos0 + lax.broadcasted_iota(jnp.int32, (LANES, cols), 1)) // CHUNK
    row = lax.broadcasted_iota(jnp.int32, (LANES, cols), 0)
    return jnp.where((row >= first_row) & (row - first_row > chunk), MASKED, 0.0)


def _flash_loop(qt_ref, k_ref, vt_ref, s_refs, p_refs, acc_ref, l_ref, *, tk, ratio):
    n_q, _, cols = acc_ref.shape

    def scores(qi, j):
        k = k_ref[pl.ds(pl.multiple_of(j * tk, tk), tk), :]
        return jnp.dot(k, qt_ref[0, 0, qi], preferred_element_type=F32)

    def step(half, carry):
        qi, j, m, l, alpha_prev, qi_prev, j_prev = carry
        last = j == qi // ratio
        qi_next = jnp.where(last, jnp.minimum(qi + 1, n_q - 1), qi)
        j_next = jnp.where(last, 0, j + 1)
        s_refs[1 - half][...] = scores(qi_next, j_next)
        s = s_refs[half][...]
        m_new = jnp.maximum(m, jnp.max(s, axis=0, keepdims=True))
        alpha = jnp.exp2(m - m_new)
        p = jnp.exp2(s - m_new)
        p_refs[half][...] = p.astype(BF16)
        l_new = alpha * l + jnp.sum(p, axis=0, keepdims=True)
        half_cols = cols // 2
        for c in range(2):
            cs = slice(c * half_cols, (c + 1) * half_cols)
            acc = alpha_prev[:, cs] * acc_ref[qi_prev, :, cs] + jnp.dot(
                vt_ref[0, 0, j_prev], p_refs[1 - half][:, cs], preferred_element_type=F32)
            acc_ref[qi_prev, :, cs] = acc
        l_ref[qi_prev] = l
        return qi_next, j_next, jnp.where(last, MASKED, m_new), l_new, alpha, qi, j

    def steps(count, carry):
        for i in range(count):
            carry = step(i % 2, carry)
        return carry

    zero = jnp.int32(0)
    s_refs[0][...] = scores(zero, zero)
    p_refs[1][...] = jnp.zeros(p_refs[1].shape, BF16)
    row = jnp.zeros((1, cols), F32)
    acc_ref[...] = jnp.zeros(acc_ref.shape, F32)
    carry = (zero, zero, row + MASKED, row, row, zero, zero)
    n_steps = sum(qi // ratio + 1 for qi in range(n_q)) + 1
    carry = lax.fori_loop(0, n_steps // STEPS_PER_ITERATION, lambda _, c: steps(STEPS_PER_ITERATION, c), carry)
    steps(n_steps % STEPS_PER_ITERATION, carry)


def _rotate_rows(y, half, rope_t):
    cos, sin_lo, sin_hi = rope_t[:LANES], rope_t[LANES:2 * LANES], rope_t[2 * LANES:]
    up = jnp.concatenate([y[half:], y[:half]], axis=0)
    down = jnp.concatenate([y[-half:], y[:-half]], axis=0)
    return y * cos + up * sin_lo + down * sin_hi


def _silu_gate_store(o_ref, z_ref, rows, y):
    z = z_ref[rows, :].astype(F32)
    o_ref[rows, :] = (y * (z * _sigmoid(z))).astype(o_ref.dtype)


def _prep_a_kernel(cq_ref, ckv_ref, h_ref, wkr_ref, gcq_ref, gckv_ref, wuqt_ref, wk_ref, wvt_ref, gq_ref, gk_ref,
                   rope_ref, ropet_ref, qt_ref, k_ref, vt_ref, *, heads, scale, n_seq):
    rows = cq_ref.shape[0]
    pos0 = (pl.program_id(0) % n_seq) * rows
    rope = rope_ref[...]
    rope_t = ropet_ref[...]
    cos, sin_lo, sin_hi = rope[:, :LANES], rope[:, LANES:2 * LANES], rope[:, 2 * LANES:]

    def rotate(y):
        return y * cos + pltpu.roll(y, LANES - A_ROPE // 2, 1) * sin_lo + pltpu.roll(y, A_ROPE // 2, 1) * sin_hi

    cq = cq_ref[...].astype(F32)
    cq = _rms(cq, gcq_ref[...], cq.shape[-1])
    ckv = ckv_ref[...].astype(F32)
    ckv = _rms(ckv, gckv_ref[...], ckv.shape[-1])
    qat = jnp.dot(wuqt_ref[...], cq.T.astype(BF16), preferred_element_type=F32)
    vat = jnp.dot(wvt_ref[...], ckv.T.astype(BF16), preferred_element_type=F32)
    kna = jnp.dot(ckv.astype(BF16), wk_ref[...], preferred_element_type=F32)
    gq = gq_ref[...]
    gk = gk_ref[...]
    kr = jnp.dot(h_ref[...], wkr_ref[...], preferred_element_type=F32)
    k_rope = (rotate(_rms(kr, gk[:, LANES:], A_ROPE)) + _chunk_indicator(pos0, rows, A_ROPE)).astype(BF16)
    mask_rows = _chunk_mask_rows(pos0, rows, A_ROPE)
    hw = 2 * LANES
    for h in range(heads):
        xn = qat[h * hw:h * hw + LANES]
        q_nope = xn * lax.rsqrt(jnp.sum(xn * xn, axis=0, keepdims=True) * (1.0 / A_NOPE) + EPS) * gq[:LANES]
        xr = qat[h * hw + LANES:(h + 1) * hw]
        q_rope = xr * lax.rsqrt(jnp.sum(xr * xr, axis=0, keepdims=True) * (1.0 / A_ROPE) + EPS) * gq[LANES:]
        q_rope = _rotate_rows(q_rope, A_ROPE // 2, rope_t)
        qt_ref[0, h, 0, :LANES, :] = (q_nope * scale).astype(BF16)
        qt_ref[0, h, 0, LANES:, :] = (q_rope * scale + mask_rows).astype(BF16)
        k_nope = _rms(kna[:, h * LANES:(h + 1) * LANES], gk[:, :LANES], A_NOPE)
        k_ref[:, h * hw:h * hw + LANES] = k_nope.astype(BF16)
        k_ref[:, h * hw + LANES:(h + 1) * hw] = k_rope
        vt_ref[0, h, 0] = vat[h * LANES:(h + 1) * LANES].astype(BF16)


def _prep_a(u, h, w_kr, off_cq, off_ckv, g_cq, g_ckv, w_uqt, w_k, w_vt, g_q, g_k, rope, rope_t, batch, seq, heads,
            t_rows):
    t = u.shape[0]
    q_rank, kv_rank = w_uqt.shape[1], w_k.shape[0]
    hw = 2 * LANES
    n_seq = seq // t_rows
    assert seq // CHUNK <= LANES - A_ROPE
    scale = (A_NOPE + A_ROPE) ** -0.5 * LOG2E
    row = lambda i: (i, 0)
    const = lambda i: (0, 0)
    tile = lambda i: (i // n_seq, 0, i % n_seq, 0, 0)
    d = h.shape[1]
    blocks = (t_rows * (q_rank + kv_rank + d) * 2 + d * LANES * 2 + (q_rank * hw + 2 * kv_rank * LANES) * heads * 2
              + 2 * t_rows * 3 * LANES * 4 + t_rows * heads * (2 * hw + LANES) * 2)
    return pl.pallas_call(
        functools.partial(_prep_a_kernel, heads=heads, scale=scale, n_seq=n_seq),
        out_shape=(jax.ShapeDtypeStruct((batch, heads, n_seq, hw, t_rows), BF16),
                   jax.ShapeDtypeStruct((t, heads * hw), BF16),
                   jax.ShapeDtypeStruct((batch, heads, n_seq, A_V, t_rows), BF16)),
        grid=(t // t_rows,),
        in_specs=[
            pl.BlockSpec((t_rows, q_rank), lambda i: (i, off_cq // q_rank)),
            pl.BlockSpec((t_rows, kv_rank), lambda i: (i, off_ckv // kv_rank)),
            pl.BlockSpec((t_rows, d), row),
            pl.BlockSpec((d, LANES), const),
            pl.BlockSpec((1, q_rank), const),
            pl.BlockSpec((1, kv_rank), const),
            pl.BlockSpec((heads * hw, q_rank), const),
            pl.BlockSpec((kv_rank, heads * LANES), const),
            pl.BlockSpec((heads * LANES, kv_rank), const),
            pl.BlockSpec((hw, 1), const),
            pl.BlockSpec((1, hw), const),
            pl.BlockSpec((t_rows, 3 * LANES), lambda i: (i % n_seq, 0)),
            pl.BlockSpec((3 * LANES, t_rows), lambda i: (0, i % n_seq)),
        ],
        out_specs=(pl.BlockSpec((1, heads, 1, hw, t_rows), tile),
                   pl.BlockSpec((t_rows, heads * hw), row),
                   pl.BlockSpec((1, heads, 1, A_V, t_rows), tile)),
        compiler_params=_params(("parallel",), blocks, 4 * t_rows * heads * hw * 4),
        name="mla_prep",
    )(u, u, h, w_kr, g_cq, g_ckv, w_uqt, w_k, w_vt, g_q, g_k, rope, rope_t)


def _attn_a_kernel(qt_ref, k_ref, vt_ref, z_ref, o_ref, s0_ref, s1_ref, p0_ref, p1_ref, acc_ref, l_ref, *, t):
    _flash_loop(qt_ref, k_ref, vt_ref, (s0_ref, s1_ref), (p0_ref, p1_ref), acc_ref, l_ref, tk=t, ratio=1)
    for qi in range(acc_ref.shape[0]):
        _silu_gate_store(o_ref, z_ref, slice(qi * t, (qi + 1) * t), (acc_ref[qi] * (1.0 / l_ref[qi])).T)


def _attn_a(qt, k, vt, u, z_off, batch, seq, heads, t):
    n_tiles = seq // t
    dk = k.shape[1] // heads
    zb = z_off // LANES
    seq_head = lambda b, h: (b, h)
    tiles = lambda b, h: (b, h, 0, 0, 0)
    blocks = 2 * seq * dk * 2 + 3 * seq * LANES * 2
    scratch = 3 * t * t * 4 + n_tiles * A_V * t * 4
    return pl.pallas_call(
        functools.partial(_attn_a_kernel, t=t),
        out_shape=jax.ShapeDtypeStruct((batch * seq, heads * LANES), BF16),
        grid=(batch, heads),
        in_specs=[
            pl.BlockSpec((1, 1, n_tiles, dk, t), tiles),
            pl.BlockSpec((seq, dk), seq_head),
            pl.BlockSpec((1, 1, n_tiles, A_V, t), tiles),
            pl.BlockSpec((seq, LANES), lambda b, h: (b, zb + h)),
        ],
        out_specs=pl.BlockSpec((seq, LANES), seq_head),
        scratch_shapes=[pltpu.VMEM((t, t), F32), pltpu.VMEM((t, t), F32),
                        pltpu.VMEM((t, t), BF16), pltpu.VMEM((t, t), BF16),
                        pltpu.VMEM((n_tiles, A_V, t), F32), pltpu.VMEM((n_tiles, 1, t), F32)],
        compiler_params=_params(("parallel", "parallel"), blocks, scratch + 6 * t * t * 4),
        name="mla_attention",
    )(qt, k, vt, u)


def _proj_t_b_kernel(wt_ref, ht_ref, gq_ref, ropet_ref, qt_ref, vt_ref, *, heads, scale, n_seq, tq):
    cols = ht_ref.shape[1]
    pos0 = (pl.program_id(0) % n_seq) * cols
    res = jnp.dot(wt_ref[...], ht_ref[...], preferred_element_type=F32)
    rope_t = ropet_ref[...]
    gq = gq_ref[...]
    first_map = lax.broadcasted_iota(jnp.int32, (LANES, 1), 0) < B_DK
    mask_rows = _chunk_mask_rows(pos0, cols, 0).astype(BF16)
    for h in range(heads):
        x = res[h * LANES:(h + 1) * LANES]
        xsq = x * x
        ss_lo = jnp.sum(xsq[:B_DK], axis=0, keepdims=True)
        ss_hi = jnp.sum(xsq[B_DK:], axis=0, keepdims=True)
        y = x * lax.rsqrt(jnp.where(first_map, ss_lo, ss_hi) * (1.0 / B_DK) + EPS) * gq
        q = _rotate_rows(y, B_ROT // 2, rope_t) * scale
        q_lo = jnp.where(first_map, q, 0.0).astype(BF16)
        q_hi = jnp.where(first_map, 0.0, q).astype(BF16)
        for a in range(cols // tq):
            cs = slice(a * tq, (a + 1) * tq)
            qt_ref[0, h, a, :LANES, :tq] = q_lo[:, cs]
            qt_ref[0, h, a, :LANES, tq:] = q_hi[:, cs]
            qt_ref[0, h, a, LANES:, :tq] = mask_rows[:, cs]
            qt_ref[0, h, a, LANES:, tq:] = mask_rows[:, cs]
        vt_ref[0, h, 0] = res[(heads + h) * LANES:(heads + h + 1) * LANES].astype(BF16)


def _proj_t_b(w_t, h_t, g_q, rope_t, batch, seq, heads, tq, tk):
    d, t = h_t.shape
    n_seq = seq // tk
    assert seq // CHUNK <= LANES
    tile = lambda i: (i // n_seq, 0, i % n_seq, 0, 0)
    blocks = w_t.size * 2 + d * tk * 2 + 3 * LANES * tk * 4 + heads * tk * (4 * LANES + B_DV) * 2
    return pl.pallas_call(
        functools.partial(_proj_t_b_kernel, heads=heads, scale=B_DK ** -0.5 * LOG2E, n_seq=n_seq, tq=tq),
        out_shape=(jax.ShapeDtypeStruct((batch, heads, seq // tq, 2 * LANES, 2 * tq), BF16),
                   jax.ShapeDtypeStruct((batch, heads, n_seq, B_DV, tk), BF16)),
        grid=(t // tk,),
        in_specs=[
            pl.BlockSpec(w_t.shape, lambda i: (0, 0)),
            pl.BlockSpec((d, tk), lambda i: (0, i)),
            pl.BlockSpec((LANES, 1), lambda i: (0, 0)),
            pl.BlockSpec((3 * LANES, tk), lambda i: (0, i % n_seq)),
        ],
        out_specs=(pl.BlockSpec((1, heads, tk // tq, 2 * LANES, 2 * tq), tile),
                   pl.BlockSpec((1, heads, 1, B_DV, tk), tile)),
        compiler_params=_params(("parallel",), blocks, 3 * w_t.shape[0] * tk * 4),
        name="diff_proj_qv",
    )(w_t, h_t, g_q, rope_t)


def _prep_b_kernel(k_ref, gk_ref, rope_ref, ko_ref, *, heads, n_seq):
    rows = k_ref.shape[0]
    pos0 = (pl.program_id(0) % n_seq) * rows
    rope = rope_ref[...]
    cos, sin_lo, sin_hi = rope[:, :LANES], rope[:, LANES:2 * LANES], rope[:, 2 * LANES:]
    first_map = lax.broadcasted_iota(jnp.int32, (1, LANES), 1) < B_DK
    gk = gk_ref[...]
    indicator = _chunk_indicator(pos0, rows, 0).astype(BF16)
    for h in range(heads):
        x = k_ref[:, h * LANES:(h + 1) * LANES].astype(F32)
        xsq = x * x
        ss_lo = jnp.sum(jnp.where(first_map, xsq, 0.0), axis=-1, keepdims=True)
        ss_hi = jnp.sum(jnp.where(first_map, 0.0, xsq), axis=-1, keepdims=True)
        y = x * lax.rsqrt(jnp.where(first_map, ss_lo, ss_hi) * (1.0 / B_DK) + EPS) * gk
        y = y * cos + pltpu.roll(y, LANES - B_ROT // 2, 1) * sin_lo + pltpu.roll(y, B_ROT // 2, 1) * sin_hi
        ko_ref[:, 2 * h * LANES:(2 * h + 1) * LANES] = y.astype(BF16)
        ko_ref[:, (2 * h + 1) * LANES:(2 * h + 2) * LANES] = indicator


def _prep_b(u, off_k, g_k, rope, seq, heads, t_rows):
    t = u.shape[0]
    w = heads * LANES
    n_seq = seq // t_rows
    return pl.pallas_call(
        functools.partial(_prep_b_kernel, heads=heads, n_seq=n_seq),
        out_shape=jax.ShapeDtypeStruct((t, 2 * w), BF16),
        grid=(t // t_rows,),
        in_specs=[
            pl.BlockSpec((t_rows, w), lambda i: (i, off_k // w)),
            pl.BlockSpec((1, LANES), lambda i: (0, 0)),
            pl.BlockSpec((t_rows, 3 * LANES), lambda i: (i % n_seq, 0)),
        ],
        out_specs=pl.BlockSpec((t_rows, 2 * w), lambda i: (i, 0)),
        compiler_params=_params(("parallel",), 3 * t_rows * w * 2 + t_rows * 3 * LANES * 4, 8 * t_rows * LANES * 4),
        name="diff_prep_k",
    )(u, g_k, rope)


def _attn_b_kernel(qt_ref, k_ref, vt_ref, z_ref, lam_ref, gsub_ref, o_ref, s0_ref, s1_ref, p0_ref, p1_ref,
                   acc_ref, l_ref, *, tq, tk, lam_init):
    _flash_loop(qt_ref, k_ref, vt_ref, (s0_ref, s1_ref), (p0_ref, p1_ref), acc_ref, l_ref, tk=tk, ratio=tk // tq)
    lf = lam_ref[...]
    lam = (jnp.exp(jnp.sum(lf[0:1] * lf[1:2], axis=-1, keepdims=True))
           - jnp.exp(jnp.sum(lf[2:3] * lf[3:4], axis=-1, keepdims=True)) + lam_init)
    gsub = gsub_ref[...]
    for qi in range(acc_ref.shape[0]):
        o = acc_ref[qi] * (1.0 / l_ref[qi])
        a = o[:, :tq] - lam * o[:, tq:]
        ss = jnp.sum(a * a, axis=0, keepdims=True) * (1.0 / B_DV)
        y = a * lax.rsqrt(ss + EPS) * gsub * (1.0 - lam_init)
        _silu_gate_store(o_ref, z_ref, slice(qi * tq, (qi + 1) * tq), y.T)


def _attn_b(qt, k, vt, u, z_off, lam, g_sub, lam_init, batch, seq, heads, tq, tk):
    n_q, n_k = seq // tq, seq // tk
    zb = z_off // LANES
    cols = 2 * tq
    seq_head = lambda b, h: (b, h)
    tiles = lambda b, h: (b, h, 0, 0, 0)
    blocks = 2 * seq * 2 * LANES * 2 + seq * 2 * LANES * 2 + 3 * seq * LANES * 2
    scratch = 3 * tk * cols * 4 + n_q * B_DV * cols * 4
    return pl.pallas_call(
        functools.partial(_attn_b_kernel, tq=tq, tk=tk, lam_init=lam_init),
        out_shape=jax.ShapeDtypeStruct((batch * seq, heads * LANES), BF16),
        grid=(batch, heads),
        in_specs=[
            pl.BlockSpec((1, 1, n_q, 2 * LANES, cols), tiles),
            pl.BlockSpec((seq, 2 * LANES), seq_head),
            pl.BlockSpec((1, 1, n_k, B_DV, tk), tiles),
            pl.BlockSpec((seq, LANES), lambda b, h: (b, zb + h)),
            pl.BlockSpec(lam.shape, lambda b, h: (0, 0)),
            pl.BlockSpec((B_DV, 1), lambda b, h: (0, 0)),
        ],
        out_specs=pl.BlockSpec((seq, LANES), seq_head),
        scratch_shapes=[pltpu.VMEM((tk, cols), F32), pltpu.VMEM((tk, cols), F32),
                        pltpu.VMEM((tk, cols), BF16), pltpu.VMEM((tk, cols), BF16),
                        pltpu.VMEM((n_q, B_DV, cols), F32), pltpu.VMEM((n_q, 1, cols), F32)],
        compiler_params=_params(("parallel", "parallel"), blocks, scratch + 6 * tk * cols * 4),
        name="diff_attention",
    )(qt, k, vt, u, lam, g_sub)


C_PAD_SLOT = 64


def _proj_t_c_kernel(wt_ref, ht_ref, gq_ref, qt_ref, vt_ref, *, heads, scale, tq):
    cols = ht_ref.shape[1]
    r = pl.program_id(1)
    is_pad = r == 0
    pos0 = jnp.maximum(r - 1, 0) * cols
    res = jnp.dot(wt_ref[...], ht_ref[...], preferred_element_type=F32)
    gq = gq_ref[...]
    q_chunk = (pos0 + lax.broadcasted_iota(jnp.int32, (LANES, cols), 1)) // CHUNK
    slot = lax.broadcasted_iota(jnp.int32, (LANES, cols), 0)
    out_of_band = ((slot < C_PAD_SLOT) & ((slot > q_chunk) | (slot < q_chunk - C_LEFT_CHUNKS))) | (slot == C_PAD_SLOT)
    mask_rows = jnp.where(out_of_band, MASKED, 0.0).astype(BF16)
    for h in range(heads):
        x = res[h * LANES:(h + 1) * LANES]
        q = x * lax.rsqrt(jnp.sum(x * x, axis=0, keepdims=True) * (1.0 / C_DH) + EPS) * gq * scale
        q = q.astype(BF16)
        v = res[(heads + h) * LANES:(heads + h + 1) * LANES].astype(BF16)
        v = jnp.where(is_pad, jnp.zeros_like(v), v)
        for a in range(cols // tq):
            cs = slice(a * tq, (a + 1) * tq)
            qt_ref[0, h, a, :LANES, :] = q[:, cs]
            qt_ref[0, h, a, LANES:, :] = mask_rows[:, cs]
            vt_ref[0, h, a] = v[:, cs]


def _proj_t_c(w_t, h_t, g_q, batch, seq, heads, pad, tq):
    d, t = h_t.shape
    n_seq = seq // pad
    per = pad // tq
    assert seq // CHUNK <= C_PAD_SLOT
    blocks = w_t.size * 2 + d * pad * 2 + heads * pad * 3 * LANES * 2
    return pl.pallas_call(
        functools.partial(_proj_t_c_kernel, heads=heads, scale=C_DH ** -0.5 * LOG2E, tq=tq),
        out_shape=(jax.ShapeDtypeStruct((batch, heads, seq // tq, 2 * LANES, tq), BF16),
                   jax.ShapeDtypeStruct((batch, heads, (seq + pad) // tq, C_DH, tq), BF16)),
        grid=(batch, n_seq + 1),
        in_specs=[
            pl.BlockSpec(w_t.shape, lambda b, r: (0, 0)),
            pl.BlockSpec((d, pad), lambda b, r: (0, b * n_seq + jnp.maximum(r - 1, 0))),
            pl.BlockSpec((LANES, 1), lambda b, r: (0, 0)),
        ],
        out_specs=(pl.BlockSpec((1, heads, per, 2 * LANES, tq), lambda b, r: (b, 0, jnp.maximum(r - 1, 0), 0, 0)),
                   pl.BlockSpec((1, heads, per, C_DH, tq), lambda b, r: (b, 0, r, 0, 0))),
        compiler_params=_params(("parallel", "arbitrary"), blocks, 3 * w_t.shape[0] * pad * 4),
        name="band_proj_qv",
    )(w_t, h_t, g_q)


def _prep_c_kernel(k_ref, gk_ref, ko_ref, *, heads):
    rows = k_ref.shape[0]
    r = pl.program_id(1)
    is_pad = r == 0
    pos0 = jnp.maximum(r - 1, 0) * rows
    gk = gk_ref[...]
    lane = lax.broadcasted_iota(jnp.int32, (rows, LANES), 1)
    indicator = jnp.where(is_pad, jnp.where(lane == C_PAD_SLOT, 1.0, 0.0), _chunk_indicator(pos0, rows, 0))
    indicator = indicator.astype(BF16)
    for h in range(heads):
        kn = _rms(k_ref[:, h * LANES:(h + 1) * LANES].astype(F32), gk, C_DH).astype(BF16)
        ko_ref[:, 2 * h * LANES:(2 * h + 1) * LANES] = jnp.where(is_pad, jnp.zeros_like(kn), kn)
        ko_ref[:, (2 * h + 1) * LANES:(2 * h + 2) * LANES] = indicator


def _prep_c(u, off_k, g_k, batch, seq, heads, pad):
    w = heads * LANES
    n_seq = seq // pad
    return pl.pallas_call(
        functools.partial(_prep_c_kernel, heads=heads),
        out_shape=jax.ShapeDtypeStruct((batch * (seq + pad), 2 * w), BF16),
        grid=(batch, n_seq + 1),
        in_specs=[
            pl.BlockSpec((pad, w), lambda b, r: (b * n_seq + jnp.maximum(r - 1, 0), off_k // w)),
            pl.BlockSpec((1, LANES), lambda b, r: (0, 0)),
        ],
        out_specs=pl.BlockSpec((pad, 2 * w), lambda b, r: (b * (n_seq + 1) + r, 0)),
        compiler_params=_params(("parallel", "parallel"), 3 * pad * w * 2, 4 * pad * LANES * 4),
        name="band_prep_k",
    )(u, g_k)


def _attn_c_kernel(qt_ref, k_ref, vt_ref, z_ref, rel_ref, o_ref, s0_ref, s1_ref, p0_ref, p1_ref, bias_ref, *, tq, tw):
    n_q = qt_ref.shape[2]
    s_refs, p_refs = (s0_ref, s1_ref), (p0_ref, p1_ref)
    width = rel_ref.shape[-1]
    toeplitz = pltpu.roll(jnp.broadcast_to(rel_ref[0], (tq, width)), 0, 1, stride=1, stride_axis=0)
    bias_ref[...] = (toeplitz[:, :tw] * LOG2E).T

    def scores(i):
        return jnp.dot(k_ref[i * tq:i * tq + tw, :], qt_ref[0, 0, i], preferred_element_type=F32) + bias_ref[...]

    s_refs[0][...] = scores(0)
    for i in range(n_q + 1):
        half = i % 2
        if i + 1 < n_q:
            s_refs[1 - half][...] = scores(i + 1)
        if i < n_q:
            s = s_refs[half][...]
            p = jnp.exp2(s - jnp.max(s, axis=0, keepdims=True))
            p_refs[half][...] = p.astype(BF16)
            l = jnp.sum(p, axis=0, keepdims=True)
        if i > 0:
            acc = None
            for a in range(tw // tq):
                part = jnp.dot(vt_ref[0, 0, i - 1 + a], p_refs[1 - half][a * tq:(a + 1) * tq, :],
                               preferred_element_type=F32)
                acc = part if acc is None else acc + part
            _silu_gate_store(o_ref, z_ref, slice((i - 1) * tq, i * tq), (acc * (1.0 / l_prev)).T)
        l_prev = l


def _attn_c(qt, k, vt, u, z_off, rel_rows, batch, seq, heads, tq, pad):
    n_q = seq // tq
    tw = tq + pad
    zb = z_off // LANES
    tiles = lambda b, h: (b, h, 0, 0, 0)
    blocks = seq * 2 * LANES * 2 + (seq + pad) * 3 * LANES * 2 + 2 * seq * LANES * 2
    return pl.pallas_call(
        functools.partial(_attn_c_kernel, tq=tq, tw=tw),
        out_shape=jax.ShapeDtypeStruct((batch * seq, heads * LANES), BF16),
        grid=(batch, heads),
        in_specs=[
            pl.BlockSpec((1, 1, n_q, 2 * LANES, tq), tiles),
            pl.BlockSpec((seq + pad, 2 * LANES), lambda b, h: (b, h)),
            pl.BlockSpec((1, 1, (seq + pad) // tq, C_DH, tq), tiles),
            pl.BlockSpec((seq, LANES), lambda b, h: (b, zb + h)),
            pl.BlockSpec((1, 1, rel_rows.shape[-1]), lambda b, h: (h, 0, 0)),
        ],
        out_specs=pl.BlockSpec((seq, LANES), lambda b, h: (b, h)),
        scratch_shapes=[pltpu.VMEM((tw, tq), F32), pltpu.VMEM((tw, tq), F32),
                        pltpu.VMEM((tw, tq), BF16), pltpu.VMEM((tw, tq), BF16), pltpu.VMEM((tw, tq), F32)],
        compiler_params=_params(("parallel", "parallel"), blocks, 12 * tw * tq * 4),
        name="band_attention",
    )(qt, k, vt, u, rel_rows)


def _rope_table(seq, dim, theta, group):
    half = dim // 2
    inv = 1.0 / (jnp.float32(theta) ** (jnp.arange(0, dim, 2, dtype=F32) / dim))
    ang = jnp.arange(seq, dtype=F32)[:, None] * inv[None, :]
    cos, sin = jnp.cos(ang), jnp.sin(ang)
    lane = jnp.arange(LANES) % group
    idx = lane % half
    in_lo = (lane < half)[None, :]
    in_hi = ((lane >= half) & (lane < dim))[None, :]
    c = jnp.where(in_lo | in_hi, cos[:, idx], 1.0 if group < LANES else 0.0)
    s_lo = jnp.where(in_lo, -sin[:, idx], 0.0)
    s_hi = jnp.where(in_hi, sin[:, idx], 0.0)
    return jnp.concatenate([c, s_lo, s_hi], axis=1)


def _band_rel_rows(rel_bias, tq, pad):
    width = pl.next_power_of_2(2 * tq + pad)
    e = jnp.arange(width)
    e = jnp.where(e < tq + pad, e, e - width)
    rel = jnp.clip(pad - e, -(CHUNK - 1), C_REL_MAX) + (CHUNK - 1)
    return rel_bias.astype(F32)[:, None, rel]


def _row(v):
    return v.astype(F32).reshape(1, -1)


def _pad_lanes(v, width):
    return jnp.pad(v, ((0, 0),) * (v.ndim - 1) + ((0, width - v.shape[-1]),))


def _layer(x2, layer_idx, batch, seq, rope_a, rope_b, g_pre, w_in, a_g_cq, a_g_ckv, a_w_uq, a_w_ukv,
           a_g_q, a_g_k, b_g_q, b_g_k, b_lam, b_g_sub, c_g_q, c_g_k, c_rel_bias, w_branch, w_out):
    d = x2.shape[1]
    bw = w_branch.shape[1]
    q_rank, kv_rank = a_w_uq.shape[0], a_w_ukv.shape[0]
    a_heads, b_heads, c_heads = bw // A_V, bw // B_DV, bw // C_DH

    s_kr = q_rank + kv_rank
    s_az = s_kr + A_ROPE
    piece = lambda n: w_in[:, s_az + n * bw:s_az + (n + 1) * bw]
    tn_main = 2048
    n_used = 5 * bw + N_BRANCH * d + s_kr
    n_main = -(-n_used // tn_main) * tn_main
    w_main = jnp.concatenate([piece(0), piece(2), piece(4), piece(6), w_in[:, s_az + 8 * bw:], w_in[:, :s_kr],
                              jnp.zeros((d, n_main - n_used), w_in.dtype)], axis=1).astype(BF16)
    o_az, o_bk, o_bz, o_ck, o_cz, o_g = (n * bw for n in range(6))
    o_acq = o_g + N_BRANCH * d
    o_ackv = o_acq + q_rank
    w_kr = _pad_lanes(w_in[:, s_kr:s_az], LANES).astype(BF16)
    w_t_b = jnp.concatenate([piece(1), piece(3)], axis=1).T.astype(BF16)
    w_t_c = jnp.concatenate([piece(5), piece(7)], axis=1).T.astype(BF16)

    h, h_t = _rmsnorm(x2, _row(g_pre))
    u = _matmul(h, w_main, BF16, 1024, tn_main, "in_proj")

    hw = 2 * LANES
    t_a = min(512, seq)
    w_uq = _pad_lanes(a_w_uq.reshape(q_rank, a_heads, A_NOPE + A_ROPE), hw).reshape(q_rank, a_heads * hw)
    w_ukv = a_w_ukv.reshape(kv_rank, a_heads, A_NOPE + A_V)
    w_k = w_ukv[:, :, :A_NOPE].reshape(kv_rank, a_heads * A_NOPE).astype(BF16)
    w_vt = w_ukv[:, :, A_NOPE:].reshape(kv_rank, a_heads * A_V).T.astype(BF16)
    g_q = jnp.concatenate([_row(a_g_q[:A_NOPE]), _pad_lanes(_row(a_g_q[A_NOPE:]), LANES)], axis=1)
    g_k = jnp.concatenate([_row(a_g_k[:A_NOPE]), _pad_lanes(_row(a_g_k[A_NOPE:]), LANES)], axis=1)
    qa, ka, vta = _prep_a(u, h, w_kr, o_acq, o_ackv, _row(a_g_cq), _row(a_g_ckv), w_uq.T.astype(BF16), w_k, w_vt,
                          g_q.T, g_k, rope_a, rope_a.T, batch, seq, a_heads, t_a)
    o_a = _attn_a(qa, ka, vta, u, o_az, batch, seq, a_heads, t_a)

    tq_b, tk_b = min(256, seq), min(512, seq)
    qb, vtb = _proj_t_b(w_t_b, h_t, jnp.tile(_row(b_g_q), (1, 2)).T, rope_b.T, batch, seq, b_heads, tq_b, tk_b)
    kb = _prep_b(u, o_bk, jnp.tile(_row(b_g_k), (1, 2)), rope_b, seq, b_heads, tk_b)
    lam_init = 0.8 - 0.6 * math.exp(-0.3 * layer_idx)
    o_b = _attn_b(qb, kb, vtb, u, o_bz, b_lam.astype(F32), b_g_sub.astype(F32).reshape(B_DV, 1), lam_init,
                  batch, seq, b_heads, tq_b, tk_b)

    pad = C_LEFT_CHUNKS * CHUNK
    tq_c = min(256, seq)
    qc, vtc = _proj_t_c(w_t_c, h_t, _row(c_g_q).T, batch, seq, c_heads, pad, tq_c)
    kc = _prep_c(u, o_ck, _row(c_g_k), batch, seq, c_heads, pad)
    o_c = _attn_c(qc, kc, vtc, u, o_cz, _band_rel_rows(c_rel_bias, tq_c, pad), batch, seq, c_heads, tq_c, pad)

    y = _merge(o_a, o_b, o_c, w_branch.astype(BF16), u, o_g, 1024, 1024)
    return _matmul_residual(y, w_out.astype(BF16), x2, 1024, 1024)


def kernel(x, g_pre, w_in, a_g_cq, a_g_ckv, a_w_uq, a_w_ukv, a_g_q, a_g_k, b_g_q, b_g_k, b_lam, b_g_sub,
           c_g_q, c_g_k, c_rel_bias, w_branch, w_out):
    batch, seq, d = x.shape
    rope_a = _rope_table(seq, A_ROPE, A_ROPE_THETA, LANES)
    rope_b = _rope_table(seq, B_ROT, B_ROPE_THETA, B_DK)
    x2 = x.reshape(batch * seq, d)
    for l in range(g_pre.shape[0]):
        x2 = _layer(x2, l, batch, seq, rope_a, rope_b, g_pre[l], w_in[l], a_g_cq[l], a_g_ckv[l], a_w_uq[l],
                    a_w_ukv[l], a_g_q[l], a_g_k[l], b_g_q[l], b_g_k[l], b_lam[l], b_g_sub[l], c_g_q[l],
                    c_g_k[l], c_rel_bias[l], w_branch[l], w_out[l])
    return x2.reshape(batch, seq, d)
```

```python
import functools
import math

import jax
import jax.numpy as jnp
from jax import lax
from jax.experimental import pallas as pl
from jax.experimental.pallas import tpu as pltpu

F32 = jnp.float32
BF16 = jnp.bfloat16

EPS = 1e-6
CHUNK = 64
MASKED = -1e30
LOG2E = math.log2(math.e)

A_NOPE, A_ROPE, A_V = 128, 64, 128
A_ROPE_THETA = 10000.0
B_DK, B_DV = 64, 128
B_ROT = B_DK // 4
B_ROPE_THETA = 500000.0
C_DH = 128
C_LEFT_CHUNKS = 8
C_REL_MAX = 128
N_BRANCH = 3

N_BUF = 3
MAIN_COLUMN_TILE = 2048
STEPS_PER_ITERATION = 12
LANES = 128
V7X_VMEM_BUDGET = 56 * 2**20


def _params(semantics, block_bytes, temp_bytes=0):
    need = 2 * block_bytes + temp_bytes + (4 << 20)
    return pltpu.CompilerParams(
        dimension_semantics=semantics,
        vmem_limit_bytes=int(min(max(need, 16 << 20), V7X_VMEM_BUDGET)),
    )


def _sigmoid(z):
    return 1.0 / (1.0 + jnp.exp(-z))


def _rms(x, g, n):
    ss = jnp.sum(x * x, axis=-1, keepdims=True) * (1.0 / n)
    return x * lax.rsqrt(ss + EPS) * g


def _rmsnorm_kernel(x_ref, g_ref, o_ref, ot_ref):
    x = x_ref[...]
    y = _rms(x, g_ref[...], x.shape[-1])
    o_ref[...] = y.astype(o_ref.dtype)
    ot_ref[...] = y.T.astype(ot_ref.dtype)


def _rmsnorm(x, g, tm=512):
    t, d = x.shape
    return pl.pallas_call(
        _rmsnorm_kernel,
        out_shape=(jax.ShapeDtypeStruct((t, d), BF16), jax.ShapeDtypeStruct((d, t), BF16)),
        grid=(t // tm,),
        in_specs=[pl.BlockSpec((tm, d), lambda i: (i, 0)), pl.BlockSpec((1, d), lambda i: (0, 0))],
        out_specs=(pl.BlockSpec((tm, d), lambda i: (i, 0)), pl.BlockSpec((d, tm), lambda i: (0, i))),
        compiler_params=_params(("parallel",), tm * d * 8, tm * d * 12),
        name="pre_rmsnorm",
    )(x, g)


def _mm_kernel(a_ref, b_ref, o_ref):
    o_ref[...] = jnp.dot(a_ref[...], b_ref[...], preferred_element_type=F32).astype(o_ref.dtype)


def _matmul(a, b, out_dtype, tm, tn, name):
    m, k = a.shape
    n = b.shape[1]
    blocks = tm * k * 2 + k * tn * 2 + tm * tn * jnp.dtype(out_dtype).itemsize
    return pl.pallas_call(
        _mm_kernel,
        out_shape=jax.ShapeDtypeStruct((m, n), out_dtype),
        grid=(m // tm, n // tn),
        in_specs=[pl.BlockSpec((tm, k), lambda i, j: (i, 0)), pl.BlockSpec((k, tn), lambda i, j: (0, j))],
        out_specs=pl.BlockSpec((tm, tn), lambda i, j: (i, j)),
        compiler_params=_params(("parallel", "parallel"), blocks, tm * tn * 4),
        name=name,
    )(a, b)


def _mm_res_kernel(a_ref, b_ref, x_ref, o_ref):
    o_ref[...] = x_ref[...] + jnp.dot(a_ref[...], b_ref[...], preferred_element_type=F32)


def _matmul_residual(a, b, x, tm, tn):
    m, k = a.shape
    n = b.shape[1]
    blocks = tm * k * 2 + k * tn * 2 + 2 * tm * tn * 4
    return pl.pallas_call(
        _mm_res_kernel,
        out_shape=jax.ShapeDtypeStruct((m, n), F32),
        grid=(m // tm, n // tn),
        in_specs=[
            pl.BlockSpec((tm, k), lambda i, j: (i, 0)),
            pl.BlockSpec((k, tn), lambda i, j: (0, j)),
            pl.BlockSpec((tm, tn), lambda i, j: (i, j)),
        ],
        out_specs=pl.BlockSpec((tm, tn), lambda i, j: (i, j)),
        compiler_params=_params(("parallel", "parallel"), blocks, tm * tn * 4),
        name="out_proj_residual",
    )(a, b, x)


def _weights_kernel(w_ref, main_ref, tb_ref, tc_ref, kr_ref, *, q_rank, kv_rank, bw, n_gate):
    s_kr = q_rank + kv_rank
    s_az = s_kr + A_ROPE

    def piece(n):
        return w_ref[0, :, s_az + n * bw:s_az + (n + 1) * bw]

    for slot, n in enumerate((0, 2, 4, 6, 8)):
        main_ref[0, :, slot * bw:(slot + 1) * bw] = piece(n).astype(BF16)
    o_g = 5 * bw
    main_ref[0, :, o_g:o_g + n_gate] = w_ref[0, :, s_az + 9 * bw:s_az + 9 * bw + n_gate].astype(BF16)
    main_ref[0, :, o_g + n_gate:o_g + n_gate + s_kr] = w_ref[0, :, :s_kr].astype(BF16)
    n_pad = main_ref.shape[2] - (o_g + n_gate + s_kr)
    main_ref[0, :, o_g + n_gate + s_kr:] = jnp.zeros((main_ref.shape[1], n_pad), BF16)
    kr_ref[0, :, :A_ROPE] = w_ref[0, :, s_kr:s_az].astype(BF16)
    kr_ref[0, :, A_ROPE:] = jnp.zeros((kr_ref.shape[1], LANES - A_ROPE), BF16)
    tb_ref[0, :bw, :] = piece(1).T.astype(BF16)
    tb_ref[0, bw:, :] = piece(3).T.astype(BF16)
    tc_ref[0, :bw, :] = piece(5).T.astype(BF16)
    tc_ref[0, bw:, :] = piece(7).T.astype(BF16)


def _weights(w_in, q_rank, kv_rank, bw, n_gate, n_main, rows=128):
    n_layers, d, d_in = w_in.shape
    blocks = rows * d_in * 4 + rows * (n_main + LANES + 4 * bw) * 2
    return pl.pallas_call(
        functools.partial(_weights_kernel, q_rank=q_rank, kv_rank=kv_rank, bw=bw, n_gate=n_gate),
        out_shape=(jax.ShapeDtypeStruct((n_layers, d, n_main), BF16),
                   jax.ShapeDtypeStruct((n_layers, 2 * bw, d), BF16),
                   jax.ShapeDtypeStruct((n_layers, 2 * bw, d), BF16),
                   jax.ShapeDtypeStruct((n_layers, d, LANES), BF16)),
        grid=(n_layers, d // rows),
        in_specs=[pl.BlockSpec((1, rows, d_in), lambda l, r: (l, r, 0))],
        out_specs=(pl.BlockSpec((1, rows, n_main), lambda l, r: (l, r, 0)),
                   pl.BlockSpec((1, 2 * bw, rows), lambda l, r: (l, 0, r)),
                   pl.BlockSpec((1, 2 * bw, rows), lambda l, r: (l, 0, r)),
                   pl.BlockSpec((1, rows, LANES), lambda l, r: (l, r, 0))),
        compiler_params=_params(("parallel", "parallel"), blocks, 4 * rows * bw * 4),
        name="weights_prep",
    )(w_in)


def _merge_kernel(oa_ref, ob_ref, oc_ref, w_ref, ga_ref, gb_ref, gc_ref, y_ref):
    y = None
    for n, (o_ref, g_ref) in enumerate(((oa_ref, ga_ref), (ob_ref, gb_ref), (oc_ref, gc_ref))):
        gate = _sigmoid(g_ref[...].astype(F32))
        term = gate * jnp.dot(o_ref[...], w_ref[n], preferred_element_type=F32)
        y = term if y is None else y + term
    y_ref[...] = y.astype(y_ref.dtype)


def _merge(o_a, o_b, o_c, w_branch, u, gate_off, tm, tn):
    t, bw = o_a.shape
    d = w_branch.shape[2]
    nj = d // tn
    g0 = gate_off // tn
    branch_spec = pl.BlockSpec((tm, bw), lambda i, j: (i, 0))
    gate_specs = [pl.BlockSpec((tm, tn), functools.partial(lambda i, j, n: (i, g0 + n * nj + j), n=n))
                  for n in range(N_BRANCH)]
    blocks = 3 * tm * bw * 2 + N_BRANCH * bw * tn * 2 + 3 * tm * tn * 2 + tm * tn * 2
    return pl.pallas_call(
        _merge_kernel,
        out_shape=jax.ShapeDtypeStruct((t, d), BF16),
        grid=(t // tm, nj),
        in_specs=[branch_spec, branch_spec, branch_spec,
                  pl.BlockSpec((N_BRANCH, bw, tn), lambda i, j: (0, 0, j))] + gate_specs,
        out_specs=pl.BlockSpec((tm, tn), lambda i, j: (i, j)),
        compiler_params=_params(("parallel", "parallel"), blocks, 3 * tm * tn * 4),
        name="gated_merge",
    )(o_a, o_b, o_c, w_branch, u, u, u)


def _chunk_indicator(pos0, rows, first_lane):
    chunk = (pos0 + lax.broadcasted_iota(jnp.int32, (rows, LANES), 0)) // CHUNK
    lane = lax.broadcasted_iota(jnp.int32, (rows, LANES), 1)
    return jnp.where(lane - first_lane == chunk, 1.0, 0.0)


def _chunk_mask_rows(pos0, cols, first_row):
    chunk = (pos0 + lax.broadcasted_iota(jnp.int32, (LANES, cols), 1)) // CHUNK
    row = lax.broadcasted_iota(jnp.int32, (LANES, cols), 0)
    return jnp.where((row >= first_row) & (row - first_row > chunk), MASKED, 0.0)


def _flash_loop(qt_ref, k_ref, vt_ref, s_refs, p_refs, acc_ref, l_ref, *, tk, ratio):
    n_q, _, cols = acc_ref.shape

    def scores(qi, j):
        k = k_ref[pl.ds(pl.multiple_of(j * tk, tk), tk), :]
        return jnp.dot(k, qt_ref[0, 0, qi], preferred_element_type=F32)

    def step(cur, carry):
        nxt, prv = (cur + 1) % N_BUF, (cur - 1) % N_BUF
        qi, j, m, l, alpha_prev, qi_prev, j_prev = carry
        last = j == qi // ratio
        qi_next = jnp.where(last, jnp.minimum(qi + 1, n_q - 1), qi)
        j_next = jnp.where(last, 0, j + 1)
        s_refs[nxt][...] = scores(qi_next, j_next)
        s = s_refs[cur][...]
        m_new = jnp.maximum(m, jnp.max(s, axis=0, keepdims=True))
        alpha = jnp.exp2(m - m_new)
        p = jnp.exp2(s - m_new)
        p_refs[cur][...] = p.astype(BF16)
        l_new = alpha * l + jnp.sum(p, axis=0, keepdims=True)
        half_cols = cols // 2
        for c in range(2):
            cs = slice(c * half_cols, (c + 1) * half_cols)
            acc = alpha_prev[:, cs] * acc_ref[qi_prev, :, cs] + jnp.dot(
                vt_ref[0, 0, j_prev], p_refs[prv][:, cs], preferred_element_type=F32)
            acc_ref[qi_prev, :, cs] = acc
        l_ref[qi_prev] = l
        return qi_next, j_next, jnp.where(last, MASKED, m_new), l_new, alpha, qi, j

    def steps(count, carry):
        for i in range(count):
            carry = step(i % N_BUF, carry)
        return carry

    zero = jnp.int32(0)
    s_refs[0][...] = scores(zero, zero)
    p_refs[N_BUF - 1][...] = jnp.zeros(p_refs[0].shape, BF16)
    row = jnp.zeros((1, cols), F32)
    acc_ref[...] = jnp.zeros(acc_ref.shape, F32)
    carry = (zero, zero, row + MASKED, row, row, zero, zero)
    n_steps = sum(qi // ratio + 1 for qi in range(n_q)) + 1
    carry = lax.fori_loop(0, n_steps // STEPS_PER_ITERATION, lambda _, c: steps(STEPS_PER_ITERATION, c), carry)
    steps(n_steps % STEPS_PER_ITERATION, carry)


def _rotate_rows(y, half, rope_t):
    cos, sin_lo, sin_hi = rope_t[:LANES], rope_t[LANES:2 * LANES], rope_t[2 * LANES:]
    up = jnp.concatenate([y[half:], y[:half]], axis=0)
    down = jnp.concatenate([y[-half:], y[:-half]], axis=0)
    return y * cos + up * sin_lo + down * sin_hi


def _silu_gate_store(o_ref, z_ref, rows, y):
    z = z_ref[rows, :].astype(F32)
    o_ref[rows, :] = (y * (z * _sigmoid(z))).astype(o_ref.dtype)


def _prep_a_kernel(cq_ref, ckv_ref, h_ref, wkr_ref, gcq_ref, gckv_ref, wuqt_ref, wk_ref, wvt_ref, gq_ref, gk_ref,
                   rope_ref, ropet_ref, qt_ref, k_ref, vt_ref, *, heads, scale, n_seq):
    rows = cq_ref.shape[0]
    pos0 = (pl.program_id(0) % n_seq) * rows
    rope = rope_ref[...]
    rope_t = ropet_ref[...]
    cos, sin_lo, sin_hi = rope[:, :LANES], rope[:, LANES:2 * LANES], rope[:, 2 * LANES:]

    def rotate(y):
        return y * cos + pltpu.roll(y, LANES - A_ROPE // 2, 1) * sin_lo + pltpu.roll(y, A_ROPE // 2, 1) * sin_hi

    cq = cq_ref[...].astype(F32)
    cq = _rms(cq, gcq_ref[...], cq.shape[-1])
    ckv = ckv_ref[...].astype(F32)
    ckv = _rms(ckv, gckv_ref[...], ckv.shape[-1])
    qat = jnp.dot(wuqt_ref[...], cq.T.astype(BF16), preferred_element_type=F32)
    vat = jnp.dot(wvt_ref[...], ckv.T.astype(BF16), preferred_element_type=F32)
    kna = jnp.dot(ckv.astype(BF16), wk_ref[...], preferred_element_type=F32)
    gq = gq_ref[...]
    gk = gk_ref[...]
    kr = jnp.dot(h_ref[...], wkr_ref[...], preferred_element_type=F32)
    k_rope = (rotate(_rms(kr, gk[:, LANES:], A_ROPE)) + _chunk_indicator(pos0, rows, A_ROPE)).astype(BF16)
    mask_rows = _chunk_mask_rows(pos0, rows, A_ROPE)
    hw = 2 * LANES
    for h in range(heads):
        xn = qat[h * hw:h * hw + LANES]
        q_nope = xn * lax.rsqrt(jnp.sum(xn * xn, axis=0, keepdims=True) * (1.0 / A_NOPE) + EPS) * gq[:LANES]
        xr = qat[h * hw + LANES:(h + 1) * hw]
        q_rope = xr * lax.rsqrt(jnp.sum(xr * xr, axis=0, keepdims=True) * (1.0 / A_ROPE) + EPS) * gq[LANES:]
        q_rope = _rotate_rows(q_rope, A_ROPE // 2, rope_t)
        qt_ref[0, h, 0, :LANES, :] = (q_nope * scale).astype(BF16)
        qt_ref[0, h, 0, LANES:, :] = (q_rope * scale + mask_rows).astype(BF16)
        k_nope = _rms(kna[:, h * LANES:(h + 1) * LANES], gk[:, :LANES], A_NOPE)
        k_ref[:, h * hw:h * hw + LANES] = k_nope.astype(BF16)
        k_ref[:, h * hw + LANES:(h + 1) * hw] = k_rope
        vt_ref[0, h, 0] = vat[h * LANES:(h + 1) * LANES].astype(BF16)


def _prep_a(u, h, w_kr, off_cq, off_ckv, g_cq, g_ckv, w_uqt, w_k, w_vt, g_q, g_k, rope, rope_t, batch, seq, heads,
            t_rows):
    t = u.shape[0]
    q_rank, kv_rank = w_uqt.shape[1], w_k.shape[0]
    hw = 2 * LANES
    n_seq = seq // t_rows
    assert seq // CHUNK <= LANES - A_ROPE
    scale = (A_NOPE + A_ROPE) ** -0.5 * LOG2E
    row = lambda i: (i, 0)
    const = lambda i: (0, 0)
    tile = lambda i: (i // n_seq, 0, i % n_seq, 0, 0)
    d = h.shape[1]
    blocks = (t_rows * (q_rank + kv_rank + d) * 2 + d * LANES * 2 + (q_rank * hw + 2 * kv_rank * LANES) * heads * 2
              + 2 * t_rows * 3 * LANES * 4 + t_rows * heads * (2 * hw + LANES) * 2)
    return pl.pallas_call(
        functools.partial(_prep_a_kernel, heads=heads, scale=scale, n_seq=n_seq),
        out_shape=(jax.ShapeDtypeStruct((batch, heads, n_seq, hw, t_rows), BF16),
                   jax.ShapeDtypeStruct((t, heads * hw), BF16),
                   jax.ShapeDtypeStruct((batch, heads, n_seq, A_V, t_rows), BF16)),
        grid=(t // t_rows,),
        in_specs=[
            pl.BlockSpec((t_rows, q_rank), lambda i: (i, off_cq // q_rank)),
            pl.BlockSpec((t_rows, kv_rank), lambda i: (i, off_ckv // kv_rank)),
            pl.BlockSpec((t_rows, d), row),
            pl.BlockSpec((d, LANES), const),
            pl.BlockSpec((1, q_rank), const),
            pl.BlockSpec((1, kv_rank), const),
            pl.BlockSpec((heads * hw, q_rank), const),
            pl.BlockSpec((kv_rank, heads * LANES), const),
            pl.BlockSpec((heads * LANES, kv_rank), const),
            pl.BlockSpec((hw, 1), const),
            pl.BlockSpec((1, hw), const),
            pl.BlockSpec((t_rows, 3 * LANES), lambda i: (i % n_seq, 0)),
            pl.BlockSpec((3 * LANES, t_rows), lambda i: (0, i % n_seq)),
        ],
        out_specs=(pl.BlockSpec((1, heads, 1, hw, t_rows), tile),
                   pl.BlockSpec((t_rows, heads * hw), row),
                   pl.BlockSpec((1, heads, 1, A_V, t_rows), tile)),
        compiler_params=_params(("parallel",), blocks, 4 * t_rows * heads * hw * 4),
        name="mla_prep",
    )(u, u, h, w_kr, g_cq, g_ckv, w_uqt, w_k, w_vt, g_q, g_k, rope, rope_t)


def _attn_a_kernel(qt_ref, k_ref, vt_ref, z_ref, o_ref, *scratch, t):
    s_refs, p_refs, (acc_ref, l_ref) = scratch[:N_BUF], scratch[N_BUF:2 * N_BUF], scratch[2 * N_BUF:]
    _flash_loop(qt_ref, k_ref, vt_ref, s_refs, p_refs, acc_ref, l_ref, tk=t, ratio=1)
    for qi in range(acc_ref.shape[0]):
        _silu_gate_store(o_ref, z_ref, slice(qi * t, (qi + 1) * t), (acc_ref[qi] * (1.0 / l_ref[qi])).T)


def _attn_a(qt, k, vt, u, z_off, batch, seq, heads, t):
    n_tiles = seq // t
    dk = k.shape[1] // heads
    zb = z_off // LANES
    seq_head = lambda b, h: (b, h)
    tiles = lambda b, h: (b, h, 0, 0, 0)
    blocks = 2 * seq * dk * 2 + 3 * seq * LANES * 2
    scratch = N_BUF * t * t * 6 + n_tiles * A_V * t * 4
    return pl.pallas_call(
        functools.partial(_attn_a_kernel, t=t),
        out_shape=jax.ShapeDtypeStruct((batch * seq, heads * LANES), BF16),
        grid=(batch, heads),
        in_specs=[
            pl.BlockSpec((1, 1, n_tiles, dk, t), tiles),
            pl.BlockSpec((seq, dk), seq_head),
            pl.BlockSpec((1, 1, n_tiles, A_V, t), tiles),
            pl.BlockSpec((seq, LANES), lambda b, h: (b, zb + h)),
        ],
        out_specs=pl.BlockSpec((seq, LANES), seq_head),
        scratch_shapes=([pltpu.VMEM((t, t), F32)] * N_BUF + [pltpu.VMEM((t, t), BF16)] * N_BUF
                        + [pltpu.VMEM((n_tiles, A_V, t), F32), pltpu.VMEM((n_tiles, 1, t), F32)]),
        compiler_params=_params(("parallel", "parallel"), blocks, scratch + 6 * t * t * 4),
        name="mla_attention",
    )(qt, k, vt, u)


def _proj_t_b_kernel(wt_ref, ht_ref, gq_ref, ropet_ref, qt_ref, vt_ref, *, heads, scale, n_seq, tq):
    cols = ht_ref.shape[1]
    pos0 = (pl.program_id(0) % n_seq) * cols
    res = jnp.dot(wt_ref[...], ht_ref[...], preferred_element_type=F32)
    rope_t = ropet_ref[...]
    gq = gq_ref[...]
    first_map = lax.broadcasted_iota(jnp.int32, (LANES, 1), 0) < B_DK
    mask_rows = _chunk_mask_rows(pos0, cols, 0).astype(BF16)
    for h in range(heads):
        x = res[h * LANES:(h + 1) * LANES]
        xsq = x * x
        ss_lo = jnp.sum(xsq[:B_DK], axis=0, keepdims=True)
        ss_hi = jnp.sum(xsq[B_DK:], axis=0, keepdims=True)
        y = x * lax.rsqrt(jnp.where(first_map, ss_lo, ss_hi) * (1.0 / B_DK) + EPS) * gq
        q = _rotate_rows(y, B_ROT // 2, rope_t) * scale
        q_lo = jnp.where(first_map, q, 0.0).astype(BF16)
        q_hi = jnp.where(first_map, 0.0, q).astype(BF16)
        for a in range(cols // tq):
            cs = slice(a * tq, (a + 1) * tq)
            qt_ref[0, h, a, :LANES, :tq] = q_lo[:, cs]
            qt_ref[0, h, a, :LANES, tq:] = q_hi[:, cs]
            qt_ref[0, h, a, LANES:, :tq] = mask_rows[:, cs]
            qt_ref[0, h, a, LANES:, tq:] = mask_rows[:, cs]
        vt_ref[0, h, 0] = res[(heads + h) * LANES:(heads + h + 1) * LANES].astype(BF16)


def _proj_t_b(w_t, h_t, g_q, rope_t, batch, seq, heads, tq, tk):
    d, t = h_t.shape
    n_seq = seq // tk
    assert seq // CHUNK <= LANES
    tile = lambda i: (i // n_seq, 0, i % n_seq, 0, 0)
    blocks = w_t.size * 2 + d * tk * 2 + 3 * LANES * tk * 4 + heads * tk * (4 * LANES + B_DV) * 2
    return pl.pallas_call(
        functools.partial(_proj_t_b_kernel, heads=heads, scale=B_DK ** -0.5 * LOG2E, n_seq=n_seq, tq=tq),
        out_shape=(jax.ShapeDtypeStruct((batch, heads, seq // tq, 2 * LANES, 2 * tq), BF16),
                   jax.ShapeDtypeStruct((batch, heads, n_seq, B_DV, tk), BF16)),
        grid=(t // tk,),
        in_specs=[
            pl.BlockSpec(w_t.shape, lambda i: (0, 0)),
            pl.BlockSpec((d, tk), lambda i: (0, i)),
            pl.BlockSpec((LANES, 1), lambda i: (0, 0)),
            pl.BlockSpec((3 * LANES, tk), lambda i: (0, i % n_seq)),
        ],
        out_specs=(pl.BlockSpec((1, heads, tk // tq, 2 * LANES, 2 * tq), tile),
                   pl.BlockSpec((1, heads, 1, B_DV, tk), tile)),
        compiler_params=_params(("parallel",), blocks, 3 * w_t.shape[0] * tk * 4),
        name="diff_proj_qv",
    )(w_t, h_t, g_q, rope_t)


def _prep_b_kernel(k_ref, gk_ref, rope_ref, ko_ref, *, heads, n_seq):
    rows = k_ref.shape[0]
    pos0 = (pl.program_id(0) % n_seq) * rows
    rope = rope_ref[...]
    cos, sin_lo, sin_hi = rope[:, :LANES], rope[:, LANES:2 * LANES], rope[:, 2 * LANES:]
    first_map = lax.broadcasted_iota(jnp.int32, (1, LANES), 1) < B_DK
    gk = gk_ref[...]
    indicator = _chunk_indicator(pos0, rows, 0).astype(BF16)
    for h in range(heads):
        x = k_ref[:, h * LANES:(h + 1) * LANES].astype(F32)
        xsq = x * x
        ss_lo = jnp.sum(jnp.where(first_map, xsq, 0.0), axis=-1, keepdims=True)
        ss_hi = jnp.sum(jnp.where(first_map, 0.0, xsq), axis=-1, keepdims=True)
        y = x * lax.rsqrt(jnp.where(first_map, ss_lo, ss_hi) * (1.0 / B_DK) + EPS) * gk
        y = y * cos + pltpu.roll(y, LANES - B_ROT // 2, 1) * sin_lo + pltpu.roll(y, B_ROT // 2, 1) * sin_hi
        ko_ref[:, 2 * h * LANES:(2 * h + 1) * LANES] = y.astype(BF16)
        ko_ref[:, (2 * h + 1) * LANES:(2 * h + 2) * LANES] = indicator


def _prep_b(u, off_k, g_k, rope, seq, heads, t_rows):
    t = u.shape[0]
    w = heads * LANES
    n_seq = seq // t_rows
    return pl.pallas_call(
        functools.partial(_prep_b_kernel, heads=heads, n_seq=n_seq),
        out_shape=jax.ShapeDtypeStruct((t, 2 * w), BF16),
        grid=(t // t_rows,),
        in_specs=[
            pl.BlockSpec((t_rows, w), lambda i: (i, off_k // w)),
            pl.BlockSpec((1, LANES), lambda i: (0, 0)),
            pl.BlockSpec((t_rows, 3 * LANES), lambda i: (i % n_seq, 0)),
        ],
        out_specs=pl.BlockSpec((t_rows, 2 * w), lambda i: (i, 0)),
        compiler_params=_params(("parallel",), 3 * t_rows * w * 2 + t_rows * 3 * LANES * 4, 8 * t_rows * LANES * 4),
        name="diff_prep_k",
    )(u, g_k, rope)


def _attn_b_kernel(qt_ref, k_ref, vt_ref, z_ref, lam_ref, gsub_ref, o_ref, *scratch, tq, tk, lam_init):
    s_refs, p_refs, (acc_ref, l_ref) = scratch[:N_BUF], scratch[N_BUF:2 * N_BUF], scratch[2 * N_BUF:]
    _flash_loop(qt_ref, k_ref, vt_ref, s_refs, p_refs, acc_ref, l_ref, tk=tk, ratio=tk // tq)
    lf = lam_ref[...]
    lam = (jnp.exp(jnp.sum(lf[0:1] * lf[1:2], axis=-1, keepdims=True))
           - jnp.exp(jnp.sum(lf[2:3] * lf[3:4], axis=-1, keepdims=True)) + lam_init)
    gsub = gsub_ref[...]
    for qi in range(acc_ref.shape[0]):
        o = acc_ref[qi] * (1.0 / l_ref[qi])
        a = o[:, :tq] - lam * o[:, tq:]
        ss = jnp.sum(a * a, axis=0, keepdims=True) * (1.0 / B_DV)
        y = a * lax.rsqrt(ss + EPS) * gsub * (1.0 - lam_init)
        _silu_gate_store(o_ref, z_ref, slice(qi * tq, (qi + 1) * tq), y.T)


def _attn_b(qt, k, vt, u, z_off, lam, g_sub, lam_init, batch, seq, heads, tq, tk):
    n_q, n_k = seq // tq, seq // tk
    zb = z_off // LANES
    cols = 2 * tq
    seq_head = lambda b, h: (b, h)
    tiles = lambda b, h: (b, h, 0, 0, 0)
    blocks = 2 * seq * 2 * LANES * 2 + seq * 2 * LANES * 2 + 3 * seq * LANES * 2
    scratch = N_BUF * tk * cols * 6 + n_q * B_DV * cols * 4
    return pl.pallas_call(
        functools.partial(_attn_b_kernel, tq=tq, tk=tk, lam_init=lam_init),
        out_shape=jax.ShapeDtypeStruct((batch * seq, heads * LANES), BF16),
        grid=(batch, heads),
        in_specs=[
            pl.BlockSpec((1, 1, n_q, 2 * LANES, cols), tiles),
            pl.BlockSpec((seq, 2 * LANES), seq_head),
            pl.BlockSpec((1, 1, n_k, B_DV, tk), tiles),
            pl.BlockSpec((seq, LANES), lambda b, h: (b, zb + h)),
            pl.BlockSpec(lam.shape, lambda b, h: (0, 0)),
            pl.BlockSpec((B_DV, 1), lambda b, h: (0, 0)),
        ],
        out_specs=pl.BlockSpec((seq, LANES), seq_head),
        scratch_shapes=([pltpu.VMEM((tk, cols), F32)] * N_BUF + [pltpu.VMEM((tk, cols), BF16)] * N_BUF
                        + [pltpu.VMEM((n_q, B_DV, cols), F32), pltpu.VMEM((n_q, 1, cols), F32)]),
        compiler_params=_params(("parallel", "parallel"), blocks, scratch + 6 * tk * cols * 4),
        name="diff_attention",
    )(qt, k, vt, u, lam, g_sub)


C_PAD_SLOT = 64


def _proj_t_c_kernel(wt_ref, ht_ref, gq_ref, qt_ref, vt_ref, *, heads, scale, tq):
    cols = ht_ref.shape[1]
    r = pl.program_id(1)
    is_pad = r == 0
    pos0 = jnp.maximum(r - 1, 0) * cols
    res = jnp.dot(wt_ref[...], ht_ref[...], preferred_element_type=F32)
    gq = gq_ref[...]
    q_chunk = (pos0 + lax.broadcasted_iota(jnp.int32, (LANES, cols), 1)) // CHUNK
    slot = lax.broadcasted_iota(jnp.int32, (LANES, cols), 0)
    out_of_band = ((slot < C_PAD_SLOT) & ((slot > q_chunk) | (slot < q_chunk - C_LEFT_CHUNKS))) | (slot == C_PAD_SLOT)
    mask_rows = jnp.where(out_of_band, MASKED, 0.0).astype(BF16)
    for h in range(heads):
        x = res[h * LANES:(h + 1) * LANES]
        q = x * lax.rsqrt(jnp.sum(x * x, axis=0, keepdims=True) * (1.0 / C_DH) + EPS) * gq * scale
        q = q.astype(BF16)
        v = res[(heads + h) * LANES:(heads + h + 1) * LANES].astype(BF16)
        v = jnp.where(is_pad, jnp.zeros_like(v), v)
        for a in range(cols // tq):
            cs = slice(a * tq, (a + 1) * tq)
            qt_ref[0, h, a, :LANES, :] = q[:, cs]
            qt_ref[0, h, a, LANES:, :] = mask_rows[:, cs]
            vt_ref[0, h, a] = v[:, cs]


def _proj_t_c(w_t, h_t, g_q, batch, seq, heads, pad, tq):
    d, t = h_t.shape
    n_seq = seq // pad
    per = pad // tq
    assert seq // CHUNK <= C_PAD_SLOT
    blocks = w_t.size * 2 + d * pad * 2 + heads * pad * 3 * LANES * 2
    return pl.pallas_call(
        functools.partial(_proj_t_c_kernel, heads=heads, scale=C_DH ** -0.5 * LOG2E, tq=tq),
        out_shape=(jax.ShapeDtypeStruct((batch, heads, seq // tq, 2 * LANES, tq), BF16),
                   jax.ShapeDtypeStruct((batch, heads, (seq + pad) // tq, C_DH, tq), BF16)),
        grid=(batch, n_seq + 1),
        in_specs=[
            pl.BlockSpec(w_t.shape, lambda b, r: (0, 0)),
            pl.BlockSpec((d, pad), lambda b, r: (0, b * n_seq + jnp.maximum(r - 1, 0))),
            pl.BlockSpec((LANES, 1), lambda b, r: (0, 0)),
        ],
        out_specs=(pl.BlockSpec((1, heads, per, 2 * LANES, tq), lambda b, r: (b, 0, jnp.maximum(r - 1, 0), 0, 0)),
                   pl.BlockSpec((1, heads, per, C_DH, tq), lambda b, r: (b, 0, r, 0, 0))),
        compiler_params=_params(("parallel", "arbitrary"), blocks, 3 * w_t.shape[0] * pad * 4),
        name="band_proj_qv",
    )(w_t, h_t, g_q)


def _prep_c_kernel(k_ref, gk_ref, ko_ref, *, heads):
    rows = k_ref.shape[0]
    r = pl.program_id(1)
    is_pad = r == 0
    pos0 = jnp.maximum(r - 1, 0) * rows
    gk = gk_ref[...]
    lane = lax.broadcasted_iota(jnp.int32, (rows, LANES), 1)
    indicator = jnp.where(is_pad, jnp.where(lane == C_PAD_SLOT, 1.0, 0.0), _chunk_indicator(pos0, rows, 0))
    indicator = indicator.astype(BF16)
    for h in range(heads):
        kn = _rms(k_ref[:, h * LANES:(h + 1) * LANES].astype(F32), gk, C_DH).astype(BF16)
        ko_ref[:, 2 * h * LANES:(2 * h + 1) * LANES] = jnp.where(is_pad, jnp.zeros_like(kn), kn)
        ko_ref[:, (2 * h + 1) * LANES:(2 * h + 2) * LANES] = indicator


def _prep_c(u, off_k, g_k, batch, seq, heads, pad):
    w = heads * LANES
    n_seq = seq // pad
    return pl.pallas_call(
        functools.partial(_prep_c_kernel, heads=heads),
        out_shape=jax.ShapeDtypeStruct((batch * (seq + pad), 2 * w), BF16),
        grid=(batch, n_seq + 1),
        in_specs=[
            pl.BlockSpec((pad, w), lambda b, r: (b * n_seq + jnp.maximum(r - 1, 0), off_k // w)),
            pl.BlockSpec((1, LANES), lambda b, r: (0, 0)),
        ],
        out_specs=pl.BlockSpec((pad, 2 * w), lambda b, r: (b * (n_seq + 1) + r, 0)),
        compiler_params=_params(("parallel", "parallel"), 3 * pad * w * 2, 4 * pad * LANES * 4),
        name="band_prep_k",
    )(u, g_k)


def _attn_c_kernel(qt_ref, k_ref, vt_ref, z_ref, rel_ref, o_ref, *scratch, tq, tw):
    n_q = qt_ref.shape[2]
    s_refs, p_refs, (bias_ref,) = scratch[:N_BUF], scratch[N_BUF:2 * N_BUF], scratch[2 * N_BUF:]
    width = rel_ref.shape[-1]
    toeplitz = pltpu.roll(jnp.broadcast_to(rel_ref[0], (tq, width)), 0, 1, stride=1, stride_axis=0)
    bias_ref[...] = (toeplitz[:, :tw] * LOG2E).T

    def scores(i):
        return jnp.dot(k_ref[i * tq:i * tq + tw, :], qt_ref[0, 0, i], preferred_element_type=F32) + bias_ref[...]

    s_refs[0][...] = scores(0)
    for i in range(n_q + 1):
        cur, nxt, prv = i % N_BUF, (i + 1) % N_BUF, (i - 1) % N_BUF
        if i + 1 < n_q:
            s_refs[nxt][...] = scores(i + 1)
        if i < n_q:
            s = s_refs[cur][...]
            p = jnp.exp2(s - jnp.max(s, axis=0, keepdims=True))
            p_refs[cur][...] = p.astype(BF16)
            l = jnp.sum(p, axis=0, keepdims=True)
        if i > 0:
            acc = None
            for a in range(tw // tq):
                part = jnp.dot(vt_ref[0, 0, i - 1 + a], p_refs[prv][a * tq:(a + 1) * tq, :],
                               preferred_element_type=F32)
                acc = part if acc is None else acc + part
            _silu_gate_store(o_ref, z_ref, slice((i - 1) * tq, i * tq), (acc * (1.0 / l_prev)).T)
        l_prev = l


def _attn_c(qt, k, vt, u, z_off, rel_rows, batch, seq, heads, tq, pad):
    n_q = seq // tq
    tw = tq + pad
    zb = z_off // LANES
    tiles = lambda b, h: (b, h, 0, 0, 0)
    blocks = seq * 2 * LANES * 2 + (seq + pad) * 3 * LANES * 2 + 2 * seq * LANES * 2
    return pl.pallas_call(
        functools.partial(_attn_c_kernel, tq=tq, tw=tw),
        out_shape=jax.ShapeDtypeStruct((batch * seq, heads * LANES), BF16),
        grid=(batch, heads),
        in_specs=[
            pl.BlockSpec((1, 1, n_q, 2 * LANES, tq), tiles),
            pl.BlockSpec((seq + pad, 2 * LANES), lambda b, h: (b, h)),
            pl.BlockSpec((1, 1, (seq + pad) // tq, C_DH, tq), tiles),
            pl.BlockSpec((seq, LANES), lambda b, h: (b, zb + h)),
            pl.BlockSpec((1, 1, rel_rows.shape[-1]), lambda b, h: (h, 0, 0)),
        ],
        out_specs=pl.BlockSpec((seq, LANES), lambda b, h: (b, h)),
        scratch_shapes=([pltpu.VMEM((tw, tq), F32)] * N_BUF + [pltpu.VMEM((tw, tq), BF16)] * N_BUF
                        + [pltpu.VMEM((tw, tq), F32)]),
        compiler_params=_params(("parallel", "parallel"), blocks, 12 * tw * tq * 4),
        name="band_attention",
    )(qt, k, vt, u, rel_rows)


def _rope_table(seq, dim, theta, group):
    half = dim // 2
    inv = 1.0 / (jnp.float32(theta) ** (jnp.arange(0, dim, 2, dtype=F32) / dim))
    ang = jnp.arange(seq, dtype=F32)[:, None] * inv[None, :]
    cos, sin = jnp.cos(ang), jnp.sin(ang)
    lane = jnp.arange(LANES) % group
    idx = lane % half
    in_lo = (lane < half)[None, :]
    in_hi = ((lane >= half) & (lane < dim))[None, :]
    c = jnp.where(in_lo | in_hi, cos[:, idx], 1.0 if group < LANES else 0.0)
    s_lo = jnp.where(in_lo, -sin[:, idx], 0.0)
    s_hi = jnp.where(in_hi, sin[:, idx], 0.0)
    return jnp.concatenate([c, s_lo, s_hi], axis=1)


def _band_rel_rows(rel_bias, tq, pad):
    width = pl.next_power_of_2(2 * tq + pad)
    e = jnp.arange(width)
    e = jnp.where(e < tq + pad, e, e - width)
    rel = jnp.clip(pad - e, -(CHUNK - 1), C_REL_MAX) + (CHUNK - 1)
    return rel_bias.astype(F32)[:, None, rel]


def _row(v):
    return v.astype(F32).reshape(1, -1)


def _pad_lanes(v, width):
    return jnp.pad(v, ((0, 0),) * (v.ndim - 1) + ((0, width - v.shape[-1]),))


def _layer(x2, layer_idx, batch, seq, rope_a, rope_b, g_pre, w_main, w_t_b, w_t_c, w_kr, a_g_cq, a_g_ckv, a_w_uq,
           a_w_ukv, a_g_q, a_g_k, b_g_q, b_g_k, b_lam, b_g_sub, c_g_q, c_g_k, c_rel_bias, w_branch, w_out):
    d = x2.shape[1]
    bw = w_branch.shape[1]
    q_rank, kv_rank = a_w_uq.shape[0], a_w_ukv.shape[0]
    a_heads, b_heads, c_heads = bw // A_V, bw // B_DV, bw // C_DH

    o_az, o_bk, o_bz, o_ck, o_cz, o_g = (n * bw for n in range(6))
    o_acq = o_g + N_BRANCH * d
    o_ackv = o_acq + q_rank

    h, h_t = _rmsnorm(x2, _row(g_pre))
    u = _matmul(h, w_main, BF16, 1024, MAIN_COLUMN_TILE, "in_proj")

    hw = 2 * LANES
    t_a = min(512, seq)
    w_uq = _pad_lanes(a_w_uq.reshape(q_rank, a_heads, A_NOPE + A_ROPE), hw).reshape(q_rank, a_heads * hw)
    w_ukv = a_w_ukv.reshape(kv_rank, a_heads, A_NOPE + A_V)
    w_k = w_ukv[:, :, :A_NOPE].reshape(kv_rank, a_heads * A_NOPE).astype(BF16)
    w_vt = w_ukv[:, :, A_NOPE:].reshape(kv_rank, a_heads * A_V).T.astype(BF16)
    g_q = jnp.concatenate([_row(a_g_q[:A_NOPE]), _pad_lanes(_row(a_g_q[A_NOPE:]), LANES)], axis=1)
    g_k = jnp.concatenate([_row(a_g_k[:A_NOPE]), _pad_lanes(_row(a_g_k[A_NOPE:]), LANES)], axis=1)
    qa, ka, vta = _prep_a(u, h, w_kr, o_acq, o_ackv, _row(a_g_cq), _row(a_g_ckv), w_uq.T.astype(BF16), w_k, w_vt,
                          g_q.T, g_k, rope_a, rope_a.T, batch, seq, a_heads, t_a)
    o_a = _attn_a(qa, ka, vta, u, o_az, batch, seq, a_heads, t_a)

    tq_b, tk_b = min(256, seq), min(512, seq)
    qb, vtb = _proj_t_b(w_t_b, h_t, jnp.tile(_row(b_g_q), (1, 2)).T, rope_b.T, batch, seq, b_heads, tq_b, tk_b)
    kb = _prep_b(u, o_bk, jnp.tile(_row(b_g_k), (1, 2)), rope_b, seq, b_heads, tk_b)
    lam_init = 0.8 - 0.6 * math.exp(-0.3 * layer_idx)
    o_b = _attn_b(qb, kb, vtb, u, o_bz, b_lam.astype(F32), b_g_sub.astype(F32).reshape(B_DV, 1), lam_init,
                  batch, seq, b_heads, tq_b, tk_b)

    pad = C_LEFT_CHUNKS * CHUNK
    tq_c = min(256, seq)
    qc, vtc = _proj_t_c(w_t_c, h_t, _row(c_g_q).T, batch, seq, c_heads, pad, tq_c)
    kc = _prep_c(u, o_ck, _row(c_g_k), batch, seq, c_heads, pad)
    o_c = _attn_c(qc, kc, vtc, u, o_cz, _band_rel_rows(c_rel_bias, tq_c, pad), batch, seq, c_heads, tq_c, pad)

    y = _merge(o_a, o_b, o_c, w_branch.astype(BF16), u, o_g, 1024, 1024)
    return _matmul_residual(y, w_out.astype(BF16), x2, 1024, 1024)


def kernel(x, g_pre, w_in, a_g_cq, a_g_ckv, a_w_uq, a_w_ukv, a_g_q, a_g_k, b_g_q, b_g_k, b_lam, b_g_sub,
           c_g_q, c_g_k, c_rel_bias, w_branch, w_out):
    batch, seq, d = x.shape
    rope_a = _rope_table(seq, A_ROPE, A_ROPE_THETA, LANES)
    rope_b = _rope_table(seq, B_ROT, B_ROPE_THETA, B_DK)
    x2 = x.reshape(batch * seq, d)
    q_rank, kv_rank, bw = a_w_uq.shape[1], a_w_ukv.shape[1], w_branch.shape[2]
    n_used = 5 * bw + N_BRANCH * d + q_rank + kv_rank
    n_main = -(-n_used // MAIN_COLUMN_TILE) * MAIN_COLUMN_TILE
    w_main, w_t_b, w_t_c, w_kr = _weights(w_in, q_rank, kv_rank, bw, N_BRANCH * d, n_main)
    for l in range(g_pre.shape[0]):
        x2 = _layer(x2, l, batch, seq, rope_a, rope_b, g_pre[l], w_main[l], w_t_b[l], w_t_c[l], w_kr[l], a_g_cq[l],
                    a_g_ckv[l], a_w_uq[l], a_w_ukv[l], a_g_q[l], a_g_k[l], b_g_q[l], b_g_k[l], b_lam[l],
                    b_g_sub[l], c_g_q[l], c_g_k[l], c_rel_bias[l], w_branch[l], w_out[l])
    return x2.reshape(batch, seq, d)
```

```python
import functools
import math

import jax
import jax.numpy as jnp
from jax import lax
from jax.experimental import pallas as pl
from jax.experimental.pallas import tpu as pltpu

F32 = jnp.float32
BF16 = jnp.bfloat16

EPS = 1e-6
CHUNK = 64
MASKED = -1e30
LOG2E = math.log2(math.e)

A_NOPE, A_ROPE, A_V = 128, 64, 128
A_ROPE_THETA = 10000.0
B_DK, B_DV = 64, 128
B_ROT = B_DK // 4
B_ROPE_THETA = 500000.0
C_DH = 128
C_LEFT_CHUNKS = 8
C_REL_MAX = 128
N_BRANCH = 3

PROJ_HEAD_GROUP = 2
N_BUF = 3
MAIN_COLUMN_TILE = 2048
STEPS_PER_ITERATION = 12
LANES = 128
V7X_VMEM_BUDGET = 56 * 2**20


def _params(semantics, block_bytes, temp_bytes=0):
    need = 2 * block_bytes + temp_bytes + (4 << 20)
    return pltpu.CompilerParams(
        dimension_semantics=semantics,
        vmem_limit_bytes=int(min(max(need, 16 << 20), V7X_VMEM_BUDGET)),
    )


def _sigmoid(z):
    return 1.0 / (1.0 + jnp.exp(-z))


def _rms(x, g, n):
    ss = jnp.sum(x * x, axis=-1, keepdims=True) * (1.0 / n)
    return x * lax.rsqrt(ss + EPS) * g


def _rmsnorm_kernel(x_ref, g_ref, o_ref, ot_ref):
    x = x_ref[...]
    y = _rms(x, g_ref[...], x.shape[-1])
    o_ref[...] = y.astype(o_ref.dtype)
    ot_ref[...] = y.T.astype(ot_ref.dtype)


def _rmsnorm(x, g, tm=512):
    t, d = x.shape
    return pl.pallas_call(
        _rmsnorm_kernel,
        out_shape=(jax.ShapeDtypeStruct((t, d), BF16), jax.ShapeDtypeStruct((d, t), BF16)),
        grid=(t // tm,),
        in_specs=[pl.BlockSpec((tm, d), lambda i: (i, 0)), pl.BlockSpec((1, d), lambda i: (0, 0))],
        out_specs=(pl.BlockSpec((tm, d), lambda i: (i, 0)), pl.BlockSpec((d, tm), lambda i: (0, i))),
        compiler_params=_params(("parallel",), tm * d * 8, tm * d * 12),
        name="pre_rmsnorm",
    )(x, g)


def _mm_kernel(a_ref, b_ref, o_ref):
    o_ref[...] = jnp.dot(a_ref[...], b_ref[...], preferred_element_type=F32).astype(o_ref.dtype)


def _matmul(a, b, out_dtype, tm, tn, name):
    m, k = a.shape
    n = b.shape[1]
    blocks = tm * k * 2 + k * tn * 2 + tm * tn * jnp.dtype(out_dtype).itemsize
    return pl.pallas_call(
        _mm_kernel,
        out_shape=jax.ShapeDtypeStruct((m, n), out_dtype),
        grid=(m // tm, n // tn),
        in_specs=[pl.BlockSpec((tm, k), lambda i, j: (i, 0)), pl.BlockSpec((k, tn), lambda i, j: (0, j))],
        out_specs=pl.BlockSpec((tm, tn), lambda i, j: (i, j)),
        compiler_params=_params(("parallel", "parallel"), blocks, tm * tn * 4),
        name=name,
    )(a, b)


def _mm_res_kernel(a_ref, b_ref, x_ref, o_ref):
    o_ref[...] = x_ref[...] + jnp.dot(a_ref[...], b_ref[...], preferred_element_type=F32)


def _matmul_residual(a, b, x, tm, tn):
    m, k = a.shape
    n = b.shape[1]
    blocks = tm * k * 2 + k * tn * 2 + 2 * tm * tn * 4
    return pl.pallas_call(
        _mm_res_kernel,
        out_shape=jax.ShapeDtypeStruct((m, n), F32),
        grid=(m // tm, n // tn),
        in_specs=[
            pl.BlockSpec((tm, k), lambda i, j: (i, 0)),
            pl.BlockSpec((k, tn), lambda i, j: (0, j)),
            pl.BlockSpec((tm, tn), lambda i, j: (i, j)),
        ],
        out_specs=pl.BlockSpec((tm, tn), lambda i, j: (i, j)),
        compiler_params=_params(("parallel", "parallel"), blocks, tm * tn * 4),
        name="out_proj_residual",
    )(a, b, x)


def _weights_kernel(wt_ref, main_ref, tb_ref, tc_ref, kr_ref, *, q_rank, kv_rank, bw, n_gate):
    s_kr = q_rank + kv_rank
    s_az = s_kr + A_ROPE

    def piece(n):
        return wt_ref[0, s_az + n * bw:s_az + (n + 1) * bw, :]

    for slot, n in enumerate((0, 2, 4, 6, 8)):
        main_ref[:, slot * bw:(slot + 1) * bw] = piece(n).T.astype(BF16)
    o_g = 5 * bw
    for g in range(n_gate // bw):
        main_ref[:, o_g + g * bw:o_g + (g + 1) * bw] = piece(9 + g).T.astype(BF16)
    main_ref[:, o_g + n_gate:o_g + n_gate + s_kr] = wt_ref[0, :s_kr, :].T.astype(BF16)
    n_pad = main_ref.shape[1] - (o_g + n_gate + s_kr)
    main_ref[:, o_g + n_gate + s_kr:] = jnp.zeros((main_ref.shape[0], n_pad), BF16)
    kr_ref[:, :A_ROPE] = wt_ref[0, s_kr:s_az, :].T.astype(BF16)
    kr_ref[:, A_ROPE:] = jnp.zeros((kr_ref.shape[0], LANES - A_ROPE), BF16)
    tb_ref[:bw, :] = piece(1).astype(BF16)
    tb_ref[bw:, :] = piece(3).astype(BF16)
    tc_ref[:bw, :] = piece(5).astype(BF16)
    tc_ref[bw:, :] = piece(7).astype(BF16)


def _weights(w_in_t, layer, q_rank, kv_rank, bw, n_gate, n_main, cols=128):
    _, d_in, d = w_in_t.shape
    blocks = cols * d_in * 4 + cols * (n_main + LANES + 4 * bw) * 2
    return pl.pallas_call(
        functools.partial(_weights_kernel, q_rank=q_rank, kv_rank=kv_rank, bw=bw, n_gate=n_gate),
        out_shape=(jax.ShapeDtypeStruct((d, n_main), BF16),
                   jax.ShapeDtypeStruct((2 * bw, d), BF16),
                   jax.ShapeDtypeStruct((2 * bw, d), BF16),
                   jax.ShapeDtypeStruct((d, LANES), BF16)),
        grid=(d // cols,),
        in_specs=[pl.BlockSpec((1, d_in, cols), lambda r: (layer, 0, r))],
        out_specs=(pl.BlockSpec((cols, n_main), lambda r: (r, 0)),
                   pl.BlockSpec((2 * bw, cols), lambda r: (0, r)),
                   pl.BlockSpec((2 * bw, cols), lambda r: (0, r)),
                   pl.BlockSpec((cols, LANES), lambda r: (r, 0))),
        compiler_params=_params(("parallel",), blocks, 4 * cols * bw * 4),
        name="weights_prep",
    )(w_in_t)


def _merge_kernel(oa_ref, ob_ref, oc_ref, w_ref, ga_ref, gb_ref, gc_ref, y_ref):
    y = None
    for n, (o_ref, g_ref) in enumerate(((oa_ref, ga_ref), (ob_ref, gb_ref), (oc_ref, gc_ref))):
        gate = _sigmoid(g_ref[...].astype(F32))
        term = gate * jnp.dot(o_ref[...], w_ref[n], preferred_element_type=F32)
        y = term if y is None else y + term
    y_ref[...] = y.astype(y_ref.dtype)


def _merge(o_a, o_b, o_c, w_branch, u, gate_off, tm, tn):
    t, bw = o_a.shape
    d = w_branch.shape[2]
    nj = d // tn
    g0 = gate_off // tn
    branch_spec = pl.BlockSpec((tm, bw), lambda i, j: (i, 0))
    gate_specs = [pl.BlockSpec((tm, tn), functools.partial(lambda i, j, n: (i, g0 + n * nj + j), n=n))
                  for n in range(N_BRANCH)]
    blocks = 3 * tm * bw * 2 + N_BRANCH * bw * tn * 2 + 3 * tm * tn * 2 + tm * tn * 2
    return pl.pallas_call(
        _merge_kernel,
        out_shape=jax.ShapeDtypeStruct((t, d), BF16),
        grid=(t // tm, nj),
        in_specs=[branch_spec, branch_spec, branch_spec,
                  pl.BlockSpec((N_BRANCH, bw, tn), lambda i, j: (0, 0, j))] + gate_specs,
        out_specs=pl.BlockSpec((tm, tn), lambda i, j: (i, j)),
        compiler_params=_params(("parallel", "parallel"), blocks, 3 * tm * tn * 4),
        name="gated_merge",
    )(o_a, o_b, o_c, w_branch, u, u, u)


def _chunk_indicator(pos0, rows, first_lane):
    chunk = (pos0 + lax.broadcasted_iota(jnp.int32, (rows, LANES), 0)) // CHUNK
    lane = lax.broadcasted_iota(jnp.int32, (rows, LANES), 1)
    return jnp.where(lane - first_lane == chunk, 1.0, 0.0)


def _chunk_mask_rows(pos0, cols, first_row):
    chunk = (pos0 + lax.broadcasted_iota(jnp.int32, (LANES, cols), 1)) // CHUNK
    row = lax.broadcasted_iota(jnp.int32, (LANES, cols), 0)
    return jnp.where((row >= first_row) & (row - first_row > chunk), MASKED, 0.0)


def _flash_loop(qt_ref, k_ref, vt_ref, s_refs, p_refs, acc_ref, l_ref, *, tk, ratio):
    n_q, _, cols = acc_ref.shape

    def scores(qi, j):
        k = k_ref[pl.ds(pl.multiple_of(j * tk, tk), tk), :]
        return jnp.dot(k, qt_ref[0, 0, qi], preferred_element_type=F32)

    def step(cur, carry):
        nxt, prv = (cur + 1) % N_BUF, (cur - 1) % N_BUF
        qi, j, m, l, alpha_prev, qi_prev, j_prev = carry
        last = j == qi // ratio
        qi_next = jnp.where(last, jnp.minimum(qi + 1, n_q - 1), qi)
        j_next = jnp.where(last, 0, j + 1)
        s_refs[nxt][...] = scores(qi_next, j_next)
        s = s_refs[cur][...]
        m_new = jnp.maximum(m, jnp.max(s, axis=0, keepdims=True))
        alpha = jnp.exp2(m - m_new)
        p = jnp.exp2(s - m_new)
        p_refs[cur][...] = p.astype(BF16)
        l_new = alpha * l + jnp.sum(p, axis=0, keepdims=True)
        half_cols = cols // 2
        for c in range(2):
            cs = slice(c * half_cols, (c + 1) * half_cols)
            acc = alpha_prev[:, cs] * acc_ref[qi_prev, :, cs] + jnp.dot(
                vt_ref[0, 0, j_prev], p_refs[prv][:, cs], preferred_element_type=F32)
            acc_ref[qi_prev, :, cs] = acc
        l_ref[qi_prev] = l
        return qi_next, j_next, jnp.where(last, MASKED, m_new), l_new, alpha, qi, j

    def steps(count, carry):
        for i in range(count):
            carry = step(i % N_BUF, carry)
        return carry

    zero = jnp.int32(0)
    s_refs[0][...] = scores(zero, zero)
    p_refs[N_BUF - 1][...] = jnp.zeros(p_refs[0].shape, BF16)
    row = jnp.zeros((1, cols), F32)
    acc_ref[...] = jnp.zeros(acc_ref.shape, F32)
    carry = (zero, zero, row + MASKED, row, row, zero, zero)
    n_steps = sum(qi // ratio + 1 for qi in range(n_q)) + 1
    carry = lax.fori_loop(0, n_steps // STEPS_PER_ITERATION, lambda _, c: steps(STEPS_PER_ITERATION, c), carry)
    steps(n_steps % STEPS_PER_ITERATION, carry)


def _rotate_rows(y, half, rope_t):
    cos, sin_lo, sin_hi = rope_t[:LANES], rope_t[LANES:2 * LANES], rope_t[2 * LANES:]
    up = jnp.concatenate([y[half:], y[:half]], axis=0)
    down = jnp.concatenate([y[-half:], y[:-half]], axis=0)
    return y * cos + up * sin_lo + down * sin_hi


def _silu_gate_store(o_ref, z_ref, rows, y):
    z = z_ref[rows, :].astype(F32)
    o_ref[rows, :] = (y * (z * _sigmoid(z))).astype(o_ref.dtype)


def _prep_a_kernel(cq_ref, ckv_ref, h_ref, wkr_ref, gcq_ref, gckv_ref, wuqt_ref, wk_ref, wvt_ref, gq_ref, gk_ref,
                   rope_ref, ropet_ref, qt_ref, k_ref, vt_ref, *, heads, scale, n_seq):
    rows = cq_ref.shape[0]
    pos0 = (pl.program_id(0) % n_seq) * rows
    rope = rope_ref[...]
    rope_t = ropet_ref[...]
    cos, sin_lo, sin_hi = rope[:, :LANES], rope[:, LANES:2 * LANES], rope[:, 2 * LANES:]

    def rotate(y):
        return y * cos + pltpu.roll(y, LANES - A_ROPE // 2, 1) * sin_lo + pltpu.roll(y, A_ROPE // 2, 1) * sin_hi

    cq = cq_ref[...].astype(F32)
    cq = _rms(cq, gcq_ref[...], cq.shape[-1])
    ckv = ckv_ref[...].astype(F32)
    ckv = _rms(ckv, gckv_ref[...], ckv.shape[-1])
    qat = jnp.dot(wuqt_ref[...], cq.T.astype(BF16), preferred_element_type=F32)
    vat = jnp.dot(wvt_ref[...], ckv.T.astype(BF16), preferred_element_type=F32)
    kna = jnp.dot(ckv.astype(BF16), wk_ref[...], preferred_element_type=F32)
    gq = gq_ref[...]
    gk = gk_ref[...]
    kr = jnp.dot(h_ref[...], wkr_ref[...], preferred_element_type=F32)
    k_rope = (rotate(_rms(kr, gk[:, LANES:], A_ROPE)) + _chunk_indicator(pos0, rows, A_ROPE)).astype(BF16)
    mask_rows = _chunk_mask_rows(pos0, rows, A_ROPE)
    hw = 2 * LANES
    for h in range(heads):
        xn = qat[h * hw:h * hw + LANES]
        q_nope = xn * lax.rsqrt(jnp.sum(xn * xn, axis=0, keepdims=True) * (1.0 / A_NOPE) + EPS) * gq[:LANES]
        xr = qat[h * hw + LANES:(h + 1) * hw]
        q_rope = xr * lax.rsqrt(jnp.sum(xr * xr, axis=0, keepdims=True) * (1.0 / A_ROPE) + EPS) * gq[LANES:]
        q_rope = _rotate_rows(q_rope, A_ROPE // 2, rope_t)
        qt_ref[0, h, 0, :LANES, :] = (q_nope * scale).astype(BF16)
        qt_ref[0, h, 0, LANES:, :] = (q_rope * scale + mask_rows).astype(BF16)
        k_nope = _rms(kna[:, h * LANES:(h + 1) * LANES], gk[:, :LANES], A_NOPE)
        k_ref[:, h * hw:h * hw + LANES] = k_nope.astype(BF16)
        k_ref[:, h * hw + LANES:(h + 1) * hw] = k_rope
        vt_ref[0, h, 0] = vat[h * LANES:(h + 1) * LANES].astype(BF16)


def _prep_a(u, h, w_kr, off_cq, off_ckv, g_cq, g_ckv, w_uqt, w_k, w_vt, g_q, g_k, rope, rope_t, batch, seq, heads,
            t_rows):
    t = u.shape[0]
    q_rank, kv_rank = w_uqt.shape[1], w_k.shape[0]
    hw = 2 * LANES
    n_seq = seq // t_rows
    assert seq // CHUNK <= LANES - A_ROPE
    scale = (A_NOPE + A_ROPE) ** -0.5 * LOG2E
    row = lambda i: (i, 0)
    const = lambda i: (0, 0)
    tile = lambda i: (i // n_seq, 0, i % n_seq, 0, 0)
    d = h.shape[1]
    blocks = (t_rows * (q_rank + kv_rank + d) * 2 + d * LANES * 2 + (q_rank * hw + 2 * kv_rank * LANES) * heads * 2
              + 2 * t_rows * 3 * LANES * 4 + t_rows * heads * (2 * hw + LANES) * 2)
    return pl.pallas_call(
        functools.partial(_prep_a_kernel, heads=heads, scale=scale, n_seq=n_seq),
        out_shape=(jax.ShapeDtypeStruct((batch, heads, n_seq, hw, t_rows), BF16),
                   jax.ShapeDtypeStruct((t, heads * hw), BF16),
                   jax.ShapeDtypeStruct((batch, heads, n_seq, A_V, t_rows), BF16)),
        grid=(t // t_rows,),
        in_specs=[
            pl.BlockSpec((t_rows, q_rank), lambda i: (i, off_cq // q_rank)),
            pl.BlockSpec((t_rows, kv_rank), lambda i: (i, off_ckv // kv_rank)),
            pl.BlockSpec((t_rows, d), row),
            pl.BlockSpec((d, LANES), const),
            pl.BlockSpec((1, q_rank), const),
            pl.BlockSpec((1, kv_rank), const),
            pl.BlockSpec((heads * hw, q_rank), const),
            pl.BlockSpec((kv_rank, heads * LANES), const),
            pl.BlockSpec((heads * LANES, kv_rank), const),
            pl.BlockSpec((hw, 1), const),
            pl.BlockSpec((1, hw), const),
            pl.BlockSpec((t_rows, 3 * LANES), lambda i: (i % n_seq, 0)),
            pl.BlockSpec((3 * LANES, t_rows), lambda i: (0, i % n_seq)),
        ],
        out_specs=(pl.BlockSpec((1, heads, 1, hw, t_rows), tile),
                   pl.BlockSpec((t_rows, heads * hw), row),
                   pl.BlockSpec((1, heads, 1, A_V, t_rows), tile)),
        compiler_params=_params(("parallel",), blocks, 4 * t_rows * heads * hw * 4),
        name="mla_prep",
    )(u, u, h, w_kr, g_cq, g_ckv, w_uqt, w_k, w_vt, g_q, g_k, rope, rope_t)


def _attn_a_kernel(qt_ref, k_ref, vt_ref, z_ref, o_ref, *scratch, t):
    s_refs, p_refs, (acc_ref, l_ref) = scratch[:N_BUF], scratch[N_BUF:2 * N_BUF], scratch[2 * N_BUF:]
    _flash_loop(qt_ref, k_ref, vt_ref, s_refs, p_refs, acc_ref, l_ref, tk=t, ratio=1)
    for qi in range(acc_ref.shape[0]):
        _silu_gate_store(o_ref, z_ref, slice(qi * t, (qi + 1) * t), (acc_ref[qi] * (1.0 / l_ref[qi])).T)


def _attn_a(qt, k, vt, u, z_off, batch, seq, heads, t):
    n_tiles = seq // t
    dk = k.shape[1] // heads
    zb = z_off // LANES
    seq_head = lambda b, h: (b, h)
    tiles = lambda b, h: (b, h, 0, 0, 0)
    blocks = 2 * seq * dk * 2 + 3 * seq * LANES * 2
    scratch = N_BUF * t * t * 6 + n_tiles * A_V * t * 4
    return pl.pallas_call(
        functools.partial(_attn_a_kernel, t=t),
        out_shape=jax.ShapeDtypeStruct((batch * seq, heads * LANES), BF16),
        grid=(batch, heads),
        in_specs=[
            pl.BlockSpec((1, 1, n_tiles, dk, t), tiles),
            pl.BlockSpec((seq, dk), seq_head),
            pl.BlockSpec((1, 1, n_tiles, A_V, t), tiles),
            pl.BlockSpec((seq, LANES), lambda b, h: (b, zb + h)),
        ],
        out_specs=pl.BlockSpec((seq, LANES), seq_head),
        scratch_shapes=([pltpu.VMEM((t, t), F32)] * N_BUF + [pltpu.VMEM((t, t), BF16)] * N_BUF
                        + [pltpu.VMEM((n_tiles, A_V, t), F32), pltpu.VMEM((n_tiles, 1, t), F32)]),
        compiler_params=_params(("parallel", "parallel"), blocks, scratch + 6 * t * t * 4),
        name="mla_attention",
    )(qt, k, vt, u)


def _proj_t_b_kernel(wt_ref, ht_ref, gq_ref, ropet_ref, qt_ref, vt_ref, *, heads, scale, n_seq, tq):
    cols = ht_ref.shape[1]
    pos0 = (pl.program_id(0) % n_seq) * cols
    ht = ht_ref[...]
    group = PROJ_HEAD_GROUP * LANES
    rope_t = ropet_ref[...]
    gq = gq_ref[...]
    first_map = lax.broadcasted_iota(jnp.int32, (LANES, 1), 0) < B_DK
    mask_rows = _chunk_mask_rows(pos0, cols, 0).astype(BF16)
    for h in range(heads):
        if h % PROJ_HEAD_GROUP == 0:
            res_q = jnp.dot(wt_ref[h * LANES:h * LANES + group], ht, preferred_element_type=F32)
            res_v = jnp.dot(wt_ref[(heads + h) * LANES:(heads + h) * LANES + group], ht,
                            preferred_element_type=F32)
        g = h % PROJ_HEAD_GROUP
        x = res_q[g * LANES:(g + 1) * LANES]
        xsq = x * x
        ss_lo = jnp.sum(xsq[:B_DK], axis=0, keepdims=True)
        ss_hi = jnp.sum(xsq[B_DK:], axis=0, keepdims=True)
        y = x * lax.rsqrt(jnp.where(first_map, ss_lo, ss_hi) * (1.0 / B_DK) + EPS) * gq
        q = _rotate_rows(y, B_ROT // 2, rope_t) * scale
        q_lo = jnp.where(first_map, q, 0.0).astype(BF16)
        q_hi = jnp.where(first_map, 0.0, q).astype(BF16)
        for a in range(cols // tq):
            cs = slice(a * tq, (a + 1) * tq)
            qt_ref[0, h, a, :LANES, :tq] = q_lo[:, cs]
            qt_ref[0, h, a, :LANES, tq:] = q_hi[:, cs]
            qt_ref[0, h, a, LANES:, :tq] = mask_rows[:, cs]
            qt_ref[0, h, a, LANES:, tq:] = mask_rows[:, cs]
        vt_ref[0, h, 0] = res_v[g * LANES:(g + 1) * LANES].astype(BF16)


def _proj_t_b(w_t, h_t, g_q, rope_t, batch, seq, heads, tq, tk):
    d, t = h_t.shape
    n_seq = seq // tk
    assert seq // CHUNK <= LANES
    tile = lambda i: (i // n_seq, 0, i % n_seq, 0, 0)
    blocks = w_t.size * 2 + d * tk * 2 + 3 * LANES * tk * 4 + heads * tk * (4 * LANES + B_DV) * 2
    return pl.pallas_call(
        functools.partial(_proj_t_b_kernel, heads=heads, scale=B_DK ** -0.5 * LOG2E, n_seq=n_seq, tq=tq),
        out_shape=(jax.ShapeDtypeStruct((batch, heads, seq // tq, 2 * LANES, 2 * tq), BF16),
                   jax.ShapeDtypeStruct((batch, heads, n_seq, B_DV, tk), BF16)),
        grid=(t // tk,),
        in_specs=[
            pl.BlockSpec(w_t.shape, lambda i: (0, 0)),
            pl.BlockSpec((d, tk), lambda i: (0, i)),
            pl.BlockSpec((LANES, 1), lambda i: (0, 0)),
            pl.BlockSpec((3 * LANES, tk), lambda i: (0, i % n_seq)),
        ],
        out_specs=(pl.BlockSpec((1, heads, tk // tq, 2 * LANES, 2 * tq), tile),
                   pl.BlockSpec((1, heads, 1, B_DV, tk), tile)),
        compiler_params=_params(("parallel",), blocks, 3 * w_t.shape[0] * tk * 4),
        name="diff_proj_qv",
    )(w_t, h_t, g_q, rope_t)


def _prep_b_kernel(k_ref, gk_ref, rope_ref, ko_ref, *, heads, n_seq):
    rows = k_ref.shape[0]
    pos0 = (pl.program_id(0) % n_seq) * rows
    rope = rope_ref[...]
    cos, sin_lo, sin_hi = rope[:, :LANES], rope[:, LANES:2 * LANES], rope[:, 2 * LANES:]
    first_map = lax.broadcasted_iota(jnp.int32, (1, LANES), 1) < B_DK
    gk = gk_ref[...]
    indicator = _chunk_indicator(pos0, rows, 0).astype(BF16)
    for h in range(heads):
        x = k_ref[:, h * LANES:(h + 1) * LANES].astype(F32)
        xsq = x * x
        ss_lo = jnp.sum(jnp.where(first_map, xsq, 0.0), axis=-1, keepdims=True)
        ss_hi = jnp.sum(jnp.where(first_map, 0.0, xsq), axis=-1, keepdims=True)
        y = x * lax.rsqrt(jnp.where(first_map, ss_lo, ss_hi) * (1.0 / B_DK) + EPS) * gk
        y = y * cos + pltpu.roll(y, LANES - B_ROT // 2, 1) * sin_lo + pltpu.roll(y, B_ROT // 2, 1) * sin_hi
        ko_ref[:, 2 * h * LANES:(2 * h + 1) * LANES] = y.astype(BF16)
        ko_ref[:, (2 * h + 1) * LANES:(2 * h + 2) * LANES] = indicator


def _prep_b(u, off_k, g_k, rope, seq, heads, t_rows):
    t = u.shape[0]
    w = heads * LANES
    n_seq = seq // t_rows
    return pl.pallas_call(
        functools.partial(_prep_b_kernel, heads=heads, n_seq=n_seq),
        out_shape=jax.ShapeDtypeStruct((t, 2 * w), BF16),
        grid=(t // t_rows,),
        in_specs=[
            pl.BlockSpec((t_rows, w), lambda i: (i, off_k // w)),
            pl.BlockSpec((1, LANES), lambda i: (0, 0)),
            pl.BlockSpec((t_rows, 3 * LANES), lambda i: (i % n_seq, 0)),
        ],
        out_specs=pl.BlockSpec((t_rows, 2 * w), lambda i: (i, 0)),
        compiler_params=_params(("parallel",), 3 * t_rows * w * 2 + t_rows * 3 * LANES * 4, 8 * t_rows * LANES * 4),
        name="diff_prep_k",
    )(u, g_k, rope)


def _attn_b_kernel(qt_ref, k_ref, vt_ref, z_ref, lam_ref, gsub_ref, o_ref, *scratch, tq, tk, lam_init):
    s_refs, p_refs, (acc_ref, l_ref) = scratch[:N_BUF], scratch[N_BUF:2 * N_BUF], scratch[2 * N_BUF:]
    _flash_loop(qt_ref, k_ref, vt_ref, s_refs, p_refs, acc_ref, l_ref, tk=tk, ratio=tk // tq)
    lf = lam_ref[...]
    lam = (jnp.exp(jnp.sum(lf[0:1] * lf[1:2], axis=-1, keepdims=True))
           - jnp.exp(jnp.sum(lf[2:3] * lf[3:4], axis=-1, keepdims=True)) + lam_init)
    gsub = gsub_ref[...]
    for qi in range(acc_ref.shape[0]):
        o = acc_ref[qi] * (1.0 / l_ref[qi])
        a = o[:, :tq] - lam * o[:, tq:]
        ss = jnp.sum(a * a, axis=0, keepdims=True) * (1.0 / B_DV)
        y = a * lax.rsqrt(ss + EPS) * gsub * (1.0 - lam_init)
        _silu_gate_store(o_ref, z_ref, slice(qi * tq, (qi + 1) * tq), y.T)


def _attn_b(qt, k, vt, u, z_off, lam, g_sub, lam_init, batch, seq, heads, tq, tk):
    n_q, n_k = seq // tq, seq // tk
    zb = z_off // LANES
    cols = 2 * tq
    seq_head = lambda b, h: (b, h)
    tiles = lambda b, h: (b, h, 0, 0, 0)
    blocks = 2 * seq * 2 * LANES * 2 + seq * 2 * LANES * 2 + 3 * seq * LANES * 2
    scratch = N_BUF * tk * cols * 6 + n_q * B_DV * cols * 4
    return pl.pallas_call(
        functools.partial(_attn_b_kernel, tq=tq, tk=tk, lam_init=lam_init),
        out_shape=jax.ShapeDtypeStruct((batch * seq, heads * LANES), BF16),
        grid=(batch, heads),
        in_specs=[
            pl.BlockSpec((1, 1, n_q, 2 * LANES, cols), tiles),
            pl.BlockSpec((seq, 2 * LANES), seq_head),
            pl.BlockSpec((1, 1, n_k, B_DV, tk), tiles),
            pl.BlockSpec((seq, LANES), lambda b, h: (b, zb + h)),
            pl.BlockSpec(lam.shape, lambda b, h: (0, 0)),
            pl.BlockSpec((B_DV, 1), lambda b, h: (0, 0)),
        ],
        out_specs=pl.BlockSpec((seq, LANES), seq_head),
        scratch_shapes=([pltpu.VMEM((tk, cols), F32)] * N_BUF + [pltpu.VMEM((tk, cols), BF16)] * N_BUF
                        + [pltpu.VMEM((n_q, B_DV, cols), F32), pltpu.VMEM((n_q, 1, cols), F32)]),
        compiler_params=_params(("parallel", "parallel"), blocks, scratch + 6 * tk * cols * 4),
        name="diff_attention",
    )(qt, k, vt, u, lam, g_sub)


C_PAD_SLOT = 64


def _proj_t_c_kernel(wt_ref, ht_ref, gq_ref, qt_ref, vt_ref, *, heads, scale, tq):
    cols = ht_ref.shape[1]
    r = pl.program_id(1)
    is_pad = r == 0
    pos0 = jnp.maximum(r - 1, 0) * cols
    ht = ht_ref[...]
    group = PROJ_HEAD_GROUP * LANES
    gq = gq_ref[...]
    q_chunk = (pos0 + lax.broadcasted_iota(jnp.int32, (LANES, cols), 1)) // CHUNK
    slot = lax.broadcasted_iota(jnp.int32, (LANES, cols), 0)
    out_of_band = ((slot < C_PAD_SLOT) & ((slot > q_chunk) | (slot < q_chunk - C_LEFT_CHUNKS))) | (slot == C_PAD_SLOT)
    mask_rows = jnp.where(out_of_band, MASKED, 0.0).astype(BF16)
    for h in range(heads):
        if h % PROJ_HEAD_GROUP == 0:
            res_q = jnp.dot(wt_ref[h * LANES:h * LANES + group], ht, preferred_element_type=F32)
            res_v = jnp.dot(wt_ref[(heads + h) * LANES:(heads + h) * LANES + group], ht,
                            preferred_element_type=F32)
        g = h % PROJ_HEAD_GROUP
        x = res_q[g * LANES:(g + 1) * LANES]
        q = x * lax.rsqrt(jnp.sum(x * x, axis=0, keepdims=True) * (1.0 / C_DH) + EPS) * gq * scale
        q = q.astype(BF16)
        v = res_v[g * LANES:(g + 1) * LANES].astype(BF16)
        v = jnp.where(is_pad, jnp.zeros_like(v), v)
        for a in range(cols // tq):
            cs = slice(a * tq, (a + 1) * tq)
            qt_ref[0, h, a, :LANES, :] = q[:, cs]
            qt_ref[0, h, a, LANES:, :] = mask_rows[:, cs]
            vt_ref[0, h, a] = v[:, cs]


def _proj_t_c(w_t, h_t, g_q, batch, seq, heads, pad, tq):
    d, t = h_t.shape
    n_seq = seq // pad
    per = pad // tq
    assert seq // CHUNK <= C_PAD_SLOT
    blocks = w_t.size * 2 + d * pad * 2 + heads * pad * 3 * LANES * 2
    return pl.pallas_call(
        functools.partial(_proj_t_c_kernel, heads=heads, scale=C_DH ** -0.5 * LOG2E, tq=tq),
        out_shape=(jax.ShapeDtypeStruct((batch, heads, seq // tq, 2 * LANES, tq), BF16),
                   jax.ShapeDtypeStruct((batch, heads, (seq + pad) // tq, C_DH, tq), BF16)),
        grid=(batch, n_seq + 1),
        in_specs=[
            pl.BlockSpec(w_t.shape, lambda b, r: (0, 0)),
            pl.BlockSpec((d, pad), lambda b, r: (0, b * n_seq + jnp.maximum(r - 1, 0))),
            pl.BlockSpec((LANES, 1), lambda b, r: (0, 0)),
        ],
        out_specs=(pl.BlockSpec((1, heads, per, 2 * LANES, tq), lambda b, r: (b, 0, jnp.maximum(r - 1, 0), 0, 0)),
                   pl.BlockSpec((1, heads, per, C_DH, tq), lambda b, r: (b, 0, r, 0, 0))),
        compiler_params=_params(("parallel", "arbitrary"), blocks, 3 * w_t.shape[0] * pad * 4),
        name="band_proj_qv",
    )(w_t, h_t, g_q)


def _prep_c_kernel(k_ref, gk_ref, ko_ref, *, heads):
    rows = k_ref.shape[0]
    r = pl.program_id(1)
    is_pad = r == 0
    pos0 = jnp.maximum(r - 1, 0) * rows
    gk = gk_ref[...]
    lane = lax.broadcasted_iota(jnp.int32, (rows, LANES), 1)
    indicator = jnp.where(is_pad, jnp.where(lane == C_PAD_SLOT, 1.0, 0.0), _chunk_indicator(pos0, rows, 0))
    indicator = indicator.astype(BF16)
    for h in range(heads):
        kn = _rms(k_ref[:, h * LANES:(h + 1) * LANES].astype(F32), gk, C_DH).astype(BF16)
        ko_ref[:, 2 * h * LANES:(2 * h + 1) * LANES] = jnp.where(is_pad, jnp.zeros_like(kn), kn)
        ko_ref[:, (2 * h + 1) * LANES:(2 * h + 2) * LANES] = indicator


def _prep_c(u, off_k, g_k, batch, seq, heads, pad):
    w = heads * LANES
    n_seq = seq // pad
    return pl.pallas_call(
        functools.partial(_prep_c_kernel, heads=heads),
        out_shape=jax.ShapeDtypeStruct((batch * (seq + pad), 2 * w), BF16),
        grid=(batch, n_seq + 1),
        in_specs=[
            pl.BlockSpec((pad, w), lambda b, r: (b * n_seq + jnp.maximum(r - 1, 0), off_k // w)),
            pl.BlockSpec((1, LANES), lambda b, r: (0, 0)),
        ],
        out_specs=pl.BlockSpec((pad, 2 * w), lambda b, r: (b * (n_seq + 1) + r, 0)),
        compiler_params=_params(("parallel", "parallel"), 3 * pad * w * 2, 4 * pad * LANES * 4),
        name="band_prep_k",
    )(u, g_k)


def _attn_c_kernel(qt_ref, k_ref, vt_ref, z_ref, rel_ref, o_ref, *scratch, tq, tw):
    n_q = qt_ref.shape[2]
    s_refs, p_refs, (bias_ref,) = scratch[:N_BUF], scratch[N_BUF:2 * N_BUF], scratch[2 * N_BUF:]
    width = rel_ref.shape[-1]
    toeplitz = pltpu.roll(jnp.broadcast_to(rel_ref[0], (tq, width)), 0, 1, stride=1, stride_axis=0)
    bias_ref[...] = (toeplitz[:, :tw] * LOG2E).T

    def scores(i):
        return jnp.dot(k_ref[i * tq:i * tq + tw, :], qt_ref[0, 0, i], preferred_element_type=F32) + bias_ref[...]

    s_refs[0][...] = scores(0)
    for i in range(n_q + 1):
        cur, nxt, prv = i % N_BUF, (i + 1) % N_BUF, (i - 1) % N_BUF
        if i + 1 < n_q:
            s_refs[nxt][...] = scores(i + 1)
        if i < n_q:
            s = s_refs[cur][...]
            p = jnp.exp2(s - jnp.max(s, axis=0, keepdims=True))
            p_refs[cur][...] = p.astype(BF16)
            l = jnp.sum(p, axis=0, keepdims=True)
        if i > 0:
            acc = None
            for a in range(tw // tq):
                part = jnp.dot(vt_ref[0, 0, i - 1 + a], p_refs[prv][a * tq:(a + 1) * tq, :],
                               preferred_element_type=F32)
                acc = part if acc is None else acc + part
            _silu_gate_store(o_ref, z_ref, slice((i - 1) * tq, i * tq), (acc * (1.0 / l_prev)).T)
        l_prev = l


def _attn_c(qt, k, vt, u, z_off, rel_rows, batch, seq, heads, tq, pad):
    n_q = seq // tq
    tw = tq + pad
    zb = z_off // LANES
    tiles = lambda b, h: (b, h, 0, 0, 0)
    blocks = seq * 2 * LANES * 2 + (seq + pad) * 3 * LANES * 2 + 2 * seq * LANES * 2
    return pl.pallas_call(
        functools.partial(_attn_c_kernel, tq=tq, tw=tw),
        out_shape=jax.ShapeDtypeStruct((batch * seq, heads * LANES), BF16),
        grid=(batch, heads),
        in_specs=[
            pl.BlockSpec((1, 1, n_q, 2 * LANES, tq), tiles),
            pl.BlockSpec((seq + pad, 2 * LANES), lambda b, h: (b, h)),
            pl.BlockSpec((1, 1, (seq + pad) // tq, C_DH, tq), tiles),
            pl.BlockSpec((seq, LANES), lambda b, h: (b, zb + h)),
            pl.BlockSpec((1, 1, rel_rows.shape[-1]), lambda b, h: (h, 0, 0)),
        ],
        out_specs=pl.BlockSpec((seq, LANES), lambda b, h: (b, h)),
        scratch_shapes=([pltpu.VMEM((tw, tq), F32)] * N_BUF + [pltpu.VMEM((tw, tq), BF16)] * N_BUF
                        + [pltpu.VMEM((tw, tq), F32)]),
        compiler_params=_params(("parallel", "parallel"), blocks, 12 * tw * tq * 4),
        name="band_attention",
    )(qt, k, vt, u, rel_rows)


def _rope_table(seq, dim, theta, group):
    half = dim // 2
    inv = 1.0 / (jnp.float32(theta) ** (jnp.arange(0, dim, 2, dtype=F32) / dim))
    ang = jnp.arange(seq, dtype=F32)[:, None] * inv[None, :]
    cos, sin = jnp.cos(ang), jnp.sin(ang)
    lane = jnp.arange(LANES) % group
    idx = lane % half
    in_lo = (lane < half)[None, :]
    in_hi = ((lane >= half) & (lane < dim))[None, :]
    c = jnp.where(in_lo | in_hi, cos[:, idx], 1.0 if group < LANES else 0.0)
    s_lo = jnp.where(in_lo, -sin[:, idx], 0.0)
    s_hi = jnp.where(in_hi, sin[:, idx], 0.0)
    return jnp.concatenate([c, s_lo, s_hi], axis=1)


def _band_rel_rows(rel_bias, tq, pad):
    width = pl.next_power_of_2(2 * tq + pad)
    e = jnp.arange(width)
    e = jnp.where(e < tq + pad, e, e - width)
    rel = jnp.clip(pad - e, -(CHUNK - 1), C_REL_MAX) + (CHUNK - 1)
    return rel_bias.astype(F32)[:, None, rel]


def _row(v):
    return v.astype(F32).reshape(1, -1)


def _pad_lanes(v, width):
    return jnp.pad(v, ((0, 0),) * (v.ndim - 1) + ((0, width - v.shape[-1]),))


def _layer(x2, layer_idx, batch, seq, rope_a, rope_b, g_pre, w_main, w_t_b, w_t_c, w_kr, a_g_cq, a_g_ckv, a_w_uq,
           a_w_ukv, a_g_q, a_g_k, b_g_q, b_g_k, b_lam, b_g_sub, c_g_q, c_g_k, c_rel_bias, w_branch, w_out):
    d = x2.shape[1]
    bw = w_branch.shape[1]
    q_rank, kv_rank = a_w_uq.shape[0], a_w_ukv.shape[0]
    a_heads, b_heads, c_heads = bw // A_V, bw // B_DV, bw // C_DH

    o_az, o_bk, o_bz, o_ck, o_cz, o_g = (n * bw for n in range(6))
    o_acq = o_g + N_BRANCH * d
    o_ackv = o_acq + q_rank

    h, h_t = _rmsnorm(x2, _row(g_pre))
    u = _matmul(h, w_main, BF16, 1024, MAIN_COLUMN_TILE, "in_proj")

    hw = 2 * LANES
    t_a = min(512, seq)
    w_uq = _pad_lanes(a_w_uq.reshape(q_rank, a_heads, A_NOPE + A_ROPE), hw).reshape(q_rank, a_heads * hw)
    w_ukv = a_w_ukv.reshape(kv_rank, a_heads, A_NOPE + A_V)
    w_k = w_ukv[:, :, :A_NOPE].reshape(kv_rank, a_heads * A_NOPE).astype(BF16)
    w_vt = w_ukv[:, :, A_NOPE:].reshape(kv_rank, a_heads * A_V).T.astype(BF16)
    g_q = jnp.concatenate([_row(a_g_q[:A_NOPE]), _pad_lanes(_row(a_g_q[A_NOPE:]), LANES)], axis=1)
    g_k = jnp.concatenate([_row(a_g_k[:A_NOPE]), _pad_lanes(_row(a_g_k[A_NOPE:]), LANES)], axis=1)
    qa, ka, vta = _prep_a(u, h, w_kr, o_acq, o_ackv, _row(a_g_cq), _row(a_g_ckv), w_uq.T.astype(BF16), w_k, w_vt,
                          g_q.T, g_k, rope_a, rope_a.T, batch, seq, a_heads, t_a)
    o_a = _attn_a(qa, ka, vta, u, o_az, batch, seq, a_heads, t_a)

    tq_b, tk_b = min(256, seq), min(512, seq)
    qb, vtb = _proj_t_b(w_t_b, h_t, jnp.tile(_row(b_g_q), (1, 2)).T, rope_b.T, batch, seq, b_heads, tq_b, tk_b)
    kb = _prep_b(u, o_bk, jnp.tile(_row(b_g_k), (1, 2)), rope_b, seq, b_heads, tk_b)
    lam_init = 0.8 - 0.6 * math.exp(-0.3 * layer_idx)
    o_b = _attn_b(qb, kb, vtb, u, o_bz, b_lam.astype(F32), b_g_sub.astype(F32).reshape(B_DV, 1), lam_init,
                  batch, seq, b_heads, tq_b, tk_b)

    pad = C_LEFT_CHUNKS * CHUNK
    tq_c = min(256, seq)
    qc, vtc = _proj_t_c(w_t_c, h_t, _row(c_g_q).T, batch, seq, c_heads, pad, tq_c)
    kc = _prep_c(u, o_ck, _row(c_g_k), batch, seq, c_heads, pad)
    o_c = _attn_c(qc, kc, vtc, u, o_cz, _band_rel_rows(c_rel_bias, tq_c, pad), batch, seq, c_heads, tq_c, pad)

    y = _merge(o_a, o_b, o_c, w_branch.astype(BF16), u, o_g, 1024, 1024)
    return _matmul_residual(y, w_out.astype(BF16), x2, 1024, 1024)


def kernel(x, g_pre, w_in, a_g_cq, a_g_ckv, a_w_uq, a_w_ukv, a_g_q, a_g_k, b_g_q, b_g_k, b_lam, b_g_sub,
           c_g_q, c_g_k, c_rel_bias, w_branch, w_out):
    batch, seq, d = x.shape
    rope_a = _rope_table(seq, A_ROPE, A_ROPE_THETA, LANES)
    rope_b = _rope_table(seq, B_ROT, B_ROPE_THETA, B_DK)
    x2 = x.reshape(batch * seq, d)
    q_rank, kv_rank, bw = a_w_uq.shape[1], a_w_ukv.shape[1], w_branch.shape[2]
    n_used = 5 * bw + N_BRANCH * d + q_rank + kv_rank
    n_main = -(-n_used // MAIN_COLUMN_TILE) * MAIN_COLUMN_TILE
    w_in_t = jnp.swapaxes(w_in, 1, 2)
    for l in range(g_pre.shape[0]):
        w_main, w_t_b, w_t_c, w_kr = _weights(w_in_t, l, q_rank, kv_rank, bw, N_BRANCH * d, n_main)
        x2 = _layer(x2, l, batch, seq, rope_a, rope_b, g_pre[l], w_main, w_t_b, w_t_c, w_kr, a_g_cq[l],
                    a_g_ckv[l], a_w_uq[l], a_w_ukv[l], a_g_q[l], a_g_k[l], b_g_q[l], b_g_k[l], b_lam[l],
                    b_g_sub[l], c_g_q[l], c_g_k[l], c_rel_bias[l], w_branch[l], w_out[l])
    return x2.reshape(batch, seq, d)
```

```python
import functools
import math

import jax
import jax.numpy as jnp
from jax import lax
from jax.experimental import pallas as pl
from jax.experimental.pallas import tpu as pltpu

F32 = jnp.float32
BF16 = jnp.bfloat16

EPS = 1e-6
CHUNK = 64
MASKED = -1e30
LOG2E = math.log2(math.e)

A_NOPE, A_ROPE, A_V = 128, 64, 128
A_ROPE_THETA = 10000.0
B_DK, B_DV = 64, 128
B_ROT = B_DK // 4
B_ROPE_THETA = 500000.0
C_DH = 128
C_LEFT_CHUNKS = 8
C_REL_MAX = 128
N_BRANCH = 3

PROJ_HEAD_GROUP = 2
N_BUF = 3
MAIN_COLUMN_TILE = 2048
STEPS_PER_ITERATION = 12
LANES = 128
V7X_VMEM_BUDGET = 56 * 2**20


def _params(semantics, block_bytes, temp_bytes=0):
    need = 2 * block_bytes + temp_bytes + (4 << 20)
    return pltpu.CompilerParams(
        dimension_semantics=semantics,
        vmem_limit_bytes=int(min(max(need, 16 << 20), V7X_VMEM_BUDGET)),
    )


def _sigmoid(z):
    return 1.0 / (1.0 + jnp.exp(-z))


def _rms(x, g, n):
    ss = jnp.sum(x * x, axis=-1, keepdims=True) * (1.0 / n)
    return x * lax.rsqrt(ss + EPS) * g


def _rmsnorm_kernel(x_ref, g_ref, o_ref, ot_ref):
    x = x_ref[...]
    y = _rms(x, g_ref[...], x.shape[-1])
    o_ref[...] = y.astype(o_ref.dtype)
    ot_ref[...] = y.T.astype(ot_ref.dtype)


def _rmsnorm(x, g, tm=512):
    t, d = x.shape
    return pl.pallas_call(
        _rmsnorm_kernel,
        out_shape=(jax.ShapeDtypeStruct((t, d), BF16), jax.ShapeDtypeStruct((d, t), BF16)),
        grid=(t // tm,),
        in_specs=[pl.BlockSpec((tm, d), lambda i: (i, 0)), pl.BlockSpec((1, d), lambda i: (0, 0))],
        out_specs=(pl.BlockSpec((tm, d), lambda i: (i, 0)), pl.BlockSpec((d, tm), lambda i: (0, i))),
        compiler_params=_params(("parallel",), tm * d * 8, tm * d * 12),
        name="pre_rmsnorm",
    )(x, g)


def _mm_kernel(a_ref, b_ref, o_ref):
    o_ref[...] = jnp.dot(a_ref[...], b_ref[...], preferred_element_type=F32).astype(o_ref.dtype)


def _matmul(a, b, out_dtype, tm, tn, name):
    m, k = a.shape
    n = b.shape[1]
    blocks = tm * k * 2 + k * tn * 2 + tm * tn * jnp.dtype(out_dtype).itemsize
    return pl.pallas_call(
        _mm_kernel,
        out_shape=jax.ShapeDtypeStruct((m, n), out_dtype),
        grid=(m // tm, n // tn),
        in_specs=[pl.BlockSpec((tm, k), lambda i, j: (i, 0)), pl.BlockSpec((k, tn), lambda i, j: (0, j))],
        out_specs=pl.BlockSpec((tm, tn), lambda i, j: (i, j)),
        compiler_params=_params(("parallel", "parallel"), blocks, tm * tn * 4),
        name=name,
    )(a, b)


def _mm_res_kernel(a_ref, b_ref, x_ref, o_ref):
    o_ref[...] = x_ref[...] + jnp.dot(a_ref[...], b_ref[...], preferred_element_type=F32)


def _matmul_residual(a, b, x, tm, tn):
    m, k = a.shape
    n = b.shape[1]
    blocks = tm * k * 2 + k * tn * 2 + 2 * tm * tn * 4
    return pl.pallas_call(
        _mm_res_kernel,
        out_shape=jax.ShapeDtypeStruct((m, n), F32),
        grid=(m // tm, n // tn),
        in_specs=[
            pl.BlockSpec((tm, k), lambda i, j: (i, 0)),
            pl.BlockSpec((k, tn), lambda i, j: (0, j)),
            pl.BlockSpec((tm, tn), lambda i, j: (i, j)),
        ],
        out_specs=pl.BlockSpec((tm, tn), lambda i, j: (i, j)),
        compiler_params=_params(("parallel", "parallel"), blocks, tm * tn * 4),
        name="out_proj_residual",
    )(a, b, x)


def _weights_kernel(wt_ref, main_ref, tb_ref, tc_ref, kr_ref, *, q_rank, kv_rank, bw, n_gate):
    s_kr = q_rank + kv_rank
    s_az = s_kr + A_ROPE

    def piece(n):
        return wt_ref[0, s_az + n * bw:s_az + (n + 1) * bw, :]

    for slot, n in enumerate((0, 2, 4, 6, 8)):
        main_ref[:, slot * bw:(slot + 1) * bw] = piece(n).T.astype(BF16)
    o_g = 5 * bw
    for g in range(n_gate // bw):
        main_ref[:, o_g + g * bw:o_g + (g + 1) * bw] = piece(9 + g).T.astype(BF16)
    main_ref[:, o_g + n_gate:o_g + n_gate + s_kr] = wt_ref[0, :s_kr, :].T.astype(BF16)
    n_pad = main_ref.shape[1] - (o_g + n_gate + s_kr)
    main_ref[:, o_g + n_gate + s_kr:] = jnp.zeros((main_ref.shape[0], n_pad), BF16)
    kr_ref[:, :A_ROPE] = wt_ref[0, s_kr:s_az, :].T.astype(BF16)
    kr_ref[:, A_ROPE:] = jnp.zeros((kr_ref.shape[0], LANES - A_ROPE), BF16)
    tb_ref[:bw, :] = piece(1).astype(BF16)
    tb_ref[bw:, :] = piece(3).astype(BF16)
    tc_ref[:bw, :] = piece(5).astype(BF16)
    tc_ref[bw:, :] = piece(7).astype(BF16)


def _weights(w_in_t, layer, q_rank, kv_rank, bw, n_gate, n_main, cols=128):
    _, d_in, d = w_in_t.shape
    blocks = cols * d_in * 4 + cols * (n_main + LANES + 4 * bw) * 2
    return pl.pallas_call(
        functools.partial(_weights_kernel, q_rank=q_rank, kv_rank=kv_rank, bw=bw, n_gate=n_gate),
        out_shape=(jax.ShapeDtypeStruct((d, n_main), BF16),
                   jax.ShapeDtypeStruct((2 * bw, d), BF16),
                   jax.ShapeDtypeStruct((2 * bw, d), BF16),
                   jax.ShapeDtypeStruct((d, LANES), BF16)),
        grid=(d // cols,),
        in_specs=[pl.BlockSpec((1, d_in, cols), lambda r: (layer, 0, r))],
        out_specs=(pl.BlockSpec((cols, n_main), lambda r: (r, 0)),
                   pl.BlockSpec((2 * bw, cols), lambda r: (0, r)),
                   pl.BlockSpec((2 * bw, cols), lambda r: (0, r)),
                   pl.BlockSpec((cols, LANES), lambda r: (r, 0))),
        compiler_params=_params(("parallel",), blocks, 4 * cols * bw * 4),
        name="weights_prep",
    )(w_in_t)


def _merge_kernel(oa_ref, ob_ref, oc_ref, w_ref, ga_ref, gb_ref, gc_ref, y_ref):
    y = None
    for n, (o_ref, g_ref) in enumerate(((oa_ref, ga_ref), (ob_ref, gb_ref), (oc_ref, gc_ref))):
        gate = _sigmoid(g_ref[...].astype(F32))
        term = gate * jnp.dot(o_ref[...], w_ref[n], preferred_element_type=F32)
        y = term if y is None else y + term
    y_ref[...] = y.astype(y_ref.dtype)


def _merge(o_a, o_b, o_c, w_branch, u, gate_off, tm, tn):
    t, bw = o_a.shape
    d = w_branch.shape[2]
    nj = d // tn
    g0 = gate_off // tn
    branch_spec = pl.BlockSpec((tm, bw), lambda i, j: (i, 0))
    gate_specs = [pl.BlockSpec((tm, tn), functools.partial(lambda i, j, n: (i, g0 + n * nj + j), n=n))
                  for n in range(N_BRANCH)]
    blocks = 3 * tm * bw * 2 + N_BRANCH * bw * tn * 2 + 3 * tm * tn * 2 + tm * tn * 2
    return pl.pallas_call(
        _merge_kernel,
        out_shape=jax.ShapeDtypeStruct((t, d), BF16),
        grid=(t // tm, nj),
        in_specs=[branch_spec, branch_spec, branch_spec,
                  pl.BlockSpec((N_BRANCH, bw, tn), lambda i, j: (0, 0, j))] + gate_specs,
        out_specs=pl.BlockSpec((tm, tn), lambda i, j: (i, j)),
        compiler_params=_params(("parallel", "parallel"), blocks, 3 * tm * tn * 4),
        name="gated_merge",
    )(o_a, o_b, o_c, w_branch, u, u, u)


def _chunk_indicator(pos0, rows, first_lane):
    chunk = (pos0 + lax.broadcasted_iota(jnp.int32, (rows, LANES), 0)) // CHUNK
    lane = lax.broadcasted_iota(jnp.int32, (rows, LANES), 1)
    return jnp.where(lane - first_lane == chunk, 1.0, 0.0)


def _chunk_mask_rows(pos0, cols, first_row):
    chunk = (pos0 + lax.broadcasted_iota(jnp.int32, (LANES, cols), 1)) // CHUNK
    row = lax.broadcasted_iota(jnp.int32, (LANES, cols), 0)
    return jnp.where((row >= first_row) & (row - first_row > chunk), MASKED, 0.0)


def _flash_loop(qt_ref, k_ref, vt_ref, s_refs, acc_ref, l_ref, *, tk, ratio):
    n_q, _, cols = acc_ref.shape

    def scores(qi, j):
        k = k_ref[pl.ds(pl.multiple_of(j * tk, tk), tk), :]
        return jnp.dot(k, qt_ref[0, 0, qi], preferred_element_type=F32)

    def step(cur, carry):
        nxt = (cur + 1) % N_BUF
        qi, j, m, l = carry
        last = j == qi // ratio
        qi_next = jnp.where(last, jnp.minimum(qi + 1, n_q - 1), qi)
        j_next = jnp.where(last, 0, j + 1)
        s_refs[nxt][...] = scores(qi_next, j_next)
        s = s_refs[cur][...]
        m_new = jnp.maximum(m, jnp.max(s, axis=0, keepdims=True))
        alpha = jnp.exp2(m - m_new)
        p = jnp.exp2(s - m_new)
        pb = p.astype(BF16)
        l_new = alpha * l + jnp.sum(p, axis=0, keepdims=True)
        half_cols = cols // 2
        for c in range(2):
            cs = slice(c * half_cols, (c + 1) * half_cols)
            acc = alpha[:, cs] * acc_ref[qi, :, cs] + jnp.dot(vt_ref[0, 0, j], pb[:, cs], preferred_element_type=F32)
            acc_ref[qi, :, cs] = acc
        l_ref[qi] = l_new
        return qi_next, j_next, jnp.where(last, MASKED, m_new), l_new

    def steps(count, carry):
        for i in range(count):
            carry = step(i % N_BUF, carry)
        return carry

    zero = jnp.int32(0)
    s_refs[0][...] = scores(zero, zero)
    row = jnp.zeros((1, cols), F32)
    acc_ref[...] = jnp.zeros(acc_ref.shape, F32)
    carry = (zero, zero, row + MASKED, row)
    n_steps = sum(qi // ratio + 1 for qi in range(n_q))
    carry = lax.fori_loop(0, n_steps // STEPS_PER_ITERATION, lambda _, c: steps(STEPS_PER_ITERATION, c), carry)
    steps(n_steps % STEPS_PER_ITERATION, carry)


def _rotate_rows(y, half, rope_t):
    cos, sin_lo, sin_hi = rope_t[:LANES], rope_t[LANES:2 * LANES], rope_t[2 * LANES:]
    up = jnp.concatenate([y[half:], y[:half]], axis=0)
    down = jnp.concatenate([y[-half:], y[:-half]], axis=0)
    return y * cos + up * sin_lo + down * sin_hi


def _silu_gate_store(o_ref, z_ref, rows, y):
    z = z_ref[rows, :].astype(F32)
    o_ref[rows, :] = (y * (z * _sigmoid(z))).astype(o_ref.dtype)


def _prep_a_kernel(cq_ref, ckv_ref, h_ref, wkr_ref, gcq_ref, gckv_ref, wuqt_ref, wk_ref, wvt_ref, gq_ref, gk_ref,
                   rope_ref, ropet_ref, qt_ref, k_ref, vt_ref, *, heads, scale, n_seq):
    rows = cq_ref.shape[0]
    pos0 = (pl.program_id(0) % n_seq) * rows
    rope = rope_ref[...]
    rope_t = ropet_ref[...]
    cos, sin_lo, sin_hi = rope[:, :LANES], rope[:, LANES:2 * LANES], rope[:, 2 * LANES:]

    def rotate(y):
        return y * cos + pltpu.roll(y, LANES - A_ROPE // 2, 1) * sin_lo + pltpu.roll(y, A_ROPE // 2, 1) * sin_hi

    cq = cq_ref[...].astype(F32)
    cq = _rms(cq, gcq_ref[...], cq.shape[-1])
    ckv = ckv_ref[...].astype(F32)
    ckv = _rms(ckv, gckv_ref[...], ckv.shape[-1])
    qat = jnp.dot(wuqt_ref[...], cq.T.astype(BF16), preferred_element_type=F32)
    vat = jnp.dot(wvt_ref[...], ckv.T.astype(BF16), preferred_element_type=F32)
    kna = jnp.dot(ckv.astype(BF16), wk_ref[...], preferred_element_type=F32)
    gq = gq_ref[...]
    gk = gk_ref[...]
    kr = jnp.dot(h_ref[...], wkr_ref[...], preferred_element_type=F32)
    k_rope = (rotate(_rms(kr, gk[:, LANES:], A_ROPE)) + _chunk_indicator(pos0, rows, A_ROPE)).astype(BF16)
    mask_rows = _chunk_mask_rows(pos0, rows, A_ROPE)
    hw = 2 * LANES
    for h in range(heads):
        xn = qat[h * hw:h * hw + LANES]
        q_nope = xn * lax.rsqrt(jnp.sum(xn * xn, axis=0, keepdims=True) * (1.0 / A_NOPE) + EPS) * gq[:LANES]
        xr = qat[h * hw + LANES:(h + 1) * hw]
        q_rope = xr * lax.rsqrt(jnp.sum(xr * xr, axis=0, keepdims=True) * (1.0 / A_ROPE) + EPS) * gq[LANES:]
        q_rope = _rotate_rows(q_rope, A_ROPE // 2, rope_t)
        qt_ref[0, h, 0, :LANES, :] = (q_nope * scale).astype(BF16)
        qt_ref[0, h, 0, LANES:, :] = (q_rope * scale + mask_rows).astype(BF16)
        k_nope = _rms(kna[:, h * LANES:(h + 1) * LANES], gk[:, :LANES], A_NOPE)
        k_ref[:, h * hw:h * hw + LANES] = k_nope.astype(BF16)
        k_ref[:, h * hw + LANES:(h + 1) * hw] = k_rope
        vt_ref[0, h, 0] = vat[h * LANES:(h + 1) * LANES].astype(BF16)


def _prep_a(u, h, w_kr, off_cq, off_ckv, g_cq, g_ckv, w_uqt, w_k, w_vt, g_q, g_k, rope, rope_t, batch, seq, heads,
            t_rows):
    t = u.shape[0]
    q_rank, kv_rank = w_uqt.shape[1], w_k.shape[0]
    hw = 2 * LANES
    n_seq = seq // t_rows
    assert seq // CHUNK <= LANES - A_ROPE
    scale = (A_NOPE + A_ROPE) ** -0.5 * LOG2E
    row = lambda i: (i, 0)
    const = lambda i: (0, 0)
    tile = lambda i: (i // n_seq, 0, i % n_seq, 0, 0)
    d = h.shape[1]
    blocks = (t_rows * (q_rank + kv_rank + d) * 2 + d * LANES * 2 + (q_rank * hw + 2 * kv_rank * LANES) * heads * 2
              + 2 * t_rows * 3 * LANES * 4 + t_rows * heads * (2 * hw + LANES) * 2)
    return pl.pallas_call(
        functools.partial(_prep_a_kernel, heads=heads, scale=scale, n_seq=n_seq),
        out_shape=(jax.ShapeDtypeStruct((batch, heads, n_seq, hw, t_rows), BF16),
                   jax.ShapeDtypeStruct((t, heads * hw), BF16),
                   jax.ShapeDtypeStruct((batch, heads, n_seq, A_V, t_rows), BF16)),
        grid=(t // t_rows,),
        in_specs=[
            pl.BlockSpec((t_rows, q_rank), lambda i: (i, off_cq // q_rank)),
            pl.BlockSpec((t_rows, kv_rank), lambda i: (i, off_ckv // kv_rank)),
            pl.BlockSpec((t_rows, d), row),
            pl.BlockSpec((d, LANES), const),
            pl.BlockSpec((1, q_rank), const),
            pl.BlockSpec((1, kv_rank), const),
            pl.BlockSpec((heads * hw, q_rank), const),
            pl.BlockSpec((kv_rank, heads * LANES), const),
            pl.BlockSpec((heads * LANES, kv_rank), const),
            pl.BlockSpec((hw, 1), const),
            pl.BlockSpec((1, hw), const),
            pl.BlockSpec((t_rows, 3 * LANES), lambda i: (i % n_seq, 0)),
            pl.BlockSpec((3 * LANES, t_rows), lambda i: (0, i % n_seq)),
        ],
        out_specs=(pl.BlockSpec((1, heads, 1, hw, t_rows), tile),
                   pl.BlockSpec((t_rows, heads * hw), row),
                   pl.BlockSpec((1, heads, 1, A_V, t_rows), tile)),
        compiler_params=_params(("parallel",), blocks, 4 * t_rows * heads * hw * 4),
        name="mla_prep",
    )(u, u, h, w_kr, g_cq, g_ckv, w_uqt, w_k, w_vt, g_q, g_k, rope, rope_t)


def _attn_a_kernel(qt_ref, k_ref, vt_ref, z_ref, o_ref, *scratch, t):
    s_refs, (acc_ref, l_ref) = scratch[:N_BUF], scratch[N_BUF:]
    _flash_loop(qt_ref, k_ref, vt_ref, s_refs, acc_ref, l_ref, tk=t, ratio=1)
    for qi in range(acc_ref.shape[0]):
        _silu_gate_store(o_ref, z_ref, slice(qi * t, (qi + 1) * t), (acc_ref[qi] * (1.0 / l_ref[qi])).T)


def _attn_a(qt, k, vt, u, z_off, batch, seq, heads, t):
    n_tiles = seq // t
    dk = k.shape[1] // heads
    zb = z_off // LANES
    seq_head = lambda b, h: (b, h)
    tiles = lambda b, h: (b, h, 0, 0, 0)
    blocks = 2 * seq * dk * 2 + 3 * seq * LANES * 2
    scratch = N_BUF * t * t * 4 + n_tiles * A_V * t * 4
    return pl.pallas_call(
        functools.partial(_attn_a_kernel, t=t),
        out_shape=jax.ShapeDtypeStruct((batch * seq, heads * LANES), BF16),
        grid=(batch, heads),
        in_specs=[
            pl.BlockSpec((1, 1, n_tiles, dk, t), tiles),
            pl.BlockSpec((seq, dk), seq_head),
            pl.BlockSpec((1, 1, n_tiles, A_V, t), tiles),
            pl.BlockSpec((seq, LANES), lambda b, h: (b, zb + h)),
        ],
        out_specs=pl.BlockSpec((seq, LANES), seq_head),
        scratch_shapes=([pltpu.VMEM((t, t), F32)] * N_BUF
                        + [pltpu.VMEM((n_tiles, A_V, t), F32), pltpu.VMEM((n_tiles, 1, t), F32)]),
        compiler_params=_params(("parallel", "parallel"), blocks, scratch + 6 * t * t * 4),
        name="mla_attention",
    )(qt, k, vt, u)


def _proj_t_b_kernel(wt_ref, ht_ref, gq_ref, ropet_ref, qt_ref, vt_ref, *, heads, scale, n_seq, tq):
    cols = ht_ref.shape[1]
    pos0 = (pl.program_id(0) % n_seq) * cols
    ht = ht_ref[...]
    group = PROJ_HEAD_GROUP * LANES
    rope_t = ropet_ref[...]
    gq = gq_ref[...]
    first_map = lax.broadcasted_iota(jnp.int32, (LANES, 1), 0) < B_DK
    mask_rows = _chunk_mask_rows(pos0, cols, 0).astype(BF16)
    for h in range(heads):
        if h % PROJ_HEAD_GROUP == 0:
            res_q = jnp.dot(wt_ref[h * LANES:h * LANES + group], ht, preferred_element_type=F32)
            res_v = jnp.dot(wt_ref[(heads + h) * LANES:(heads + h) * LANES + group], ht,
                            preferred_element_type=F32)
        g = h % PROJ_HEAD_GROUP
        x = res_q[g * LANES:(g + 1) * LANES]
        xsq = x * x
        ss_lo = jnp.sum(xsq[:B_DK], axis=0, keepdims=True)
        ss_hi = jnp.sum(xsq[B_DK:], axis=0, keepdims=True)
        y = x * lax.rsqrt(jnp.where(first_map, ss_lo, ss_hi) * (1.0 / B_DK) + EPS) * gq
        q = _rotate_rows(y, B_ROT // 2, rope_t) * scale
        q_lo = jnp.where(first_map, q, 0.0).astype(BF16)
        q_hi = jnp.where(first_map, 0.0, q).astype(BF16)
        for a in range(cols // tq):
            cs = slice(a * tq, (a + 1) * tq)
            qt_ref[0, h, a, :LANES, :tq] = q_lo[:, cs]
            qt_ref[0, h, a, :LANES, tq:] = q_hi[:, cs]
            qt_ref[0, h, a, LANES:, :tq] = mask_rows[:, cs]
            qt_ref[0, h, a, LANES:, tq:] = mask_rows[:, cs]
        vt_ref[0, h, 0] = res_v[g * LANES:(g + 1) * LANES].astype(BF16)


def _proj_t_b(w_t, h_t, g_q, rope_t, batch, seq, heads, tq, tk):
    d, t = h_t.shape
    n_seq = seq // tk
    assert seq // CHUNK <= LANES
    tile = lambda i: (i // n_seq, 0, i % n_seq, 0, 0)
    blocks = w_t.size * 2 + d * tk * 2 + 3 * LANES * tk * 4 + heads * tk * (4 * LANES + B_DV) * 2
    return pl.pallas_call(
        functools.partial(_proj_t_b_kernel, heads=heads, scale=B_DK ** -0.5 * LOG2E, n_seq=n_seq, tq=tq),
        out_shape=(jax.ShapeDtypeStruct((batch, heads, seq // tq, 2 * LANES, 2 * tq), BF16),
                   jax.ShapeDtypeStruct((batch, heads, n_seq, B_DV, tk), BF16)),
        grid=(t // tk,),
        in_specs=[
            pl.BlockSpec(w_t.shape, lambda i: (0, 0)),
            pl.BlockSpec((d, tk), lambda i: (0, i)),
            pl.BlockSpec((LANES, 1), lambda i: (0, 0)),
            pl.BlockSpec((3 * LANES, tk), lambda i: (0, i % n_seq)),
        ],
        out_specs=(pl.BlockSpec((1, heads, tk // tq, 2 * LANES, 2 * tq), tile),
                   pl.BlockSpec((1, heads, 1, B_DV, tk), tile)),
        compiler_params=_params(("parallel",), blocks, 3 * w_t.shape[0] * tk * 4),
        name="diff_proj_qv",
    )(w_t, h_t, g_q, rope_t)


def _prep_b_kernel(k_ref, gk_ref, rope_ref, ko_ref, *, heads, n_seq):
    rows = k_ref.shape[0]
    pos0 = (pl.program_id(0) % n_seq) * rows
    rope = rope_ref[...]
    cos, sin_lo, sin_hi = rope[:, :LANES], rope[:, LANES:2 * LANES], rope[:, 2 * LANES:]
    first_map = lax.broadcasted_iota(jnp.int32, (1, LANES), 1) < B_DK
    gk = gk_ref[...]
    indicator = _chunk_indicator(pos0, rows, 0).astype(BF16)
    for h in range(heads):
        x = k_ref[:, h * LANES:(h + 1) * LANES].astype(F32)
        xsq = x * x
        ss_lo = jnp.sum(jnp.where(first_map, xsq, 0.0), axis=-1, keepdims=True)
        ss_hi = jnp.sum(jnp.where(first_map, 0.0, xsq), axis=-1, keepdims=True)
        y = x * lax.rsqrt(jnp.where(first_map, ss_lo, ss_hi) * (1.0 / B_DK) + EPS) * gk
        y = y * cos + pltpu.roll(y, LANES - B_ROT // 2, 1) * sin_lo + pltpu.roll(y, B_ROT // 2, 1) * sin_hi
        ko_ref[:, 2 * h * LANES:(2 * h + 1) * LANES] = y.astype(BF16)
        ko_ref[:, (2 * h + 1) * LANES:(2 * h + 2) * LANES] = indicator


def _prep_b(u, off_k, g_k, rope, seq, heads, t_rows):
    t = u.shape[0]
    w = heads * LANES
    n_seq = seq // t_rows
    return pl.pallas_call(
        functools.partial(_prep_b_kernel, heads=heads, n_seq=n_seq),
        out_shape=jax.ShapeDtypeStruct((t, 2 * w), BF16),
        grid=(t // t_rows,),
        in_specs=[
            pl.BlockSpec((t_rows, w), lambda i: (i, off_k // w)),
            pl.BlockSpec((1, LANES), lambda i: (0, 0)),
            pl.BlockSpec((t_rows, 3 * LANES), lambda i: (i % n_seq, 0)),
        ],
        out_specs=pl.BlockSpec((t_rows, 2 * w), lambda i: (i, 0)),
        compiler_params=_params(("parallel",), 3 * t_rows * w * 2 + t_rows * 3 * LANES * 4, 8 * t_rows * LANES * 4),
        name="diff_prep_k",
    )(u, g_k, rope)


def _attn_b_kernel(qt_ref, k_ref, vt_ref, z_ref, lam_ref, gsub_ref, o_ref, *scratch, tq, tk, lam_init):
    s_refs, (acc_ref, l_ref) = scratch[:N_BUF], scratch[N_BUF:]
    _flash_loop(qt_ref, k_ref, vt_ref, s_refs, acc_ref, l_ref, tk=tk, ratio=tk // tq)
    lf = lam_ref[...]
    lam = (jnp.exp(jnp.sum(lf[0:1] * lf[1:2], axis=-1, keepdims=True))
           - jnp.exp(jnp.sum(lf[2:3] * lf[3:4], axis=-1, keepdims=True)) + lam_init)
    gsub = gsub_ref[...]
    for qi in range(acc_ref.shape[0]):
        o = acc_ref[qi] * (1.0 / l_ref[qi])
        a = o[:, :tq] - lam * o[:, tq:]
        ss = jnp.sum(a * a, axis=0, keepdims=True) * (1.0 / B_DV)
        y = a * lax.rsqrt(ss + EPS) * gsub * (1.0 - lam_init)
        _silu_gate_store(o_ref, z_ref, slice(qi * tq, (qi + 1) * tq), y.T)


def _attn_b(qt, k, vt, u, z_off, lam, g_sub, lam_init, batch, seq, heads, tq, tk):
    n_q, n_k = seq // tq, seq // tk
    zb = z_off // LANES
    cols = 2 * tq
    seq_head = lambda b, h: (b, h)
    tiles = lambda b, h: (b, h, 0, 0, 0)
    blocks = 2 * seq * 2 * LANES * 2 + seq * 2 * LANES * 2 + 3 * seq * LANES * 2
    scratch = N_BUF * tk * cols * 4 + n_q * B_DV * cols * 4
    return pl.pallas_call(
        functools.partial(_attn_b_kernel, tq=tq, tk=tk, lam_init=lam_init),
        out_shape=jax.ShapeDtypeStruct((batch * seq, heads * LANES), BF16),
        grid=(batch, heads),
        in_specs=[
            pl.BlockSpec((1, 1, n_q, 2 * LANES, cols), tiles),
            pl.BlockSpec((seq, 2 * LANES), seq_head),
            pl.BlockSpec((1, 1, n_k, B_DV, tk), tiles),
            pl.BlockSpec((seq, LANES), lambda b, h: (b, zb + h)),
            pl.BlockSpec(lam.shape, lambda b, h: (0, 0)),
            pl.BlockSpec((B_DV, 1), lambda b, h: (0, 0)),
        ],
        out_specs=pl.BlockSpec((seq, LANES), seq_head),
        scratch_shapes=([pltpu.VMEM((tk, cols), F32)] * N_BUF
                        + [pltpu.VMEM((n_q, B_DV, cols), F32), pltpu.VMEM((n_q, 1, cols), F32)]),
        compiler_params=_params(("parallel", "parallel"), blocks, scratch + 6 * tk * cols * 4),
        name="diff_attention",
    )(qt, k, vt, u, lam, g_sub)


C_PAD_SLOT = 64


def _proj_t_c_kernel(wt_ref, ht_ref, gq_ref, qt_ref, vt_ref, *, heads, scale, tq):
    cols = ht_ref.shape[1]
    r = pl.program_id(1)
    is_pad = r == 0
    pos0 = jnp.maximum(r - 1, 0) * cols
    ht = ht_ref[...]
    group = PROJ_HEAD_GROUP * LANES
    gq = gq_ref[...]
    q_chunk = (pos0 + lax.broadcasted_iota(jnp.int32, (LANES, cols), 1)) // CHUNK
    slot = lax.broadcasted_iota(jnp.int32, (LANES, cols), 0)
    out_of_band = ((slot < C_PAD_SLOT) & ((slot > q_chunk) | (slot < q_chunk - C_LEFT_CHUNKS))) | (slot == C_PAD_SLOT)
    mask_rows = jnp.where(out_of_band, MASKED, 0.0).astype(BF16)
    for h in range(heads):
        if h % PROJ_HEAD_GROUP == 0:
            res_q = jnp.dot(wt_ref[h * LANES:h * LANES + group], ht, preferred_element_type=F32)
            res_v = jnp.dot(wt_ref[(heads + h) * LANES:(heads + h) * LANES + group], ht,
                            preferred_element_type=F32)
        g = h % PROJ_HEAD_GROUP
        x = res_q[g * LANES:(g + 1) * LANES]
        q = x * lax.rsqrt(jnp.sum(x * x, axis=0, keepdims=True) * (1.0 / C_DH) + EPS) * gq * scale
        q = q.astype(BF16)
        v = res_v[g * LANES:(g + 1) * LANES].astype(BF16)
        v = jnp.where(is_pad, jnp.zeros_like(v), v)
        for a in range(cols // tq):
            cs = slice(a * tq, (a + 1) * tq)
            qt_ref[0, h, a, :LANES, :] = q[:, cs]
            qt_ref[0, h, a, LANES:, :] = mask_rows[:, cs]
            vt_ref[0, h, a] = v[:, cs]


def _proj_t_c(w_t, h_t, g_q, batch, seq, heads, pad, tq):
    d, t = h_t.shape
    n_seq = seq // pad
    per = pad // tq
    assert seq // CHUNK <= C_PAD_SLOT
    blocks = w_t.size * 2 + d * pad * 2 + heads * pad * 3 * LANES * 2
    return pl.pallas_call(
        functools.partial(_proj_t_c_kernel, heads=heads, scale=C_DH ** -0.5 * LOG2E, tq=tq),
        out_shape=(jax.ShapeDtypeStruct((batch, heads, seq // tq, 2 * LANES, tq), BF16),
                   jax.ShapeDtypeStruct((batch, heads, (seq + pad) // tq, C_DH, tq), BF16)),
        grid=(batch, n_seq + 1),
        in_specs=[
            pl.BlockSpec(w_t.shape, lambda b, r: (0, 0)),
            pl.BlockSpec((d, pad), lambda b, r: (0, b * n_seq + jnp.maximum(r - 1, 0))),
            pl.BlockSpec((LANES, 1), lambda b, r: (0, 0)),
        ],
        out_specs=(pl.BlockSpec((1, heads, per, 2 * LANES, tq), lambda b, r: (b, 0, jnp.maximum(r - 1, 0), 0, 0)),
                   pl.BlockSpec((1, heads, per, C_DH, tq), lambda b, r: (b, 0, r, 0, 0))),
        compiler_params=_params(("parallel", "arbitrary"), blocks, 3 * w_t.shape[0] * pad * 4),
        name="band_proj_qv",
    )(w_t, h_t, g_q)


def _prep_c_kernel(k_ref, gk_ref, ko_ref, *, heads):
    rows = k_ref.shape[0]
    r = pl.program_id(1)
    is_pad = r == 0
    pos0 = jnp.maximum(r - 1, 0) * rows
    gk = gk_ref[...]
    lane = lax.broadcasted_iota(jnp.int32, (rows, LANES), 1)
    indicator = jnp.where(is_pad, jnp.where(lane == C_PAD_SLOT, 1.0, 0.0), _chunk_indicator(pos0, rows, 0))
    indicator = indicator.astype(BF16)
    for h in range(heads):
        kn = _rms(k_ref[:, h * LANES:(h + 1) * LANES].astype(F32), gk, C_DH).astype(BF16)
        ko_ref[:, 2 * h * LANES:(2 * h + 1) * LANES] = jnp.where(is_pad, jnp.zeros_like(kn), kn)
        ko_ref[:, (2 * h + 1) * LANES:(2 * h + 2) * LANES] = indicator


def _prep_c(u, off_k, g_k, batch, seq, heads, pad):
    w = heads * LANES
    n_seq = seq // pad
    return pl.pallas_call(
        functools.partial(_prep_c_kernel, heads=heads),
        out_shape=jax.ShapeDtypeStruct((batch * (seq + pad), 2 * w), BF16),
        grid=(batch, n_seq + 1),
        in_specs=[
            pl.BlockSpec((pad, w), lambda b, r: (b * n_seq + jnp.maximum(r - 1, 0), off_k // w)),
            pl.BlockSpec((1, LANES), lambda b, r: (0, 0)),
        ],
        out_specs=pl.BlockSpec((pad, 2 * w), lambda b, r: (b * (n_seq + 1) + r, 0)),
        compiler_params=_params(("parallel", "parallel"), 3 * pad * w * 2, 4 * pad * LANES * 4),
        name="band_prep_k",
    )(u, g_k)


def _attn_c_kernel(qt_ref, k_ref, vt_ref, z_ref, rel_ref, o_ref, *scratch, tq, tw):
    n_q = qt_ref.shape[2]
    s_refs, p_refs, (bias_ref,) = scratch[:N_BUF], scratch[N_BUF:2 * N_BUF], scratch[2 * N_BUF:]
    width = rel_ref.shape[-1]
    toeplitz = pltpu.roll(jnp.broadcast_to(rel_ref[0], (tq, width)), 0, 1, stride=1, stride_axis=0)
    bias_ref[...] = (toeplitz[:, :tw] * LOG2E).T

    def scores(i):
        return jnp.dot(k_ref[i * tq:i * tq + tw, :], qt_ref[0, 0, i], preferred_element_type=F32) + bias_ref[...]

    s_refs[0][...] = scores(0)
    for i in range(n_q + 1):
        cur, nxt, prv = i % N_BUF, (i + 1) % N_BUF, (i - 1) % N_BUF
        if i + 1 < n_q:
            s_refs[nxt][...] = scores(i + 1)
        if i < n_q:
            s = s_refs[cur][...]
            p = jnp.exp2(s - jnp.max(s, axis=0, keepdims=True))
            p_refs[cur][...] = p.astype(BF16)
            l = jnp.sum(p, axis=0, keepdims=True)
        if i > 0:
            acc = None
            for a in range(tw // tq):
                part = jnp.dot(vt_ref[0, 0, i - 1 + a], p_refs[prv][a * tq:(a + 1) * tq, :],
                               preferred_element_type=F32)
                acc = part if acc is None else acc + part
            _silu_gate_store(o_ref, z_ref, slice((i - 1) * tq, i * tq), (acc * (1.0 / l_prev)).T)
        l_prev = l


def _attn_c(qt, k, vt, u, z_off, rel_rows, batch, seq, heads, tq, pad):
    n_q = seq // tq
    tw = tq + pad
    zb = z_off // LANES
    tiles = lambda b, h: (b, h, 0, 0, 0)
    blocks = seq * 2 * LANES * 2 + (seq + pad) * 3 * LANES * 2 + 2 * seq * LANES * 2
    return pl.pallas_call(
        functools.partial(_attn_c_kernel, tq=tq, tw=tw),
        out_shape=jax.ShapeDtypeStruct((batch * seq, heads * LANES), BF16),
        grid=(batch, heads),
        in_specs=[
            pl.BlockSpec((1, 1, n_q, 2 * LANES, tq), tiles),
            pl.BlockSpec((seq + pad, 2 * LANES), lambda b, h: (b, h)),
            pl.BlockSpec((1, 1, (seq + pad) // tq, C_DH, tq), tiles),
            pl.BlockSpec((seq, LANES), lambda b, h: (b, zb + h)),
            pl.BlockSpec((1, 1, rel_rows.shape[-1]), lambda b, h: (h, 0, 0)),
        ],
        out_specs=pl.BlockSpec((seq, LANES), lambda b, h: (b, h)),
        scratch_shapes=([pltpu.VMEM((tw, tq), F32)] * N_BUF + [pltpu.VMEM((tw, tq), BF16)] * N_BUF
                        + [pltpu.VMEM((tw, tq), F32)]),
        compiler_params=_params(("parallel", "parallel"), blocks, 12 * tw * tq * 4),
        name="band_attention",
    )(qt, k, vt, u, rel_rows)


def _rope_table(seq, dim, theta, group):
    half = dim // 2
    inv = 1.0 / (jnp.float32(theta) ** (jnp.arange(0, dim, 2, dtype=F32) / dim))
    ang = jnp.arange(seq, dtype=F32)[:, None] * inv[None, :]
    cos, sin = jnp.cos(ang), jnp.sin(ang)
    lane = jnp.arange(LANES) % group
    idx = lane % half
    in_lo = (lane < half)[None, :]
    in_hi = ((lane >= half) & (lane < dim))[None, :]
    c = jnp.where(in_lo | in_hi, cos[:, idx], 1.0 if group < LANES else 0.0)
    s_lo = jnp.where(in_lo, -sin[:, idx], 0.0)
    s_hi = jnp.where(in_hi, sin[:, idx], 0.0)
    return jnp.concatenate([c, s_lo, s_hi], axis=1)


def _band_rel_rows(rel_bias, tq, pad):
    width = pl.next_power_of_2(2 * tq + pad)
    e = jnp.arange(width)
    e = jnp.where(e < tq + pad, e, e - width)
    rel = jnp.clip(pad - e, -(CHUNK - 1), C_REL_MAX) + (CHUNK - 1)
    return rel_bias.astype(F32)[:, None, rel]


def _row(v):
    return v.astype(F32).reshape(1, -1)


def _pad_lanes(v, width):
    return jnp.pad(v, ((0, 0),) * (v.ndim - 1) + ((0, width - v.shape[-1]),))


def _layer(x2, layer_idx, batch, seq, rope_a, rope_b, g_pre, w_main, w_t_b, w_t_c, w_kr, a_g_cq, a_g_ckv, a_w_uq,
           a_w_ukv, a_g_q, a_g_k, b_g_q, b_g_k, b_lam, b_g_sub, c_g_q, c_g_k, c_rel_bias, w_branch, w_out):
    d = x2.shape[1]
    bw = w_branch.shape[1]
    q_rank, kv_rank = a_w_uq.shape[0], a_w_ukv.shape[0]
    a_heads, b_heads, c_heads = bw // A_V, bw // B_DV, bw // C_DH

    o_az, o_bk, o_bz, o_ck, o_cz, o_g = (n * bw for n in range(6))
    o_acq = o_g + N_BRANCH * d
    o_ackv = o_acq + q_rank

    h, h_t = _rmsnorm(x2, _row(g_pre))
    u = _matmul(h, w_main, BF16, 1024, MAIN_COLUMN_TILE, "in_proj")

    hw = 2 * LANES
    t_a = min(512, seq)
    w_uq = _pad_lanes(a_w_uq.reshape(q_rank, a_heads, A_NOPE + A_ROPE), hw).reshape(q_rank, a_heads * hw)
    w_ukv = a_w_ukv.reshape(kv_rank, a_heads, A_NOPE + A_V)
    w_k = w_ukv[:, :, :A_NOPE].reshape(kv_rank, a_heads * A_NOPE).astype(BF16)
    w_vt = w_ukv[:, :, A_NOPE:].reshape(kv_rank, a_heads * A_V).T.astype(BF16)
    g_q = jnp.concatenate([_row(a_g_q[:A_NOPE]), _pad_lanes(_row(a_g_q[A_NOPE:]), LANES)], axis=1)
    g_k = jnp.concatenate([_row(a_g_k[:A_NOPE]), _pad_lanes(_row(a_g_k[A_NOPE:]), LANES)], axis=1)
    qa, ka, vta = _prep_a(u, h, w_kr, o_acq, o_ackv, _row(a_g_cq), _row(a_g_ckv), w_uq.T.astype(BF16), w_k, w_vt,
                          g_q.T, g_k, rope_a, rope_a.T, batch, seq, a_heads, t_a)
    o_a = _attn_a(qa, ka, vta, u, o_az, batch, seq, a_heads, t_a)

    tq_b, tk_b = min(256, seq), min(512, seq)
    qb, vtb = _proj_t_b(w_t_b, h_t, jnp.tile(_row(b_g_q), (1, 2)).T, rope_b.T, batch, seq, b_heads, tq_b, tk_b)
    kb = _prep_b(u, o_bk, jnp.tile(_row(b_g_k), (1, 2)), rope_b, seq, b_heads, tk_b)
    lam_init = 0.8 - 0.6 * math.exp(-0.3 * layer_idx)
    o_b = _attn_b(qb, kb, vtb, u, o_bz, b_lam.astype(F32), b_g_sub.astype(F32).reshape(B_DV, 1), lam_init,
                  batch, seq, b_heads, tq_b, tk_b)

    pad = C_LEFT_CHUNKS * CHUNK
    tq_c = min(256, seq)
    qc, vtc = _proj_t_c(w_t_c, h_t, _row(c_g_q).T, batch, seq, c_heads, pad, tq_c)
    kc = _prep_c(u, o_ck, _row(c_g_k), batch, seq, c_heads, pad)
    o_c = _attn_c(qc, kc, vtc, u, o_cz, _band_rel_rows(c_rel_bias, tq_c, pad), batch, seq, c_heads, tq_c, pad)

    y = _merge(o_a, o_b, o_c, w_branch.astype(BF16), u, o_g, 1024, 1024)
    return _matmul_residual(y, w_out.astype(BF16), x2, 1024, 1024)


def kernel(x, g_pre, w_in, a_g_cq, a_g_ckv, a_w_uq, a_w_ukv, a_g_q, a_g_k, b_g_q, b_g_k, b_lam, b_g_sub,
           c_g_q, c_g_k, c_rel_bias, w_branch, w_out):
    batch, seq, d = x.shape
    rope_a = _rope_table(seq, A_ROPE, A_ROPE_THETA, LANES)
    rope_b = _rope_table(seq, B_ROT, B_ROPE_THETA, B_DK)
    x2 = x.reshape(batch * seq, d)
    q_rank, kv_rank, bw = a_w_uq.shape[1], a_w_ukv.shape[1], w_branch.shape[2]
    n_used = 5 * bw + N_BRANCH * d + q_rank + kv_rank
    n_main = -(-n_used // MAIN_COLUMN_TILE) * MAIN_COLUMN_TILE
    w_in_t = jnp.swapaxes(w_in, 1, 2)
    for l in range(g_pre.shape[0]):
        w_main, w_t_b, w_t_c, w_kr = _weights(w_in_t, l, q_rank, kv_rank, bw, N_BRANCH * d, n_main)
        x2 = _layer(x2, l, batch, seq, rope_a, rope_b, g_pre[l], w_main, w_t_b, w_t_c, w_kr, a_g_cq[l],
                    a_g_ckv[l], a_w_uq[l], a_w_ukv[l], a_g_q[l], a_g_k[l], b_g_q[l], b_g_k[l], b_lam[l],
                    b_g_sub[l], c_g_q[l], c_g_k[l], c_rel_bias[l], w_branch[l], w_out[l])
    return x2.reshape(batch, seq, d)
```

```python
import functools
import math

import jax
import jax.numpy as jnp
from jax import lax
from jax.experimental import pallas as pl
from jax.experimental.pallas import tpu as pltpu

F32 = jnp.float32
BF16 = jnp.bfloat16

EPS = 1e-6
CHUNK = 64
MASKED = -1e30
LOG2E = math.log2(math.e)

A_NOPE, A_ROPE, A_V = 128, 64, 128
A_ROPE_THETA = 10000.0
B_DK, B_DV = 64, 128
B_ROT = B_DK // 4
B_ROPE_THETA = 500000.0
C_DH = 128
C_LEFT_CHUNKS = 8
C_REL_MAX = 128
N_BRANCH = 3

PROJ_HEAD_GROUP = 2
N_BUF = 3
MAIN_COLUMN_TILE = 2048
STEPS_PER_ITERATION = 24
LANES = 128
V7X_VMEM_BUDGET = 56 * 2**20


def _params(semantics, block_bytes, temp_bytes=0):
    need = 2 * block_bytes + temp_bytes + (4 << 20)
    return pltpu.CompilerParams(
        dimension_semantics=semantics,
        vmem_limit_bytes=int(min(max(need, 16 << 20), V7X_VMEM_BUDGET)),
    )


def _sigmoid(z):
    return 1.0 / (1.0 + jnp.exp(-z))


def _rms(x, g, n):
    ss = jnp.sum(x * x, axis=-1, keepdims=True) * (1.0 / n)
    return x * lax.rsqrt(ss + EPS) * g


def _rmsnorm_kernel(x_ref, g_ref, o_ref, ot_ref):
    x = x_ref[...]
    y = _rms(x, g_ref[...], x.shape[-1])
    o_ref[...] = y.astype(o_ref.dtype)
    ot_ref[...] = y.T.astype(ot_ref.dtype)


def _rmsnorm(x, g, tm=512):
    t, d = x.shape
    return pl.pallas_call(
        _rmsnorm_kernel,
        out_shape=(jax.ShapeDtypeStruct((t, d), BF16), jax.ShapeDtypeStruct((d, t), BF16)),
        grid=(t // tm,),
        in_specs=[pl.BlockSpec((tm, d), lambda i: (i, 0)), pl.BlockSpec((1, d), lambda i: (0, 0))],
        out_specs=(pl.BlockSpec((tm, d), lambda i: (i, 0)), pl.BlockSpec((d, tm), lambda i: (0, i))),
        compiler_params=_params(("parallel",), tm * d * 8, tm * d * 12),
        name="pre_rmsnorm",
    )(x, g)


def _mm_kernel(a_ref, b_ref, o_ref):
    o_ref[...] = jnp.dot(a_ref[...], b_ref[...], preferred_element_type=F32).astype(o_ref.dtype)


def _matmul(a, b, out_dtype, tm, tn, name):
    m, k = a.shape
    n = b.shape[1]
    blocks = tm * k * 2 + k * tn * 2 + tm * tn * jnp.dtype(out_dtype).itemsize
    return pl.pallas_call(
        _mm_kernel,
        out_shape=jax.ShapeDtypeStruct((m, n), out_dtype),
        grid=(m // tm, n // tn),
        in_specs=[pl.BlockSpec((tm, k), lambda i, j: (i, 0)), pl.BlockSpec((k, tn), lambda i, j: (0, j))],
        out_specs=pl.BlockSpec((tm, tn), lambda i, j: (i, j)),
        compiler_params=_params(("parallel", "parallel"), blocks, tm * tn * 4),
        name=name,
    )(a, b)


def _mm_res_kernel(a_ref, b_ref, x_ref, o_ref):
    o_ref[...] = x_ref[...] + jnp.dot(a_ref[...], b_ref[...], preferred_element_type=F32)


def _matmul_residual(a, b, x, tm, tn):
    m, k = a.shape
    n = b.shape[1]
    blocks = tm * k * 2 + k * tn * 2 + 2 * tm * tn * 4
    return pl.pallas_call(
        _mm_res_kernel,
        out_shape=jax.ShapeDtypeStruct((m, n), F32),
        grid=(m // tm, n // tn),
        in_specs=[
            pl.BlockSpec((tm, k), lambda i, j: (i, 0)),
            pl.BlockSpec((k, tn), lambda i, j: (0, j)),
            pl.BlockSpec((tm, tn), lambda i, j: (i, j)),
        ],
        out_specs=pl.BlockSpec((tm, tn), lambda i, j: (i, j)),
        compiler_params=_params(("parallel", "parallel"), blocks, tm * tn * 4),
        name="out_proj_residual",
    )(a, b, x)


def _weights_kernel(wt_ref, main_ref, tb_ref, tc_ref, kr_ref, *, q_rank, kv_rank, bw, n_gate):
    s_kr = q_rank + kv_rank
    s_az = s_kr + A_ROPE

    def piece(n):
        return wt_ref[0, s_az + n * bw:s_az + (n + 1) * bw, :]

    for slot, n in enumerate((0, 2, 4, 6, 8)):
        main_ref[:, slot * bw:(slot + 1) * bw] = piece(n).T.astype(BF16)
    o_g = 5 * bw
    for g in range(n_gate // bw):
        main_ref[:, o_g + g * bw:o_g + (g + 1) * bw] = piece(9 + g).T.astype(BF16)
    main_ref[:, o_g + n_gate:o_g + n_gate + s_kr] = wt_ref[0, :s_kr, :].T.astype(BF16)
    n_pad = main_ref.shape[1] - (o_g + n_gate + s_kr)
    main_ref[:, o_g + n_gate + s_kr:] = jnp.zeros((main_ref.shape[0], n_pad), BF16)
    kr_ref[:, :A_ROPE] = wt_ref[0, s_kr:s_az, :].T.astype(BF16)
    kr_ref[:, A_ROPE:] = jnp.zeros((kr_ref.shape[0], LANES - A_ROPE), BF16)
    tb_ref[:bw, :] = piece(1).astype(BF16)
    tb_ref[bw:, :] = piece(3).astype(BF16)
    tc_ref[:bw, :] = piece(5).astype(BF16)
    tc_ref[bw:, :] = piece(7).astype(BF16)


def _weights(w_in_t, layer, q_rank, kv_rank, bw, n_gate, n_main, cols=128):
    _, d_in, d = w_in_t.shape
    blocks = cols * d_in * 4 + cols * (n_main + LANES + 4 * bw) * 2
    return pl.pallas_call(
        functools.partial(_weights_kernel, q_rank=q_rank, kv_rank=kv_rank, bw=bw, n_gate=n_gate),
        out_shape=(jax.ShapeDtypeStruct((d, n_main), BF16),
                   jax.ShapeDtypeStruct((2 * bw, d), BF16),
                   jax.ShapeDtypeStruct((2 * bw, d), BF16),
                   jax.ShapeDtypeStruct((d, LANES), BF16)),
        grid=(d // cols,),
        in_specs=[pl.BlockSpec((1, d_in, cols), lambda r: (layer, 0, r))],
        out_specs=(pl.BlockSpec((cols, n_main), lambda r: (r, 0)),
                   pl.BlockSpec((2 * bw, cols), lambda r: (0, r)),
                   pl.BlockSpec((2 * bw, cols), lambda r: (0, r)),
                   pl.BlockSpec((cols, LANES), lambda r: (r, 0))),
        compiler_params=_params(("parallel",), blocks, 4 * cols * bw * 4),
        name="weights_prep",
    )(w_in_t)


def _merge_kernel(oa_ref, ob_ref, oc_ref, w_ref, ga_ref, gb_ref, gc_ref, y_ref):
    y = None
    for n, (o_ref, g_ref) in enumerate(((oa_ref, ga_ref), (ob_ref, gb_ref), (oc_ref, gc_ref))):
        gate = _sigmoid(g_ref[...].astype(F32))
        term = gate * jnp.dot(o_ref[...], w_ref[n], preferred_element_type=F32)
        y = term if y is None else y + term
    y_ref[...] = y.astype(y_ref.dtype)


def _merge(o_a, o_b, o_c, w_branch, u, gate_off, tm, tn):
    t, bw = o_a.shape
    d = w_branch.shape[2]
    nj = d // tn
    g0 = gate_off // tn
    branch_spec = pl.BlockSpec((tm, bw), lambda i, j: (i, 0))
    gate_specs = [pl.BlockSpec((tm, tn), functools.partial(lambda i, j, n: (i, g0 + n * nj + j), n=n))
                  for n in range(N_BRANCH)]
    blocks = 3 * tm * bw * 2 + N_BRANCH * bw * tn * 2 + 3 * tm * tn * 2 + tm * tn * 2
    return pl.pallas_call(
        _merge_kernel,
        out_shape=jax.ShapeDtypeStruct((t, d), BF16),
        grid=(t // tm, nj),
        in_specs=[branch_spec, branch_spec, branch_spec,
                  pl.BlockSpec((N_BRANCH, bw, tn), lambda i, j: (0, 0, j))] + gate_specs,
        out_specs=pl.BlockSpec((tm, tn), lambda i, j: (i, j)),
        compiler_params=_params(("parallel", "parallel"), blocks, 3 * tm * tn * 4),
        name="gated_merge",
    )(o_a, o_b, o_c, w_branch, u, u, u)


def _chunk_indicator(pos0, rows, first_lane):
    chunk = (pos0 + lax.broadcasted_iota(jnp.int32, (rows, LANES), 0)) // CHUNK
    lane = lax.broadcasted_iota(jnp.int32, (rows, LANES), 1)
    return jnp.where(lane - first_lane == chunk, 1.0, 0.0)


def _chunk_mask_rows(pos0, cols, first_row):
    chunk = (pos0 + lax.broadcasted_iota(jnp.int32, (LANES, cols), 1)) // CHUNK
    row = lax.broadcasted_iota(jnp.int32, (LANES, cols), 0)
    return jnp.where((row >= first_row) & (row - first_row > chunk), MASKED, 0.0)


def _flash_loop(qt_ref, k_ref, vt_ref, s_refs, p_refs, acc_ref, l_ref, *, tk, ratio):
    n_q, _, cols = acc_ref.shape

    def scores(qi, j):
        k = k_ref[pl.ds(pl.multiple_of(j * tk, tk), tk), :]
        return jnp.dot(k, qt_ref[0, 0, qi], preferred_element_type=F32)

    def step(cur, carry):
        nxt, prv = (cur + 1) % N_BUF, (cur - 1) % N_BUF
        qi, j, m, l, alpha_prev, qi_prev, j_prev = carry
        last = j == qi // ratio
        qi_next = jnp.where(last, jnp.minimum(qi + 1, n_q - 1), qi)
        j_next = jnp.where(last, 0, j + 1)
        s_refs[nxt][...] = scores(qi_next, j_next)
        s = s_refs[cur][...]
        m_new = jnp.maximum(m, jnp.max(s, axis=0, keepdims=True))
        alpha = jnp.exp2(m - m_new)
        p = jnp.exp2(s - m_new)
        p_refs[cur][...] = p.astype(BF16)
        l_new = alpha * l + jnp.sum(p, axis=0, keepdims=True)
        half_cols = cols // 2
        for c in range(2):
            cs = slice(c * half_cols, (c + 1) * half_cols)
            acc = alpha_prev[:, cs] * acc_ref[qi_prev, :, cs] + jnp.dot(
                vt_ref[0, 0, j_prev], p_refs[prv][:, cs], preferred_element_type=F32)
            acc_ref[qi_prev, :, cs] = acc
        l_ref[qi_prev] = l
        return qi_next, j_next, jnp.where(last, MASKED, m_new), l_new, alpha, qi, j

    def steps(count, carry):
        for i in range(count):
            carry = step(i % N_BUF, carry)
        return carry

    zero = jnp.int32(0)
    s_refs[0][...] = scores(zero, zero)
    p_refs[N_BUF - 1][...] = jnp.zeros(p_refs[0].shape, BF16)
    row = jnp.zeros((1, cols), F32)
    acc_ref[...] = jnp.zeros(acc_ref.shape, F32)
    carry = (zero, zero, row + MASKED, row, row, zero, zero)
    n_steps = sum(qi // ratio + 1 for qi in range(n_q)) + 1
    carry = lax.fori_loop(0, n_steps // STEPS_PER_ITERATION, lambda _, c: steps(STEPS_PER_ITERATION, c), carry)
    steps(n_steps % STEPS_PER_ITERATION, carry)


def _rotate_rows(y, half, rope_t):
    cos, sin_lo, sin_hi = rope_t[:LANES], rope_t[LANES:2 * LANES], rope_t[2 * LANES:]
    up = jnp.concatenate([y[half:], y[:half]], axis=0)
    down = jnp.concatenate([y[-half:], y[:-half]], axis=0)
    return y * cos + up * sin_lo + down * sin_hi


def _silu_gate_store(o_ref, z_ref, rows, y):
    z = z_ref[rows, :].astype(F32)
    o_ref[rows, :] = (y * (z * _sigmoid(z))).astype(o_ref.dtype)


def _prep_a_kernel(cq_ref, ckv_ref, h_ref, wkr_ref, gcq_ref, gckv_ref, wuqt_ref, wk_ref, wvt_ref, gq_ref, gk_ref,
                   rope_ref, ropet_ref, qt_ref, k_ref, vt_ref, *, heads, scale, n_seq):
    rows = cq_ref.shape[0]
    pos0 = (pl.program_id(0) % n_seq) * rows
    rope = rope_ref[...]
    rope_t = ropet_ref[...]
    cos, sin_lo, sin_hi = rope[:, :LANES], rope[:, LANES:2 * LANES], rope[:, 2 * LANES:]

    def rotate(y):
        return y * cos + pltpu.roll(y, LANES - A_ROPE // 2, 1) * sin_lo + pltpu.roll(y, A_ROPE // 2, 1) * sin_hi

    cq = cq_ref[...].astype(F32)
    cq = _rms(cq, gcq_ref[...], cq.shape[-1])
    ckv = ckv_ref[...].astype(F32)
    ckv = _rms(ckv, gckv_ref[...], ckv.shape[-1])
    qat = jnp.dot(wuqt_ref[...], cq.T.astype(BF16), preferred_element_type=F32)
    vat = jnp.dot(wvt_ref[...], ckv.T.astype(BF16), preferred_element_type=F32)
    kna = jnp.dot(ckv.astype(BF16), wk_ref[...], preferred_element_type=F32)
    gq = gq_ref[...]
    gk = gk_ref[...]
    kr = jnp.dot(h_ref[...], wkr_ref[...], preferred_element_type=F32)
    k_rope = (rotate(_rms(kr, gk[:, LANES:], A_ROPE)) + _chunk_indicator(pos0, rows, A_ROPE)).astype(BF16)
    mask_rows = _chunk_mask_rows(pos0, rows, A_ROPE)
    hw = 2 * LANES
    for h in range(heads):
        xn = qat[h * hw:h * hw + LANES]
        q_nope = xn * lax.rsqrt(jnp.sum(xn * xn, axis=0, keepdims=True) * (1.0 / A_NOPE) + EPS) * gq[:LANES]
        xr = qat[h * hw + LANES:(h + 1) * hw]
        q_rope = xr * lax.rsqrt(jnp.sum(xr * xr, axis=0, keepdims=True) * (1.0 / A_ROPE) + EPS) * gq[LANES:]
        q_rope = _rotate_rows(q_rope, A_ROPE // 2, rope_t)
        qt_ref[0, h, 0, :LANES, :] = (q_nope * scale).astype(BF16)
        qt_ref[0, h, 0, LANES:, :] = (q_rope * scale + mask_rows).astype(BF16)
        k_nope = _rms(kna[:, h * LANES:(h + 1) * LANES], gk[:, :LANES], A_NOPE)
        k_ref[:, h * hw:h * hw + LANES] = k_nope.astype(BF16)
        k_ref[:, h * hw + LANES:(h + 1) * hw] = k_rope
        vt_ref[0, h, 0] = vat[h * LANES:(h + 1) * LANES].astype(BF16)


def _prep_a(u, h, w_kr, off_cq, off_ckv, g_cq, g_ckv, w_uqt, w_k, w_vt, g_q, g_k, rope, rope_t, batch, seq, heads,
            t_rows):
    t = u.shape[0]
    q_rank, kv_rank = w_uqt.shape[1], w_k.shape[0]
    hw = 2 * LANES
    n_seq = seq // t_rows
    assert seq // CHUNK <= LANES - A_ROPE
    scale = (A_NOPE + A_ROPE) ** -0.5 * LOG2E
    row = lambda i: (i, 0)
    const = lambda i: (0, 0)
    tile = lambda i: (i // n_seq, 0, i % n_seq, 0, 0)
    d = h.shape[1]
    blocks = (t_rows * (q_rank + kv_rank + d) * 2 + d * LANES * 2 + (q_rank * hw + 2 * kv_rank * LANES) * heads * 2
              + 2 * t_rows * 3 * LANES * 4 + t_rows * heads * (2 * hw + LANES) * 2)
    return pl.pallas_call(
        functools.partial(_prep_a_kernel, heads=heads, scale=scale, n_seq=n_seq),
        out_shape=(jax.ShapeDtypeStruct((batch, heads, n_seq, hw, t_rows), BF16),
                   jax.ShapeDtypeStruct((t, heads * hw), BF16),
                   jax.ShapeDtypeStruct((batch, heads, n_seq, A_V, t_rows), BF16)),
        grid=(t // t_rows,),
        in_specs=[
            pl.BlockSpec((t_rows, q_rank), lambda i: (i, off_cq // q_rank)),
            pl.BlockSpec((t_rows, kv_rank), lambda i: (i, off_ckv // kv_rank)),
            pl.BlockSpec((t_rows, d), row),
            pl.BlockSpec((d, LANES), const),
            pl.BlockSpec((1, q_rank), const),
            pl.BlockSpec((1, kv_rank), const),
            pl.BlockSpec((heads * hw, q_rank), const),
            pl.BlockSpec((kv_rank, heads * LANES), const),
            pl.BlockSpec((heads * LANES, kv_rank), const),
            pl.BlockSpec((hw, 1), const),
            pl.BlockSpec((1, hw), const),
            pl.BlockSpec((t_rows, 3 * LANES), lambda i: (i % n_seq, 0)),
            pl.BlockSpec((3 * LANES, t_rows), lambda i: (0, i % n_seq)),
        ],
        out_specs=(pl.BlockSpec((1, heads, 1, hw, t_rows), tile),
                   pl.BlockSpec((t_rows, heads * hw), row),
                   pl.BlockSpec((1, heads, 1, A_V, t_rows), tile)),
        compiler_params=_params(("parallel",), blocks, 4 * t_rows * heads * hw * 4),
        name="mla_prep",
    )(u, u, h, w_kr, g_cq, g_ckv, w_uqt, w_k, w_vt, g_q, g_k, rope, rope_t)


def _attn_a_kernel(qt_ref, k_ref, vt_ref, z_ref, o_ref, *scratch, t):
    s_refs, p_refs, (acc_ref, l_ref) = scratch[:N_BUF], scratch[N_BUF:2 * N_BUF], scratch[2 * N_BUF:]
    _flash_loop(qt_ref, k_ref, vt_ref, s_refs, p_refs, acc_ref, l_ref, tk=t, ratio=1)
    for qi in range(acc_ref.shape[0]):
        _silu_gate_store(o_ref, z_ref, slice(qi * t, (qi + 1) * t), (acc_ref[qi] * (1.0 / l_ref[qi])).T)


def _attn_a(qt, k, vt, u, z_off, batch, seq, heads, t):
    n_tiles = seq // t
    dk = k.shape[1] // heads
    zb = z_off // LANES
    seq_head = lambda b, h: (b, h)
    tiles = lambda b, h: (b, h, 0, 0, 0)
    blocks = 2 * seq * dk * 2 + 3 * seq * LANES * 2
    scratch = N_BUF * t * t * 6 + n_tiles * A_V * t * 4
    return pl.pallas_call(
        functools.partial(_attn_a_kernel, t=t),
        out_shape=jax.ShapeDtypeStruct((batch * seq, heads * LANES), BF16),
        grid=(batch, heads),
        in_specs=[
            pl.BlockSpec((1, 1, n_tiles, dk, t), tiles),
            pl.BlockSpec((seq, dk), seq_head),
            pl.BlockSpec((1, 1, n_tiles, A_V, t), tiles),
            pl.BlockSpec((seq, LANES), lambda b, h: (b, zb + h)),
        ],
        out_specs=pl.BlockSpec((seq, LANES), seq_head),
        scratch_shapes=([pltpu.VMEM((t, t), F32)] * N_BUF + [pltpu.VMEM((t, t), BF16)] * N_BUF
                        + [pltpu.VMEM((n_tiles, A_V, t), F32), pltpu.VMEM((n_tiles, 1, t), F32)]),
        compiler_params=_params(("parallel", "parallel"), blocks, scratch + 6 * t * t * 4),
        name="mla_attention",
    )(qt, k, vt, u)


def _proj_t_b_kernel(wt_ref, ht_ref, gq_ref, ropet_ref, qt_ref, vt_ref, *, heads, scale, n_seq, tq):
    cols = ht_ref.shape[1]
    pos0 = (pl.program_id(0) % n_seq) * cols
    ht = ht_ref[...]
    group = PROJ_HEAD_GROUP * LANES
    rope_t = ropet_ref[...]
    gq = gq_ref[...]
    first_map = lax.broadcasted_iota(jnp.int32, (LANES, 1), 0) < B_DK
    mask_rows = _chunk_mask_rows(pos0, cols, 0).astype(BF16)
    for h in range(heads):
        if h % PROJ_HEAD_GROUP == 0:
            res_q = jnp.dot(wt_ref[h * LANES:h * LANES + group], ht, preferred_element_type=F32)
            res_v = jnp.dot(wt_ref[(heads + h) * LANES:(heads + h) * LANES + group], ht,
                            preferred_element_type=F32)
        g = h % PROJ_HEAD_GROUP
        x = res_q[g * LANES:(g + 1) * LANES]
        xsq = x * x
        ss_lo = jnp.sum(xsq[:B_DK], axis=0, keepdims=True)
        ss_hi = jnp.sum(xsq[B_DK:], axis=0, keepdims=True)
        y = x * lax.rsqrt(jnp.where(first_map, ss_lo, ss_hi) * (1.0 / B_DK) + EPS) * gq
        q = _rotate_rows(y, B_ROT // 2, rope_t) * scale
        q_lo = jnp.where(first_map, q, 0.0).astype(BF16)
        q_hi = jnp.where(first_map, 0.0, q).astype(BF16)
        for a in range(cols // tq):
            cs = slice(a * tq, (a + 1) * tq)
            qt_ref[0, h, a, :LANES, :tq] = q_lo[:, cs]
            qt_ref[0, h, a, :LANES, tq:] = q_hi[:, cs]
            qt_ref[0, h, a, LANES:, :tq] = mask_rows[:, cs]
            qt_ref[0, h, a, LANES:, tq:] = mask_rows[:, cs]
        vt_ref[0, h, 0] = res_v[g * LANES:(g + 1) * LANES].astype(BF16)


def _proj_t_b(w_t, h_t, g_q, rope_t, batch, seq, heads, tq, tk):
    d, t = h_t.shape
    n_seq = seq // tk
    assert seq // CHUNK <= LANES
    tile = lambda i: (i // n_seq, 0, i % n_seq, 0, 0)
    blocks = w_t.size * 2 + d * tk * 2 + 3 * LANES * tk * 4 + heads * tk * (4 * LANES + B_DV) * 2
    return pl.pallas_call(
        functools.partial(_proj_t_b_kernel, heads=heads, scale=B_DK ** -0.5 * LOG2E, n_seq=n_seq, tq=tq),
        out_shape=(jax.ShapeDtypeStruct((batch, heads, seq // tq, 2 * LANES, 2 * tq), BF16),
                   jax.ShapeDtypeStruct((batch, heads, n_seq, B_DV, tk), BF16)),
        grid=(t // tk,),
        in_specs=[
            pl.BlockSpec(w_t.shape, lambda i: (0, 0)),
            pl.BlockSpec((d, tk), lambda i: (0, i)),
            pl.BlockSpec((LANES, 1), lambda i: (0, 0)),
            pl.BlockSpec((3 * LANES, tk), lambda i: (0, i % n_seq)),
        ],
        out_specs=(pl.BlockSpec((1, heads, tk // tq, 2 * LANES, 2 * tq), tile),
                   pl.BlockSpec((1, heads, 1, B_DV, tk), tile)),
        compiler_params=_params(("parallel",), blocks, 3 * w_t.shape[0] * tk * 4),
        name="diff_proj_qv",
    )(w_t, h_t, g_q, rope_t)


def _prep_b_kernel(k_ref, gk_ref, rope_ref, ko_ref, *, heads, n_seq):
    rows = k_ref.shape[0]
    pos0 = (pl.program_id(0) % n_seq) * rows
    rope = rope_ref[...]
    cos, sin_lo, sin_hi = rope[:, :LANES], rope[:, LANES:2 * LANES], rope[:, 2 * LANES:]
    first_map = lax.broadcasted_iota(jnp.int32, (1, LANES), 1) < B_DK
    gk = gk_ref[...]
    indicator = _chunk_indicator(pos0, rows, 0).astype(BF16)
    for h in range(heads):
        x = k_ref[:, h * LANES:(h + 1) * LANES].astype(F32)
        xsq = x * x
        ss_lo = jnp.sum(jnp.where(first_map, xsq, 0.0), axis=-1, keepdims=True)
        ss_hi = jnp.sum(jnp.where(first_map, 0.0, xsq), axis=-1, keepdims=True)
        y = x * lax.rsqrt(jnp.where(first_map, ss_lo, ss_hi) * (1.0 / B_DK) + EPS) * gk
        y = y * cos + pltpu.roll(y, LANES - B_ROT // 2, 1) * sin_lo + pltpu.roll(y, B_ROT // 2, 1) * sin_hi
        ko_ref[:, 2 * h * LANES:(2 * h + 1) * LANES] = y.astype(BF16)
        ko_ref[:, (2 * h + 1) * LANES:(2 * h + 2) * LANES] = indicator


def _prep_b(u, off_k, g_k, rope, seq, heads, t_rows):
    t = u.shape[0]
    w = heads * LANES
    n_seq = seq // t_rows
    return pl.pallas_call(
        functools.partial(_prep_b_kernel, heads=heads, n_seq=n_seq),
        out_shape=jax.ShapeDtypeStruct((t, 2 * w), BF16),
        grid=(t // t_rows,),
        in_specs=[
            pl.BlockSpec((t_rows, w), lambda i: (i, off_k // w)),
            pl.BlockSpec((1, LANES), lambda i: (0, 0)),
            pl.BlockSpec((t_rows, 3 * LANES), lambda i: (i % n_seq, 0)),
        ],
        out_specs=pl.BlockSpec((t_rows, 2 * w), lambda i: (i, 0)),
        compiler_params=_params(("parallel",), 3 * t_rows * w * 2 + t_rows * 3 * LANES * 4, 8 * t_rows * LANES * 4),
        name="diff_prep_k",
    )(u, g_k, rope)


def _attn_b_kernel(qt_ref, k_ref, vt_ref, z_ref, lam_ref, gsub_ref, o_ref, *scratch, tq, tk, lam_init):
    s_refs, p_refs, (acc_ref, l_ref) = scratch[:N_BUF], scratch[N_BUF:2 * N_BUF], scratch[2 * N_BUF:]
    _flash_loop(qt_ref, k_ref, vt_ref, s_refs, p_refs, acc_ref, l_ref, tk=tk, ratio=tk // tq)
    lf = lam_ref[...]
    lam = (jnp.exp(jnp.sum(lf[0:1] * lf[1:2], axis=-1, keepdims=True))
           - jnp.exp(jnp.sum(lf[2:3] * lf[3:4], axis=-1, keepdims=True)) + lam_init)
    gsub = gsub_ref[...]
    for qi in range(acc_ref.shape[0]):
        o = acc_ref[qi] * (1.0 / l_ref[qi])
        a = o[:, :tq] - lam * o[:, tq:]
        ss = jnp.sum(a * a, axis=0, keepdims=True) * (1.0 / B_DV)
        y = a * lax.rsqrt(ss + EPS) * gsub * (1.0 - lam_init)
        _silu_gate_store(o_ref, z_ref, slice(qi * tq, (qi + 1) * tq), y.T)


def _attn_b(qt, k, vt, u, z_off, lam, g_sub, lam_init, batch, seq, heads, tq, tk):
    n_q, n_k = seq // tq, seq // tk
    zb = z_off // LANES
    cols = 2 * tq
    seq_head = lambda b, h: (b, h)
    tiles = lambda b, h: (b, h, 0, 0, 0)
    blocks = 2 * seq * 2 * LANES * 2 + seq * 2 * LANES * 2 + 3 * seq * LANES * 2
    scratch = N_BUF * tk * cols * 6 + n_q * B_DV * cols * 4
    return pl.pallas_call(
        functools.partial(_attn_b_kernel, tq=tq, tk=tk, lam_init=lam_init),
        out_shape=jax.ShapeDtypeStruct((batch * seq, heads * LANES), BF16),
        grid=(batch, heads),
        in_specs=[
            pl.BlockSpec((1, 1, n_q, 2 * LANES, cols), tiles),
            pl.BlockSpec((seq, 2 * LANES), seq_head),
            pl.BlockSpec((1, 1, n_k, B_DV, tk), tiles),
            pl.BlockSpec((seq, LANES), lambda b, h: (b, zb + h)),
            pl.BlockSpec(lam.shape, lambda b, h: (0, 0)),
            pl.BlockSpec((B_DV, 1), lambda b, h: (0, 0)),
        ],
        out_specs=pl.BlockSpec((seq, LANES), seq_head),
        scratch_shapes=([pltpu.VMEM((tk, cols), F32)] * N_BUF + [pltpu.VMEM((tk, cols), BF16)] * N_BUF
                        + [pltpu.VMEM((n_q, B_DV, cols), F32), pltpu.VMEM((n_q, 1, cols), F32)]),
        compiler_params=_params(("parallel", "parallel"), blocks, scratch + 6 * tk * cols * 4),
        name="diff_attention",
    )(qt, k, vt, u, lam, g_sub)


C_PAD_SLOT = 64


def _proj_t_c_kernel(wt_ref, ht_ref, gq_ref, qt_ref, vt_ref, *, heads, scale, tq):
    cols = ht_ref.shape[1]
    r = pl.program_id(1)
    is_pad = r == 0
    pos0 = jnp.maximum(r - 1, 0) * cols
    ht = ht_ref[...]
    group = PROJ_HEAD_GROUP * LANES
    gq = gq_ref[...]
    q_chunk = (pos0 + lax.broadcasted_iota(jnp.int32, (LANES, cols), 1)) // CHUNK
    slot = lax.broadcasted_iota(jnp.int32, (LANES, cols), 0)
    out_of_band = ((slot < C_PAD_SLOT) & ((slot > q_chunk) | (slot < q_chunk - C_LEFT_CHUNKS))) | (slot == C_PAD_SLOT)
    mask_rows = jnp.where(out_of_band, MASKED, 0.0).astype(BF16)
    for h in range(heads):
        if h % PROJ_HEAD_GROUP == 0:
            res_q = jnp.dot(wt_ref[h * LANES:h * LANES + group], ht, preferred_element_type=F32)
            res_v = jnp.dot(wt_ref[(heads + h) * LANES:(heads + h) * LANES + group], ht,
                            preferred_element_type=F32)
        g = h % PROJ_HEAD_GROUP
        x = res_q[g * LANES:(g + 1) * LANES]
        q = x * lax.rsqrt(jnp.sum(x * x, axis=0, keepdims=True) * (1.0 / C_DH) + EPS) * gq * scale
        q = q.astype(BF16)
        v = res_v[g * LANES:(g + 1) * LANES].astype(BF16)
        v = jnp.where(is_pad, jnp.zeros_like(v), v)
        for a in range(cols // tq):
            cs = slice(a * tq, (a + 1) * tq)
            qt_ref[0, h, a, :LANES, :] = q[:, cs]
            qt_ref[0, h, a, LANES:, :] = mask_rows[:, cs]
            vt_ref[0, h, a] = v[:, cs]


def _proj_t_c(w_t, h_t, g_q, batch, seq, heads, pad, tq):
    d, t = h_t.shape
    n_seq = seq // pad
    per = pad // tq
    assert seq // CHUNK <= C_PAD_SLOT
    blocks = w_t.size * 2 + d * pad * 2 + heads * pad * 3 * LANES * 2
    return pl.pallas_call(
        functools.partial(_proj_t_c_kernel, heads=heads, scale=C_DH ** -0.5 * LOG2E, tq=tq),
        out_shape=(jax.ShapeDtypeStruct((batch, heads, seq // tq, 2 * LANES, tq), BF16),
                   jax.ShapeDtypeStruct((batch, heads, (seq + pad) // tq, C_DH, tq), BF16)),
        grid=(batch, n_seq + 1),
        in_specs=[
            pl.BlockSpec(w_t.shape, lambda b, r: (0, 0)),
            pl.BlockSpec((d, pad), lambda b, r: (0, b * n_seq + jnp.maximum(r - 1, 0))),
            pl.BlockSpec((LANES, 1), lambda b, r: (0, 0)),
        ],
        out_specs=(pl.BlockSpec((1, heads, per, 2 * LANES, tq), lambda b, r: (b, 0, jnp.maximum(r - 1, 0), 0, 0)),
                   pl.BlockSpec((1, heads, per, C_DH, tq), lambda b, r: (b, 0, r, 0, 0))),
        compiler_params=_params(("parallel", "arbitrary"), blocks, 3 * w_t.shape[0] * pad * 4),
        name="band_proj_qv",
    )(w_t, h_t, g_q)


def _prep_c_kernel(k_ref, gk_ref, ko_ref, *, heads):
    rows = k_ref.shape[0]
    r = pl.program_id(1)
    is_pad = r == 0
    pos0 = jnp.maximum(r - 1, 0) * rows
    gk = gk_ref[...]
    lane = lax.broadcasted_iota(jnp.int32, (rows, LANES), 1)
    indicator = jnp.where(is_pad, jnp.where(lane == C_PAD_SLOT, 1.0, 0.0), _chunk_indicator(pos0, rows, 0))
    indicator = indicator.astype(BF16)
    for h in range(heads):
        kn = _rms(k_ref[:, h * LANES:(h + 1) * LANES].astype(F32), gk, C_DH).astype(BF16)
        ko_ref[:, 2 * h * LANES:(2 * h + 1) * LANES] = jnp.where(is_pad, jnp.zeros_like(kn), kn)
        ko_ref[:, (2 * h + 1) * LANES:(2 * h + 2) * LANES] = indicator


def _prep_c(u, off_k, g_k, batch, seq, heads, pad):
    w = heads * LANES
    n_seq = seq // pad
    return pl.pallas_call(
        functools.partial(_prep_c_kernel, heads=heads),
        out_shape=jax.ShapeDtypeStruct((batch * (seq + pad), 2 * w), BF16),
        grid=(batch, n_seq + 1),
        in_specs=[
            pl.BlockSpec((pad, w), lambda b, r: (b * n_seq + jnp.maximum(r - 1, 0), off_k // w)),
            pl.BlockSpec((1, LANES), lambda b, r: (0, 0)),
        ],
        out_specs=pl.BlockSpec((pad, 2 * w), lambda b, r: (b * (n_seq + 1) + r, 0)),
        compiler_params=_params(("parallel", "parallel"), 3 * pad * w * 2, 4 * pad * LANES * 4),
        name="band_prep_k",
    )(u, g_k)


def _attn_c_kernel(qt_ref, k_ref, vt_ref, z_ref, rel_ref, o_ref, *scratch, tq, tw):
    n_q = qt_ref.shape[2]
    s_refs, p_refs, (bias_ref,) = scratch[:N_BUF], scratch[N_BUF:2 * N_BUF], scratch[2 * N_BUF:]
    width = rel_ref.shape[-1]
    toeplitz = pltpu.roll(jnp.broadcast_to(rel_ref[0], (tq, width)), 0, 1, stride=1, stride_axis=0)
    bias_ref[...] = (toeplitz[:, :tw] * LOG2E).T

    def scores(i):
        return jnp.dot(k_ref[i * tq:i * tq + tw, :], qt_ref[0, 0, i], preferred_element_type=F32) + bias_ref[...]

    s_refs[0][...] = scores(0)
    for i in range(n_q + 1):
        cur, nxt, prv = i % N_BUF, (i + 1) % N_BUF, (i - 1) % N_BUF
        if i + 1 < n_q:
            s_refs[nxt][...] = scores(i + 1)
        if i < n_q:
            s = s_refs[cur][...]
            p = jnp.exp2(s - jnp.max(s, axis=0, keepdims=True))
            p_refs[cur][...] = p.astype(BF16)
            l = jnp.sum(p, axis=0, keepdims=True)
        if i > 0:
            acc = None
            for a in range(tw // tq):
                part = jnp.dot(vt_ref[0, 0, i - 1 + a], p_refs[prv][a * tq:(a + 1) * tq, :],
                               preferred_element_type=F32)
                acc = part if acc is None else acc + part
            _silu_gate_store(o_ref, z_ref, slice((i - 1) * tq, i * tq), (acc * (1.0 / l_prev)).T)
        l_prev = l


def _attn_c(qt, k, vt, u, z_off, rel_rows, batch, seq, heads, tq, pad):
    n_q = seq // tq
    tw = tq + pad
    zb = z_off // LANES
    tiles = lambda b, h: (b, h, 0, 0, 0)
    blocks = seq * 2 * LANES * 2 + (seq + pad) * 3 * LANES * 2 + 2 * seq * LANES * 2
    return pl.pallas_call(
        functools.partial(_attn_c_kernel, tq=tq, tw=tw),
        out_shape=jax.ShapeDtypeStruct((batch * seq, heads * LANES), BF16),
        grid=(batch, heads),
        in_specs=[
            pl.BlockSpec((1, 1, n_q, 2 * LANES, tq), tiles),
            pl.BlockSpec((seq + pad, 2 * LANES), lambda b, h: (b, h)),
            pl.BlockSpec((1, 1, (seq + pad) // tq, C_DH, tq), tiles),
            pl.BlockSpec((seq, LANES), lambda b, h: (b, zb + h)),
            pl.BlockSpec((1, 1, rel_rows.shape[-1]), lambda b, h: (h, 0, 0)),
        ],
        out_specs=pl.BlockSpec((seq, LANES), lambda b, h: (b, h)),
        scratch_shapes=([pltpu.VMEM((tw, tq), F32)] * N_BUF + [pltpu.VMEM((tw, tq), BF16)] * N_BUF
                        + [pltpu.VMEM((tw, tq), F32)]),
        compiler_params=_params(("parallel", "parallel"), blocks, 12 * tw * tq * 4),
        name="band_attention",
    )(qt, k, vt, u, rel_rows)


def _rope_table(seq, dim, theta, group):
    half = dim // 2
    inv = 1.0 / (jnp.float32(theta) ** (jnp.arange(0, dim, 2, dtype=F32) / dim))
    ang = jnp.arange(seq, dtype=F32)[:, None] * inv[None, :]
    cos, sin = jnp.cos(ang), jnp.sin(ang)
    lane = jnp.arange(LANES) % group
    idx = lane % half
    in_lo = (lane < half)[None, :]
    in_hi = ((lane >= half) & (lane < dim))[None, :]
    c = jnp.where(in_lo | in_hi, cos[:, idx], 1.0 if group < LANES else 0.0)
    s_lo = jnp.where(in_lo, -sin[:, idx], 0.0)
    s_hi = jnp.where(in_hi, sin[:, idx], 0.0)
    return jnp.concatenate([c, s_lo, s_hi], axis=1)


def _band_rel_rows(rel_bias, tq, pad):
    width = pl.next_power_of_2(2 * tq + pad)
    e = jnp.arange(width)
    e = jnp.where(e < tq + pad, e, e - width)
    rel = jnp.clip(pad - e, -(CHUNK - 1), C_REL_MAX) + (CHUNK - 1)
    return rel_bias.astype(F32)[:, None, rel]


def _row(v):
    return v.astype(F32).reshape(1, -1)


def _pad_lanes(v, width):
    return jnp.pad(v, ((0, 0),) * (v.ndim - 1) + ((0, width - v.shape[-1]),))


def _layer(x2, layer_idx, batch, seq, rope_a, rope_b, g_pre, w_main, w_t_b, w_t_c, w_kr, a_g_cq, a_g_ckv, a_w_uq,
           a_w_ukv, a_g_q, a_g_k, b_g_q, b_g_k, b_lam, b_g_sub, c_g_q, c_g_k, c_rel_bias, w_branch, w_out):
    d = x2.shape[1]
    bw = w_branch.shape[1]
    q_rank, kv_rank = a_w_uq.shape[0], a_w_ukv.shape[0]
    a_heads, b_heads, c_heads = bw // A_V, bw // B_DV, bw // C_DH

    o_az, o_bk, o_bz, o_ck, o_cz, o_g = (n * bw for n in range(6))
    o_acq = o_g + N_BRANCH * d
    o_ackv = o_acq + q_rank

    h, h_t = _rmsnorm(x2, _row(g_pre))
    u = _matmul(h, w_main, BF16, 1024, MAIN_COLUMN_TILE, "in_proj")

    hw = 2 * LANES
    t_a = min(512, seq)
    w_uq = _pad_lanes(a_w_uq.reshape(q_rank, a_heads, A_NOPE + A_ROPE), hw).reshape(q_rank, a_heads * hw)
    w_ukv = a_w_ukv.reshape(kv_rank, a_heads, A_NOPE + A_V)
    w_k = w_ukv[:, :, :A_NOPE].reshape(kv_rank, a_heads * A_NOPE).astype(BF16)
    w_vt = w_ukv[:, :, A_NOPE:].reshape(kv_rank, a_heads * A_V).T.astype(BF16)
    g_q = jnp.concatenate([_row(a_g_q[:A_NOPE]), _pad_lanes(_row(a_g_q[A_NOPE:]), LANES)], axis=1)
    g_k = jnp.concatenate([_row(a_g_k[:A_NOPE]), _pad_lanes(_row(a_g_k[A_NOPE:]), LANES)], axis=1)
    qa, ka, vta = _prep_a(u, h, w_kr, o_acq, o_ackv, _row(a_g_cq), _row(a_g_ckv), w_uq.T.astype(BF16), w_k, w_vt,
                          g_q.T, g_k, rope_a, rope_a.T, batch, seq, a_heads, t_a)
    o_a = _attn_a(qa, ka, vta, u, o_az, batch, seq, a_heads, t_a)

    tq_b, tk_b = min(256, seq), min(512, seq)
    qb, vtb = _proj_t_b(w_t_b, h_t, jnp.tile(_row(b_g_q), (1, 2)).T, rope_b.T, batch, seq, b_heads, tq_b, tk_b)
    kb = _prep_b(u, o_bk, jnp.tile(_row(b_g_k), (1, 2)), rope_b, seq, b_heads, tk_b)
    lam_init = 0.8 - 0.6 * math.exp(-0.3 * layer_idx)
    o_b = _attn_b(qb, kb, vtb, u, o_bz, b_lam.astype(F32), b_g_sub.astype(F32).reshape(B_DV, 1), lam_init,
                  batch, seq, b_heads, tq_b, tk_b)

    pad = C_LEFT_CHUNKS * CHUNK
    tq_c = min(256, seq)
    qc, vtc = _proj_t_c(w_t_c, h_t, _row(c_g_q).T, batch, seq, c_heads, pad, tq_c)
    kc = _prep_c(u, o_ck, _row(c_g_k), batch, seq, c_heads, pad)
    o_c = _attn_c(qc, kc, vtc, u, o_cz, _band_rel_rows(c_rel_bias, tq_c, pad), batch, seq, c_heads, tq_c, pad)

    y = _merge(o_a, o_b, o_c, w_branch.astype(BF16), u, o_g, 1024, 1024)
    return _matmul_residual(y, w_out.astype(BF16), x2, 1024, 1024)


def kernel(x, g_pre, w_in, a_g_cq, a_g_ckv, a_w_uq, a_w_ukv, a_g_q, a_g_k, b_g_q, b_g_k, b_lam, b_g_sub,
           c_g_q, c_g_k, c_rel_bias, w_branch, w_out):
    batch, seq, d = x.shape
    rope_a = _rope_table(seq, A_ROPE, A_ROPE_THETA, LANES)
    rope_b = _rope_table(seq, B_ROT, B_ROPE_THETA, B_DK)
    x2 = x.reshape(batch * seq, d)
    q_rank, kv_rank, bw = a_w_uq.shape[1], a_w_ukv.shape[1], w_branch.shape[2]
    n_used = 5 * bw + N_BRANCH * d + q_rank + kv_rank
    n_main = -(-n_used // MAIN_COLUMN_TILE) * MAIN_COLUMN_TILE
    w_in_t = jnp.swapaxes(w_in, 1, 2)
    for l in range(g_pre.shape[0]):
        w_main, w_t_b, w_t_c, w_kr = _weights(w_in_t, l, q_rank, kv_rank, bw, N_BRANCH * d, n_main)
        x2 = _layer(x2, l, batch, seq, rope_a, rope_b, g_pre[l], w_main, w_t_b, w_t_c, w_kr, a_g_cq[l],
                    a_g_ckv[l], a_w_uq[l], a_w_ukv[l], a_g_q[l], a_g_k[l], b_g_q[l], b_g_k[l], b_lam[l],
                    b_g_sub[l], c_g_q[l], c_g_k[l], c_rel_bias[l], w_branch[l], w_out[l])
    return x2.reshape(batch, seq, d)
```

```python
import functools
import math

import jax
import jax.numpy as jnp
from jax import lax
from jax.experimental import pallas as pl
from jax.experimental.pallas import tpu as pltpu

F32 = jnp.float32
BF16 = jnp.bfloat16

EPS = 1e-6
CHUNK = 64
MASKED = -1e30
LOG2E = math.log2(math.e)

A_NOPE, A_ROPE, A_V = 128, 64, 128
A_ROPE_THETA = 10000.0
B_DK, B_DV = 64, 128
B_ROT = B_DK // 4
B_ROPE_THETA = 500000.0
C_DH = 128
C_LEFT_CHUNKS = 8
C_REL_MAX = 128
N_BRANCH = 3

PROJ_HEAD_GROUP = 2
N_BUF = 4
MAIN_COLUMN_TILE = 2048
STEPS_PER_ITERATION = 24
LANES = 128
V7X_VMEM_BUDGET = 56 * 2**20


def _params(semantics, block_bytes, temp_bytes=0):
    need = 2 * block_bytes + temp_bytes + (4 << 20)
    return pltpu.CompilerParams(
        dimension_semantics=semantics,
        vmem_limit_bytes=int(min(max(need, 16 << 20), V7X_VMEM_BUDGET)),
    )


def _sigmoid(z):
    return 1.0 / (1.0 + jnp.exp(-z))


def _rms(x, g, n):
    ss = jnp.sum(x * x, axis=-1, keepdims=True) * (1.0 / n)
    return x * lax.rsqrt(ss + EPS) * g


def _rmsnorm_kernel(x_ref, g_ref, o_ref, ot_ref):
    x = x_ref[...]
    y = _rms(x, g_ref[...], x.shape[-1])
    o_ref[...] = y.astype(o_ref.dtype)
    ot_ref[...] = y.T.astype(ot_ref.dtype)


def _rmsnorm(x, g, tm=512):
    t, d = x.shape
    return pl.pallas_call(
        _rmsnorm_kernel,
        out_shape=(jax.ShapeDtypeStruct((t, d), BF16), jax.ShapeDtypeStruct((d, t), BF16)),
        grid=(t // tm,),
        in_specs=[pl.BlockSpec((tm, d), lambda i: (i, 0)), pl.BlockSpec((1, d), lambda i: (0, 0))],
        out_specs=(pl.BlockSpec((tm, d), lambda i: (i, 0)), pl.BlockSpec((d, tm), lambda i: (0, i))),
        compiler_params=_params(("parallel",), tm * d * 8, tm * d * 12),
        name="pre_rmsnorm",
    )(x, g)


def _mm_kernel(a_ref, b_ref, o_ref):
    o_ref[...] = jnp.dot(a_ref[...], b_ref[...], preferred_element_type=F32).astype(o_ref.dtype)


def _matmul(a, b, out_dtype, tm, tn, name):
    m, k = a.shape
    n = b.shape[1]
    blocks = tm * k * 2 + k * tn * 2 + tm * tn * jnp.dtype(out_dtype).itemsize
    return pl.pallas_call(
        _mm_kernel,
        out_shape=jax.ShapeDtypeStruct((m, n), out_dtype),
        grid=(m // tm, n // tn),
        in_specs=[pl.BlockSpec((tm, k), lambda i, j: (i, 0)), pl.BlockSpec((k, tn), lambda i, j: (0, j))],
        out_specs=pl.BlockSpec((tm, tn), lambda i, j: (i, j)),
        compiler_params=_params(("parallel", "parallel"), blocks, tm * tn * 4),
        name=name,
    )(a, b)


def _mm_res_kernel(a_ref, b_ref, x_ref, o_ref):
    o_ref[...] = x_ref[...] + jnp.dot(a_ref[...], b_ref[...], preferred_element_type=F32)


def _matmul_residual(a, b, x, tm, tn):
    m, k = a.shape
    n = b.shape[1]
    blocks = tm * k * 2 + k * tn * 2 + 2 * tm * tn * 4
    return pl.pallas_call(
        _mm_res_kernel,
        out_shape=jax.ShapeDtypeStruct((m, n), F32),
        grid=(m // tm, n // tn),
        in_specs=[
            pl.BlockSpec((tm, k), lambda i, j: (i, 0)),
            pl.BlockSpec((k, tn), lambda i, j: (0, j)),
            pl.BlockSpec((tm, tn), lambda i, j: (i, j)),
        ],
        out_specs=pl.BlockSpec((tm, tn), lambda i, j: (i, j)),
        compiler_params=_params(("parallel", "parallel"), blocks, tm * tn * 4),
        name="out_proj_residual",
    )(a, b, x)


def _weights_kernel(wt_ref, main_ref, tb_ref, tc_ref, kr_ref, *, q_rank, kv_rank, bw, n_gate):
    s_kr = q_rank + kv_rank
    s_az = s_kr + A_ROPE

    def piece(n):
        return wt_ref[0, s_az + n * bw:s_az + (n + 1) * bw, :]

    for slot, n in enumerate((0, 2, 4, 6, 8)):
        main_ref[:, slot * bw:(slot + 1) * bw] = piece(n).T.astype(BF16)
    o_g = 5 * bw
    for g in range(n_gate // bw):
        main_ref[:, o_g + g * bw:o_g + (g + 1) * bw] = piece(9 + g).T.astype(BF16)
    main_ref[:, o_g + n_gate:o_g + n_gate + s_kr] = wt_ref[0, :s_kr, :].T.astype(BF16)
    n_pad = main_ref.shape[1] - (o_g + n_gate + s_kr)
    main_ref[:, o_g + n_gate + s_kr:] = jnp.zeros((main_ref.shape[0], n_pad), BF16)
    kr_ref[:, :A_ROPE] = wt_ref[0, s_kr:s_az, :].T.astype(BF16)
    kr_ref[:, A_ROPE:] = jnp.zeros((kr_ref.shape[0], LANES - A_ROPE), BF16)
    tb_ref[:bw, :] = piece(1).astype(BF16)
    tb_ref[bw:, :] = piece(3).astype(BF16)
    tc_ref[:bw, :] = piece(5).astype(BF16)
    tc_ref[bw:, :] = piece(7).astype(BF16)


def _weights(w_in_t, layer, q_rank, kv_rank, bw, n_gate, n_main, cols=128):
    _, d_in, d = w_in_t.shape
    blocks = cols * d_in * 4 + cols * (n_main + LANES + 4 * bw) * 2
    return pl.pallas_call(
        functools.partial(_weights_kernel, q_rank=q_rank, kv_rank=kv_rank, bw=bw, n_gate=n_gate),
        out_shape=(jax.ShapeDtypeStruct((d, n_main), BF16),
                   jax.ShapeDtypeStruct((2 * bw, d), BF16),
                   jax.ShapeDtypeStruct((2 * bw, d), BF16),
                   jax.ShapeDtypeStruct((d, LANES), BF16)),
        grid=(d // cols,),
        in_specs=[pl.BlockSpec((1, d_in, cols), lambda r: (layer, 0, r))],
        out_specs=(pl.BlockSpec((cols, n_main), lambda r: (r, 0)),
                   pl.BlockSpec((2 * bw, cols), lambda r: (0, r)),
                   pl.BlockSpec((2 * bw, cols), lambda r: (0, r)),
                   pl.BlockSpec((cols, LANES), lambda r: (r, 0))),
        compiler_params=_params(("parallel",), blocks, 4 * cols * bw * 4),
        name="weights_prep",
    )(w_in_t)


def _merge_kernel(oa_ref, ob_ref, oc_ref, w_ref, ga_ref, gb_ref, gc_ref, y_ref):
    y = None
    for n, (o_ref, g_ref) in enumerate(((oa_ref, ga_ref), (ob_ref, gb_ref), (oc_ref, gc_ref))):
        gate = _sigmoid(g_ref[...].astype(F32))
        term = gate * jnp.dot(o_ref[...], w_ref[n], preferred_element_type=F32)
        y = term if y is None else y + term
    y_ref[...] = y.astype(y_ref.dtype)


def _merge(o_a, o_b, o_c, w_branch, u, gate_off, tm, tn):
    t, bw = o_a.shape
    d = w_branch.shape[2]
    nj = d // tn
    g0 = gate_off // tn
    branch_spec = pl.BlockSpec((tm, bw), lambda i, j: (i, 0))
    gate_specs = [pl.BlockSpec((tm, tn), functools.partial(lambda i, j, n: (i, g0 + n * nj + j), n=n))
                  for n in range(N_BRANCH)]
    blocks = 3 * tm * bw * 2 + N_BRANCH * bw * tn * 2 + 3 * tm * tn * 2 + tm * tn * 2
    return pl.pallas_call(
        _merge_kernel,
        out_shape=jax.ShapeDtypeStruct((t, d), BF16),
        grid=(t // tm, nj),
        in_specs=[branch_spec, branch_spec, branch_spec,
                  pl.BlockSpec((N_BRANCH, bw, tn), lambda i, j: (0, 0, j))] + gate_specs,
        out_specs=pl.BlockSpec((tm, tn), lambda i, j: (i, j)),
        compiler_params=_params(("parallel", "parallel"), blocks, 3 * tm * tn * 4),
        name="gated_merge",
    )(o_a, o_b, o_c, w_branch, u, u, u)


def _chunk_indicator(pos0, rows, first_lane):
    chunk = (pos0 + lax.broadcasted_iota(jnp.int32, (rows, LANES), 0)) // CHUNK
    lane = lax.broadcasted_iota(jnp.int32, (rows, LANES), 1)
    return jnp.where(lane - first_lane == chunk, 1.0, 0.0)


def _chunk_mask_rows(pos0, cols, first_row):
    chunk = (pos0 + lax.broadcasted_iota(jnp.int32, (LANES, cols), 1)) // CHUNK
    row = lax.broadcasted_iota(jnp.int32, (LANES, cols), 0)
    return jnp.where((row >= first_row) & (row - first_row > chunk), MASKED, 0.0)


def _flash_loop(qt_ref, k_ref, vt_ref, s_refs, p_refs, acc_ref, l_ref, *, tk, ratio):
    n_q, _, cols = acc_ref.shape

    def scores(qi, j):
        k = k_ref[pl.ds(pl.multiple_of(j * tk, tk), tk), :]
        return jnp.dot(k, qt_ref[0, 0, qi], preferred_element_type=F32)

    def step(cur, carry):
        nxt, prv = (cur + 1) % N_BUF, (cur - 1) % N_BUF
        qi, j, m, l, alpha_prev, qi_prev, j_prev = carry
        last = j == qi // ratio
        qi_next = jnp.where(last, jnp.minimum(qi + 1, n_q - 1), qi)
        j_next = jnp.where(last, 0, j + 1)
        s_refs[nxt][...] = scores(qi_next, j_next)
        s = s_refs[cur][...]
        m_new = jnp.maximum(m, jnp.max(s, axis=0, keepdims=True))
        alpha = jnp.exp2(m - m_new)
        p = jnp.exp2(s - m_new)
        p_refs[cur][...] = p.astype(BF16)
        l_new = alpha * l + jnp.sum(p, axis=0, keepdims=True)
        half_cols = cols // 2
        for c in range(2):
            cs = slice(c * half_cols, (c + 1) * half_cols)
            acc = alpha_prev[:, cs] * acc_ref[qi_prev, :, cs] + jnp.dot(
                vt_ref[0, 0, j_prev], p_refs[prv][:, cs], preferred_element_type=F32)
            acc_ref[qi_prev, :, cs] = acc
        l_ref[qi_prev] = l
        return qi_next, j_next, jnp.where(last, MASKED, m_new), l_new, alpha, qi, j

    def steps(count, carry):
        for i in range(count):
            carry = step(i % N_BUF, carry)
        return carry

    zero = jnp.int32(0)
    s_refs[0][...] = scores(zero, zero)
    p_refs[N_BUF - 1][...] = jnp.zeros(p_refs[0].shape, BF16)
    row = jnp.zeros((1, cols), F32)
    acc_ref[...] = jnp.zeros(acc_ref.shape, F32)
    carry = (zero, zero, row + MASKED, row, row, zero, zero)
    n_steps = sum(qi // ratio + 1 for qi in range(n_q)) + 1
    carry = lax.fori_loop(0, n_steps // STEPS_PER_ITERATION, lambda _, c: steps(STEPS_PER_ITERATION, c), carry)
    steps(n_steps % STEPS_PER_ITERATION, carry)


def _rotate_rows(y, half, rope_t):
    cos, sin_lo, sin_hi = rope_t[:LANES], rope_t[LANES:2 * LANES], rope_t[2 * LANES:]
    up = jnp.concatenate([y[half:], y[:half]], axis=0)
    down = jnp.concatenate([y[-half:], y[:-half]], axis=0)
    return y * cos + up * sin_lo + down * sin_hi


def _silu_gate_store(o_ref, z_ref, rows, y):
    z = z_ref[rows, :].astype(F32)
    o_ref[rows, :] = (y * (z * _sigmoid(z))).astype(o_ref.dtype)


def _prep_a_kernel(cq_ref, ckv_ref, h_ref, wkr_ref, gcq_ref, gckv_ref, wuqt_ref, wk_ref, wvt_ref, gq_ref, gk_ref,
                   rope_ref, ropet_ref, qt_ref, k_ref, vt_ref, *, heads, scale, n_seq):
    rows = cq_ref.shape[0]
    pos0 = (pl.program_id(0) % n_seq) * rows
    rope = rope_ref[...]
    rope_t = ropet_ref[...]
    cos, sin_lo, sin_hi = rope[:, :LANES], rope[:, LANES:2 * LANES], rope[:, 2 * LANES:]

    def rotate(y):
        return y * cos + pltpu.roll(y, LANES - A_ROPE // 2, 1) * sin_lo + pltpu.roll(y, A_ROPE // 2, 1) * sin_hi

    cq = cq_ref[...].astype(F32)
    cq = _rms(cq, gcq_ref[...], cq.shape[-1])
    ckv = ckv_ref[...].astype(F32)
    ckv = _rms(ckv, gckv_ref[...], ckv.shape[-1])
    qat = jnp.dot(wuqt_ref[...], cq.T.astype(BF16), preferred_element_type=F32)
    vat = jnp.dot(wvt_ref[...], ckv.T.astype(BF16), preferred_element_type=F32)
    kna = jnp.dot(ckv.astype(BF16), wk_ref[...], preferred_element_type=F32)
    gq = gq_ref[...]
    gk = gk_ref[...]
    kr = jnp.dot(h_ref[...], wkr_ref[...], preferred_element_type=F32)
    k_rope = (rotate(_rms(kr, gk[:, LANES:], A_ROPE)) + _chunk_indicator(pos0, rows, A_ROPE)).astype(BF16)
    mask_rows = _chunk_mask_rows(pos0, rows, A_ROPE)
    hw = 2 * LANES
    for h in range(heads):
        xn = qat[h * hw:h * hw + LANES]
        q_nope = xn * lax.rsqrt(jnp.sum(xn * xn, axis=0, keepdims=True) * (1.0 / A_NOPE) + EPS) * gq[:LANES]
        xr = qat[h * hw + LANES:(h + 1) * hw]
        q_rope = xr * lax.rsqrt(jnp.sum(xr * xr, axis=0, keepdims=True) * (1.0 / A_ROPE) + EPS) * gq[LANES:]
        q_rope = _rotate_rows(q_rope, A_ROPE // 2, rope_t)
        qt_ref[0, h, 0, :LANES, :] = (q_nope * scale).astype(BF16)
        qt_ref[0, h, 0, LANES:, :] = (q_rope * scale + mask_rows).astype(BF16)
        k_nope = _rms(kna[:, h * LANES:(h + 1) * LANES], gk[:, :LANES], A_NOPE)
        k_ref[:, h * hw:h * hw + LANES] = k_nope.astype(BF16)
        k_ref[:, h * hw + LANES:(h + 1) * hw] = k_rope
        vt_ref[0, h, 0] = vat[h * LANES:(h + 1) * LANES].astype(BF16)


def _prep_a(u, h, w_kr, off_cq, off_ckv, g_cq, g_ckv, w_uqt, w_k, w_vt, g_q, g_k, rope, rope_t, batch, seq, heads,
            t_rows):
    t = u.shape[0]
    q_rank, kv_rank = w_uqt.shape[1], w_k.shape[0]
    hw = 2 * LANES
    n_seq = seq // t_rows
    assert seq // CHUNK <= LANES - A_ROPE
    scale = (A_NOPE + A_ROPE) ** -0.5 * LOG2E
    row = lambda i: (i, 0)
    const = lambda i: (0, 0)
    tile = lambda i: (i // n_seq, 0, i % n_seq, 0, 0)
    d = h.shape[1]
    blocks = (t_rows * (q_rank + kv_rank + d) * 2 + d * LANES * 2 + (q_rank * hw + 2 * kv_rank * LANES) * heads * 2
              + 2 * t_rows * 3 * LANES * 4 + t_rows * heads * (2 * hw + LANES) * 2)
    return pl.pallas_call(
        functools.partial(_prep_a_kernel, heads=heads, scale=scale, n_seq=n_seq),
        out_shape=(jax.ShapeDtypeStruct((batch, heads, n_seq, hw, t_rows), BF16),
                   jax.ShapeDtypeStruct((t, heads * hw), BF16),
                   jax.ShapeDtypeStruct((batch, heads, n_seq, A_V, t_rows), BF16)),
        grid=(t // t_rows,),
        in_specs=[
            pl.BlockSpec((t_rows, q_rank), lambda i: (i, off_cq // q_rank)),
            pl.BlockSpec((t_rows, kv_rank), lambda i: (i, off_ckv // kv_rank)),
            pl.BlockSpec((t_rows, d), row),
            pl.BlockSpec((d, LANES), const),
            pl.BlockSpec((1, q_rank), const),
            pl.BlockSpec((1, kv_rank), const),
            pl.BlockSpec((heads * hw, q_rank), const),
            pl.BlockSpec((kv_rank, heads * LANES), const),
            pl.BlockSpec((heads * LANES, kv_rank), const),
            pl.BlockSpec((hw, 1), const),
            pl.BlockSpec((1, hw), const),
            pl.BlockSpec((t_rows, 3 * LANES), lambda i: (i % n_seq, 0)),
            pl.BlockSpec((3 * LANES, t_rows), lambda i: (0, i % n_seq)),
        ],
        out_specs=(pl.BlockSpec((1, heads, 1, hw, t_rows), tile),
                   pl.BlockSpec((t_rows, heads * hw), row),
                   pl.BlockSpec((1, heads, 1, A_V, t_rows), tile)),
        compiler_params=_params(("parallel",), blocks, 4 * t_rows * heads * hw * 4),
        name="mla_prep",
    )(u, u, h, w_kr, g_cq, g_ckv, w_uqt, w_k, w_vt, g_q, g_k, rope, rope_t)


def _attn_a_kernel(qt_ref, k_ref, vt_ref, z_ref, o_ref, *scratch, t):
    s_refs, p_refs, (acc_ref, l_ref) = scratch[:N_BUF], scratch[N_BUF:2 * N_BUF], scratch[2 * N_BUF:]
    _flash_loop(qt_ref, k_ref, vt_ref, s_refs, p_refs, acc_ref, l_ref, tk=t, ratio=1)
    for qi in range(acc_ref.shape[0]):
        _silu_gate_store(o_ref, z_ref, slice(qi * t, (qi + 1) * t), (acc_ref[qi] * (1.0 / l_ref[qi])).T)


def _attn_a(qt, k, vt, u, z_off, batch, seq, heads, t):
    n_tiles = seq // t
    dk = k.shape[1] // heads
    zb = z_off // LANES
    seq_head = lambda b, h: (b, h)
    tiles = lambda b, h: (b, h, 0, 0, 0)
    blocks = 2 * seq * dk * 2 + 3 * seq * LANES * 2
    scratch = N_BUF * t * t * 6 + n_tiles * A_V * t * 4
    return pl.pallas_call(
        functools.partial(_attn_a_kernel, t=t),
        out_shape=jax.ShapeDtypeStruct((batch * seq, heads * LANES), BF16),
        grid=(batch, heads),
        in_specs=[
            pl.BlockSpec((1, 1, n_tiles, dk, t), tiles),
            pl.BlockSpec((seq, dk), seq_head),
            pl.BlockSpec((1, 1, n_tiles, A_V, t), tiles),
            pl.BlockSpec((seq, LANES), lambda b, h: (b, zb + h)),
        ],
        out_specs=pl.BlockSpec((seq, LANES), seq_head),
        scratch_shapes=([pltpu.VMEM((t, t), F32)] * N_BUF + [pltpu.VMEM((t, t), BF16)] * N_BUF
                        + [pltpu.VMEM((n_tiles, A_V, t), F32), pltpu.VMEM((n_tiles, 1, t), F32)]),
        compiler_params=_params(("parallel", "parallel"), blocks, scratch + 6 * t * t * 4),
        name="mla_attention",
    )(qt, k, vt, u)


def _proj_t_b_kernel(wt_ref, ht_ref, gq_ref, ropet_ref, qt_ref, vt_ref, *, heads, scale, n_seq, tq):
    cols = ht_ref.shape[1]
    pos0 = (pl.program_id(0) % n_seq) * cols
    ht = ht_ref[...]
    group = PROJ_HEAD_GROUP * LANES
    rope_t = ropet_ref[...]
    gq = gq_ref[...]
    first_map = lax.broadcasted_iota(jnp.int32, (LANES, 1), 0) < B_DK
    mask_rows = _chunk_mask_rows(pos0, cols, 0).astype(BF16)
    for h in range(heads):
        if h % PROJ_HEAD_GROUP == 0:
            res_q = jnp.dot(wt_ref[h * LANES:h * LANES + group], ht, preferred_element_type=F32)
            res_v = jnp.dot(wt_ref[(heads + h) * LANES:(heads + h) * LANES + group], ht,
                            preferred_element_type=F32)
        g = h % PROJ_HEAD_GROUP
        x = res_q[g * LANES:(g + 1) * LANES]
        xsq = x * x
        ss_lo = jnp.sum(xsq[:B_DK], axis=0, keepdims=True)
        ss_hi = jnp.sum(xsq[B_DK:], axis=0, keepdims=True)
        y = x * lax.rsqrt(jnp.where(first_map, ss_lo, ss_hi) * (1.0 / B_DK) + EPS) * gq
        q = _rotate_rows(y, B_ROT // 2, rope_t) * scale
        q_lo = jnp.where(first_map, q, 0.0).astype(BF16)
        q_hi = jnp.where(first_map, 0.0, q).astype(BF16)
        for a in range(cols // tq):
            cs = slice(a * tq, (a + 1) * tq)
            qt_ref[0, h, a, :LANES, :tq] = q_lo[:, cs]
            qt_ref[0, h, a, :LANES, tq:] = q_hi[:, cs]
            qt_ref[0, h, a, LANES:, :tq] = mask_rows[:, cs]
            qt_ref[0, h, a, LANES:, tq:] = mask_rows[:, cs]
        vt_ref[0, h, 0] = res_v[g * LANES:(g + 1) * LANES].astype(BF16)


def _proj_t_b(w_t, h_t, g_q, rope_t, batch, seq, heads, tq, tk):
    d, t = h_t.shape
    n_seq = seq // tk
    assert seq // CHUNK <= LANES
    tile = lambda i: (i // n_seq, 0, i % n_seq, 0, 0)
    blocks = w_t.size * 2 + d * tk * 2 + 3 * LANES * tk * 4 + heads * tk * (4 * LANES + B_DV) * 2
    return pl.pallas_call(
        functools.partial(_proj_t_b_kernel, heads=heads, scale=B_DK ** -0.5 * LOG2E, n_seq=n_seq, tq=tq),
        out_shape=(jax.ShapeDtypeStruct((batch, heads, seq // tq, 2 * LANES, 2 * tq), BF16),
                   jax.ShapeDtypeStruct((batch, heads, n_seq, B_DV, tk), BF16)),
        grid=(t // tk,),
        in_specs=[
            pl.BlockSpec(w_t.shape, lambda i: (0, 0)),
            pl.BlockSpec((d, tk), lambda i: (0, i)),
            pl.BlockSpec((LANES, 1), lambda i: (0, 0)),
            pl.BlockSpec((3 * LANES, tk), lambda i: (0, i % n_seq)),
        ],
        out_specs=(pl.BlockSpec((1, heads, tk // tq, 2 * LANES, 2 * tq), tile),
                   pl.BlockSpec((1, heads, 1, B_DV, tk), tile)),
        compiler_params=_params(("parallel",), blocks, 3 * w_t.shape[0] * tk * 4),
        name="diff_proj_qv",
    )(w_t, h_t, g_q, rope_t)


def _prep_b_kernel(k_ref, gk_ref, rope_ref, ko_ref, *, heads, n_seq):
    rows = k_ref.shape[0]
    pos0 = (pl.program_id(0) % n_seq) * rows
    rope = rope_ref[...]
    cos, sin_lo, sin_hi = rope[:, :LANES], rope[:, LANES:2 * LANES], rope[:, 2 * LANES:]
    first_map = lax.broadcasted_iota(jnp.int32, (1, LANES), 1) < B_DK
    gk = gk_ref[...]
    indicator = _chunk_indicator(pos0, rows, 0).astype(BF16)
    for h in range(heads):
        x = k_ref[:, h * LANES:(h + 1) * LANES].astype(F32)
        xsq = x * x
        ss_lo = jnp.sum(jnp.where(first_map, xsq, 0.0), axis=-1, keepdims=True)
        ss_hi = jnp.sum(jnp.where(first_map, 0.0, xsq), axis=-1, keepdims=True)
        y = x * lax.rsqrt(jnp.where(first_map, ss_lo, ss_hi) * (1.0 / B_DK) + EPS) * gk
        y = y * cos + pltpu.roll(y, LANES - B_ROT // 2, 1) * sin_lo + pltpu.roll(y, B_ROT // 2, 1) * sin_hi
        ko_ref[:, 2 * h * LANES:(2 * h + 1) * LANES] = y.astype(BF16)
        ko_ref[:, (2 * h + 1) * LANES:(2 * h + 2) * LANES] = indicator


def _prep_b(u, off_k, g_k, rope, seq, heads, t_rows):
    t = u.shape[0]
    w = heads * LANES
    n_seq = seq // t_rows
    return pl.pallas_call(
        functools.partial(_prep_b_kernel, heads=heads, n_seq=n_seq),
        out_shape=jax.ShapeDtypeStruct((t, 2 * w), BF16),
        grid=(t // t_rows,),
        in_specs=[
            pl.BlockSpec((t_rows, w), lambda i: (i, off_k // w)),
            pl.BlockSpec((1, LANES), lambda i: (0, 0)),
            pl.BlockSpec((t_rows, 3 * LANES), lambda i: (i % n_seq, 0)),
        ],
        out_specs=pl.BlockSpec((t_rows, 2 * w), lambda i: (i, 0)),
        compiler_params=_params(("parallel",), 3 * t_rows * w * 2 + t_rows * 3 * LANES * 4, 8 * t_rows * LANES * 4),
        name="diff_prep_k",
    )(u, g_k, rope)


def _attn_b_kernel(qt_ref, k_ref, vt_ref, z_ref, lam_ref, gsub_ref, o_ref, *scratch, tq, tk, lam_init):
    s_refs, p_refs, (acc_ref, l_ref) = scratch[:N_BUF], scratch[N_BUF:2 * N_BUF], scratch[2 * N_BUF:]
    _flash_loop(qt_ref, k_ref, vt_ref, s_refs, p_refs, acc_ref, l_ref, tk=tk, ratio=tk // tq)
    lf = lam_ref[...]
    lam = (jnp.exp(jnp.sum(lf[0:1] * lf[1:2], axis=-1, keepdims=True))
           - jnp.exp(jnp.sum(lf[2:3] * lf[3:4], axis=-1, keepdims=True)) + lam_init)
    gsub = gsub_ref[...]
    for qi in range(acc_ref.shape[0]):
        o = acc_ref[qi] * (1.0 / l_ref[qi])
        a = o[:, :tq] - lam * o[:, tq:]
        ss = jnp.sum(a * a, axis=0, keepdims=True) * (1.0 / B_DV)
        y = a * lax.rsqrt(ss + EPS) * gsub * (1.0 - lam_init)
        _silu_gate_store(o_ref, z_ref, slice(qi * tq, (qi + 1) * tq), y.T)


def _attn_b(qt, k, vt, u, z_off, lam, g_sub, lam_init, batch, seq, heads, tq, tk):
    n_q, n_k = seq // tq, seq // tk
    zb = z_off // LANES
    cols = 2 * tq
    seq_head = lambda b, h: (b, h)
    tiles = lambda b, h: (b, h, 0, 0, 0)
    blocks = 2 * seq * 2 * LANES * 2 + seq * 2 * LANES * 2 + 3 * seq * LANES * 2
    scratch = N_BUF * tk * cols * 6 + n_q * B_DV * cols * 4
    return pl.pallas_call(
        functools.partial(_attn_b_kernel, tq=tq, tk=tk, lam_init=lam_init),
        out_shape=jax.ShapeDtypeStruct((batch * seq, heads * LANES), BF16),
        grid=(batch, heads),
        in_specs=[
            pl.BlockSpec((1, 1, n_q, 2 * LANES, cols), tiles),
            pl.BlockSpec((seq, 2 * LANES), seq_head),
            pl.BlockSpec((1, 1, n_k, B_DV, tk), tiles),
            pl.BlockSpec((seq, LANES), lambda b, h: (b, zb + h)),
            pl.BlockSpec(lam.shape, lambda b, h: (0, 0)),
            pl.BlockSpec((B_DV, 1), lambda b, h: (0, 0)),
        ],
        out_specs=pl.BlockSpec((seq, LANES), seq_head),
        scratch_shapes=([pltpu.VMEM((tk, cols), F32)] * N_BUF + [pltpu.VMEM((tk, cols), BF16)] * N_BUF
                        + [pltpu.VMEM((n_q, B_DV, cols), F32), pltpu.VMEM((n_q, 1, cols), F32)]),
        compiler_params=_params(("parallel", "parallel"), blocks, scratch + 6 * tk * cols * 4),
        name="diff_attention",
    )(qt, k, vt, u, lam, g_sub)


C_PAD_SLOT = 64


def _proj_t_c_kernel(wt_ref, ht_ref, gq_ref, qt_ref, vt_ref, *, heads, scale, tq):
    cols = ht_ref.shape[1]
    r = pl.program_id(1)
    is_pad = r == 0
    pos0 = jnp.maximum(r - 1, 0) * cols
    ht = ht_ref[...]
    group = PROJ_HEAD_GROUP * LANES
    gq = gq_ref[...]
    q_chunk = (pos0 + lax.broadcasted_iota(jnp.int32, (LANES, cols), 1)) // CHUNK
    slot = lax.broadcasted_iota(jnp.int32, (LANES, cols), 0)
    out_of_band = ((slot < C_PAD_SLOT) & ((slot > q_chunk) | (slot < q_chunk - C_LEFT_CHUNKS))) | (slot == C_PAD_SLOT)
    mask_rows = jnp.where(out_of_band, MASKED, 0.0).astype(BF16)
    for h in range(heads):
        if h % PROJ_HEAD_GROUP == 0:
            res_q = jnp.dot(wt_ref[h * LANES:h * LANES + group], ht, preferred_element_type=F32)
            res_v = jnp.dot(wt_ref[(heads + h) * LANES:(heads + h) * LANES + group], ht,
                            preferred_element_type=F32)
        g = h % PROJ_HEAD_GROUP
        x = res_q[g * LANES:(g + 1) * LANES]
        q = x * lax.rsqrt(jnp.sum(x * x, axis=0, keepdims=True) * (1.0 / C_DH) + EPS) * gq * scale
        q = q.astype(BF16)
        v = res_v[g * LANES:(g + 1) * LANES].astype(BF16)
        v = jnp.where(is_pad, jnp.zeros_like(v), v)
        for a in range(cols // tq):
            cs = slice(a * tq, (a + 1) * tq)
            qt_ref[0, h, a, :LANES, :] = q[:, cs]
            qt_ref[0, h, a, LANES:, :] = mask_rows[:, cs]
            vt_ref[0, h, a] = v[:, cs]


def _proj_t_c(w_t, h_t, g_q, batch, seq, heads, pad, tq):
    d, t = h_t.shape
    n_seq = seq // pad
    per = pad // tq
    assert seq // CHUNK <= C_PAD_SLOT
    blocks = w_t.size * 2 + d * pad * 2 + heads * pad * 3 * LANES * 2
    return pl.pallas_call(
        functools.partial(_proj_t_c_kernel, heads=heads, scale=C_DH ** -0.5 * LOG2E, tq=tq),
        out_shape=(jax.ShapeDtypeStruct((batch, heads, seq // tq, 2 * LANES, tq), BF16),
                   jax.ShapeDtypeStruct((batch, heads, (seq + pad) // tq, C_DH, tq), BF16)),
        grid=(batch, n_seq + 1),
        in_specs=[
            pl.BlockSpec(w_t.shape, lambda b, r: (0, 0)),
            pl.BlockSpec((d, pad), lambda b, r: (0, b * n_seq + jnp.maximum(r - 1, 0))),
            pl.BlockSpec((LANES, 1), lambda b, r: (0, 0)),
        ],
        out_specs=(pl.BlockSpec((1, heads, per, 2 * LANES, tq), lambda b, r: (b, 0, jnp.maximum(r - 1, 0), 0, 0)),
                   pl.BlockSpec((1, heads, per, C_DH, tq), lambda b, r: (b, 0, r, 0, 0))),
        compiler_params=_params(("parallel", "arbitrary"), blocks, 3 * w_t.shape[0] * pad * 4),
        name="band_proj_qv",
    )(w_t, h_t, g_q)


def _prep_c_kernel(k_ref, gk_ref, ko_ref, *, heads):
    rows = k_ref.shape[0]
    r = pl.program_id(1)
    is_pad = r == 0
    pos0 = jnp.maximum(r - 1, 0) * rows
    gk = gk_ref[...]
    lane = lax.broadcasted_iota(jnp.int32, (rows, LANES), 1)
    indicator = jnp.where(is_pad, jnp.where(lane == C_PAD_SLOT, 1.0, 0.0), _chunk_indicator(pos0, rows, 0))
    indicator = indicator.astype(BF16)
    for h in range(heads):
        kn = _rms(k_ref[:, h * LANES:(h + 1) * LANES].astype(F32), gk, C_DH).astype(BF16)
        ko_ref[:, 2 * h * LANES:(2 * h + 1) * LANES] = jnp.where(is_pad, jnp.zeros_like(kn), kn)
        ko_ref[:, (2 * h + 1) * LANES:(2 * h + 2) * LANES] = indicator


def _prep_c(u, off_k, g_k, batch, seq, heads, pad):
    w = heads * LANES
    n_seq = seq // pad
    return pl.pallas_call(
        functools.partial(_prep_c_kernel, heads=heads),
        out_shape=jax.ShapeDtypeStruct((batch * (seq + pad), 2 * w), BF16),
        grid=(batch, n_seq + 1),
        in_specs=[
            pl.BlockSpec((pad, w), lambda b, r: (b * n_seq + jnp.maximum(r - 1, 0), off_k // w)),
            pl.BlockSpec((1, LANES), lambda b, r: (0, 0)),
        ],
        out_specs=pl.BlockSpec((pad, 2 * w), lambda b, r: (b * (n_seq + 1) + r, 0)),
        compiler_params=_params(("parallel", "parallel"), 3 * pad * w * 2, 4 * pad * LANES * 4),
        name="band_prep_k",
    )(u, g_k)


def _attn_c_kernel(qt_ref, k_ref, vt_ref, z_ref, rel_ref, o_ref, *scratch, tq, tw):
    n_q = qt_ref.shape[2]
    s_refs, p_refs, (bias_ref,) = scratch[:N_BUF], scratch[N_BUF:2 * N_BUF], scratch[2 * N_BUF:]
    width = rel_ref.shape[-1]
    toeplitz = pltpu.roll(jnp.broadcast_to(rel_ref[0], (tq, width)), 0, 1, stride=1, stride_axis=0)
    bias_ref[...] = (toeplitz[:, :tw] * LOG2E).T

    def scores(i):
        return jnp.dot(k_ref[i * tq:i * tq + tw, :], qt_ref[0, 0, i], preferred_element_type=F32) + bias_ref[...]

    s_refs[0][...] = scores(0)
    for i in range(n_q + 1):
        cur, nxt, prv = i % N_BUF, (i + 1) % N_BUF, (i - 1) % N_BUF
        if i + 1 < n_q:
            s_refs[nxt][...] = scores(i + 1)
        if i < n_q:
            s = s_refs[cur][...]
            p = jnp.exp2(s - jnp.max(s, axis=0, keepdims=True))
            p_refs[cur][...] = p.astype(BF16)
            l = jnp.sum(p, axis=0, keepdims=True)
        if i > 0:
            acc = None
            for a in range(tw // tq):
                part = jnp.dot(vt_ref[0, 0, i - 1 + a], p_refs[prv][a * tq:(a + 1) * tq, :],
                               preferred_element_type=F32)
                acc = part if acc is None else acc + part
            _silu_gate_store(o_ref, z_ref, slice((i - 1) * tq, i * tq), (acc * (1.0 / l_prev)).T)
        l_prev = l


def _attn_c(qt, k, vt, u, z_off, rel_rows, batch, seq, heads, tq, pad):
    n_q = seq // tq
    tw = tq + pad
    zb = z_off // LANES
    tiles = lambda b, h: (b, h, 0, 0, 0)
    blocks = seq * 2 * LANES * 2 + (seq + pad) * 3 * LANES * 2 + 2 * seq * LANES * 2
    return pl.pallas_call(
        functools.partial(_attn_c_kernel, tq=tq, tw=tw),
        out_shape=jax.ShapeDtypeStruct((batch * seq, heads * LANES), BF16),
        grid=(batch, heads),
        in_specs=[
            pl.BlockSpec((1, 1, n_q, 2 * LANES, tq), tiles),
            pl.BlockSpec((seq + pad, 2 * LANES), lambda b, h: (b, h)),
            pl.BlockSpec((1, 1, (seq + pad) // tq, C_DH, tq), tiles),
            pl.BlockSpec((seq, LANES), lambda b, h: (b, zb + h)),
            pl.BlockSpec((1, 1, rel_rows.shape[-1]), lambda b, h: (h, 0, 0)),
        ],
        out_specs=pl.BlockSpec((seq, LANES), lambda b, h: (b, h)),
        scratch_shapes=([pltpu.VMEM((tw, tq), F32)] * N_BUF + [pltpu.VMEM((tw, tq), BF16)] * N_BUF
                        + [pltpu.VMEM((tw, tq), F32)]),
        compiler_params=_params(("parallel", "parallel"), blocks, 12 * tw * tq * 4),
        name="band_attention",
    )(qt, k, vt, u, rel_rows)


def _rope_table(seq, dim, theta, group):
    half = dim // 2
    inv = 1.0 / (jnp.float32(theta) ** (jnp.arange(0, dim, 2, dtype=F32) / dim))
    ang = jnp.arange(seq, dtype=F32)[:, None] * inv[None, :]
    cos, sin = jnp.cos(ang), jnp.sin(ang)
    lane = jnp.arange(LANES) % group
    idx = lane % half
    in_lo = (lane < half)[None, :]
    in_hi = ((lane >= half) & (lane < dim))[None, :]
    c = jnp.where(in_lo | in_hi, cos[:, idx], 1.0 if group < LANES else 0.0)
    s_lo = jnp.where(in_lo, -sin[:, idx], 0.0)
    s_hi = jnp.where(in_hi, sin[:, idx], 0.0)
    return jnp.concatenate([c, s_lo, s_hi], axis=1)


def _band_rel_rows(rel_bias, tq, pad):
    width = pl.next_power_of_2(2 * tq + pad)
    e = jnp.arange(width)
    e = jnp.where(e < tq + pad, e, e - width)
    rel = jnp.clip(pad - e, -(CHUNK - 1), C_REL_MAX) + (CHUNK - 1)
    return rel_bias.astype(F32)[:, None, rel]


def _row(v):
    return v.astype(F32).reshape(1, -1)


def _pad_lanes(v, width):
    return jnp.pad(v, ((0, 0),) * (v.ndim - 1) + ((0, width - v.shape[-1]),))


def _layer(x2, layer_idx, batch, seq, rope_a, rope_b, g_pre, w_main, w_t_b, w_t_c, w_kr, a_g_cq, a_g_ckv, a_w_uq,
           a_w_ukv, a_g_q, a_g_k, b_g_q, b_g_k, b_lam, b_g_sub, c_g_q, c_g_k, c_rel_bias, w_branch, w_out):
    d = x2.shape[1]
    bw = w_branch.shape[1]
    q_rank, kv_rank = a_w_uq.shape[0], a_w_ukv.shape[0]
    a_heads, b_heads, c_heads = bw // A_V, bw // B_DV, bw // C_DH

    o_az, o_bk, o_bz, o_ck, o_cz, o_g = (n * bw for n in range(6))
    o_acq = o_g + N_BRANCH * d
    o_ackv = o_acq + q_rank

    h, h_t = _rmsnorm(x2, _row(g_pre))
    u = _matmul(h, w_main, BF16, 1024, MAIN_COLUMN_TILE, "in_proj")

    hw = 2 * LANES
    t_a = min(512, seq)
    w_uq = _pad_lanes(a_w_uq.reshape(q_rank, a_heads, A_NOPE + A_ROPE), hw).reshape(q_rank, a_heads * hw)
    w_ukv = a_w_ukv.reshape(kv_rank, a_heads, A_NOPE + A_V)
    w_k = w_ukv[:, :, :A_NOPE].reshape(kv_rank, a_heads * A_NOPE).astype(BF16)
    w_vt = w_ukv[:, :, A_NOPE:].reshape(kv_rank, a_heads * A_V).T.astype(BF16)
    g_q = jnp.concatenate([_row(a_g_q[:A_NOPE]), _pad_lanes(_row(a_g_q[A_NOPE:]), LANES)], axis=1)
    g_k = jnp.concatenate([_row(a_g_k[:A_NOPE]), _pad_lanes(_row(a_g_k[A_NOPE:]), LANES)], axis=1)
    qa, ka, vta = _prep_a(u, h, w_kr, o_acq, o_ackv, _row(a_g_cq), _row(a_g_ckv), w_uq.T.astype(BF16), w_k, w_vt,
                          g_q.T, g_k, rope_a, rope_a.T, batch, seq, a_heads, t_a)
    o_a = _attn_a(qa, ka, vta, u, o_az, batch, seq, a_heads, t_a)

    tq_b, tk_b = min(256, seq), min(512, seq)
    qb, vtb = _proj_t_b(w_t_b, h_t, jnp.tile(_row(b_g_q), (1, 2)).T, rope_b.T, batch, seq, b_heads, tq_b, tk_b)
    kb = _prep_b(u, o_bk, jnp.tile(_row(b_g_k), (1, 2)), rope_b, seq, b_heads, tk_b)
    lam_init = 0.8 - 0.6 * math.exp(-0.3 * layer_idx)
    o_b = _attn_b(qb, kb, vtb, u, o_bz, b_lam.astype(F32), b_g_sub.astype(F32).reshape(B_DV, 1), lam_init,
                  batch, seq, b_heads, tq_b, tk_b)

    pad = C_LEFT_CHUNKS * CHUNK
    tq_c = min(256, seq)
    qc, vtc = _proj_t_c(w_t_c, h_t, _row(c_g_q).T, batch, seq, c_heads, pad, tq_c)
    kc = _prep_c(u, o_ck, _row(c_g_k), batch, seq, c_heads, pad)
    o_c = _attn_c(qc, kc, vtc, u, o_cz, _band_rel_rows(c_rel_bias, tq_c, pad), batch, seq, c_heads, tq_c, pad)

    y = _merge(o_a, o_b, o_c, w_branch.astype(BF16), u, o_g, 1024, 1024)
    return _matmul_residual(y, w_out.astype(BF16), x2, 1024, 1024)


def kernel(x, g_pre, w_in, a_g_cq, a_g_ckv, a_w_uq, a_w_ukv, a_g_q, a_g_k, b_g_q, b_g_k, b_lam, b_g_sub,
           c_g_q, c_g_k, c_rel_bias, w_branch, w_out):
    batch, seq, d = x.shape
    rope_a = _rope_table(seq, A_ROPE, A_ROPE_THETA, LANES)
    rope_b = _rope_table(seq, B_ROT, B_ROPE_THETA, B_DK)
    x2 = x.reshape(batch * seq, d)
    q_rank, kv_rank, bw = a_w_uq.shape[1], a_w_ukv.shape[1], w_branch.shape[2]
    n_used = 5 * bw + N_BRANCH * d + q_rank + kv_rank
    n_main = -(-n_used // MAIN_COLUMN_TILE) * MAIN_COLUMN_TILE
    w_in_t = jnp.swapaxes(w_in, 1, 2)
    for l in range(g_pre.shape[0]):
        w_main, w_t_b, w_t_c, w_kr = _weights(w_in_t, l, q_rank, kv_rank, bw, N_BRANCH * d, n_main)
        x2 = _layer(x2, l, batch, seq, rope_a, rope_b, g_pre[l], w_main, w_t_b, w_t_c, w_kr, a_g_cq[l],
                    a_g_ckv[l], a_w_uq[l], a_w_ukv[l], a_g_q[l], a_g_k[l], b_g_q[l], b_g_k[l], b_lam[l],
                    b_g_sub[l], c_g_q[l], c_g_k[l], c_rel_bias[l], w_branch[l], w_out[l])
    return x2.reshape(batch, seq, d)
```

```python
import functools
import math

import jax
import jax.numpy as jnp
from jax import lax
from jax.experimental import pallas as pl
from jax.experimental.pallas import tpu as pltpu

F32 = jnp.float32
BF16 = jnp.bfloat16

EPS = 1e-6
CHUNK = 64
MASKED = -1e30
LOG2E = math.log2(math.e)

A_NOPE, A_ROPE, A_V = 128, 64, 128
A_ROPE_THETA = 10000.0
B_DK, B_DV = 64, 128
B_ROT = B_DK // 4
B_ROPE_THETA = 500000.0
C_DH = 128
C_LEFT_CHUNKS = 8
C_REL_MAX = 128
N_BRANCH = 3

PROJ_HEAD_GROUP = 2
N_BUF = 3
MAIN_COLUMN_TILE = 2048
STEPS_PER_ITERATION = 24
LANES = 128
V7X_VMEM_BUDGET = 56 * 2**20


def _params(semantics, block_bytes, temp_bytes=0):
    need = 2 * block_bytes + temp_bytes + (4 << 20)
    return pltpu.CompilerParams(
        dimension_semantics=semantics,
        vmem_limit_bytes=int(min(max(need, 16 << 20), V7X_VMEM_BUDGET)),
    )


def _sigmoid(z):
    return 1.0 / (1.0 + jnp.exp(-z))


def _rms(x, g, n):
    ss = jnp.sum(x * x, axis=-1, keepdims=True) * (1.0 / n)
    return x * lax.rsqrt(ss + EPS) * g


def _rmsnorm_kernel(x_ref, g_ref, o_ref, ot_ref):
    x = x_ref[...]
    y = _rms(x, g_ref[...], x.shape[-1])
    o_ref[...] = y.astype(o_ref.dtype)
    ot_ref[...] = y.T.astype(ot_ref.dtype)


def _rmsnorm(x, g, tm=512):
    t, d = x.shape
    return pl.pallas_call(
        _rmsnorm_kernel,
        out_shape=(jax.ShapeDtypeStruct((t, d), BF16), jax.ShapeDtypeStruct((d, t), BF16)),
        grid=(t // tm,),
        in_specs=[pl.BlockSpec((tm, d), lambda i: (i, 0)), pl.BlockSpec((1, d), lambda i: (0, 0))],
        out_specs=(pl.BlockSpec((tm, d), lambda i: (i, 0)), pl.BlockSpec((d, tm), lambda i: (0, i))),
        compiler_params=_params(("parallel",), tm * d * 8, tm * d * 12),
        name="pre_rmsnorm",
    )(x, g)


def _mm_kernel(a_ref, b_ref, o_ref):
    o_ref[...] = jnp.dot(a_ref[...], b_ref[...], preferred_element_type=F32).astype(o_ref.dtype)


def _matmul(a, b, out_dtype, tm, tn, name):
    m, k = a.shape
    n = b.shape[1]
    blocks = tm * k * 2 + k * tn * 2 + tm * tn * jnp.dtype(out_dtype).itemsize
    return pl.pallas_call(
        _mm_kernel,
        out_shape=jax.ShapeDtypeStruct((m, n), out_dtype),
        grid=(m // tm, n // tn),
        in_specs=[pl.BlockSpec((tm, k), lambda i, j: (i, 0)), pl.BlockSpec((k, tn), lambda i, j: (0, j))],
        out_specs=pl.BlockSpec((tm, tn), lambda i, j: (i, j)),
        compiler_params=_params(("parallel", "parallel"), blocks, tm * tn * 4),
        name=name,
    )(a, b)


def _mm_res_kernel(a_ref, b_ref, x_ref, o_ref):
    o_ref[...] = x_ref[...] + jnp.dot(a_ref[...], b_ref[...], preferred_element_type=F32)


def _matmul_residual(a, b, x, tm, tn):
    m, k = a.shape
    n = b.shape[1]
    blocks = tm * k * 2 + k * tn * 2 + 2 * tm * tn * 4
    return pl.pallas_call(
        _mm_res_kernel,
        out_shape=jax.ShapeDtypeStruct((m, n), F32),
        grid=(m // tm, n // tn),
        in_specs=[
            pl.BlockSpec((tm, k), lambda i, j: (i, 0)),
            pl.BlockSpec((k, tn), lambda i, j: (0, j)),
            pl.BlockSpec((tm, tn), lambda i, j: (i, j)),
        ],
        out_specs=pl.BlockSpec((tm, tn), lambda i, j: (i, j)),
        compiler_params=_params(("parallel", "parallel"), blocks, tm * tn * 4),
        name="out_proj_residual",
    )(a, b, x)


def _weights_kernel(wt_ref, main_ref, tb_ref, tc_ref, kr_ref, *, q_rank, kv_rank, bw, n_gate):
    s_kr = q_rank + kv_rank
    s_az = s_kr + A_ROPE

    def piece(n):
        return wt_ref[0, s_az + n * bw:s_az + (n + 1) * bw, :]

    for slot, n in enumerate((0, 2, 4, 6, 8)):
        main_ref[:, slot * bw:(slot + 1) * bw] = piece(n).T.astype(BF16)
    o_g = 5 * bw
    for g in range(n_gate // bw):
        main_ref[:, o_g + g * bw:o_g + (g + 1) * bw] = piece(9 + g).T.astype(BF16)
    main_ref[:, o_g + n_gate:o_g + n_gate + s_kr] = wt_ref[0, :s_kr, :].T.astype(BF16)
    n_pad = main_ref.shape[1] - (o_g + n_gate + s_kr)
    main_ref[:, o_g + n_gate + s_kr:] = jnp.zeros((main_ref.shape[0], n_pad), BF16)
    kr_ref[:, :A_ROPE] = wt_ref[0, s_kr:s_az, :].T.astype(BF16)
    kr_ref[:, A_ROPE:] = jnp.zeros((kr_ref.shape[0], LANES - A_ROPE), BF16)
    tb_ref[:bw, :] = piece(1).astype(BF16)
    tb_ref[bw:, :] = piece(3).astype(BF16)
    tc_ref[:bw, :] = piece(5).astype(BF16)
    tc_ref[bw:, :] = piece(7).astype(BF16)


def _weights(w_in_t, layer, q_rank, kv_rank, bw, n_gate, n_main, cols=128):
    _, d_in, d = w_in_t.shape
    blocks = cols * d_in * 4 + cols * (n_main + LANES + 4 * bw) * 2
    return pl.pallas_call(
        functools.partial(_weights_kernel, q_rank=q_rank, kv_rank=kv_rank, bw=bw, n_gate=n_gate),
        out_shape=(jax.ShapeDtypeStruct((d, n_main), BF16),
                   jax.ShapeDtypeStruct((2 * bw, d), BF16),
                   jax.ShapeDtypeStruct((2 * bw, d), BF16),
                   jax.ShapeDtypeStruct((d, LANES), BF16)),
        grid=(d // cols,),
        in_specs=[pl.BlockSpec((1, d_in, cols), lambda r: (layer, 0, r))],
        out_specs=(pl.BlockSpec((cols, n_main), lambda r: (r, 0)),
                   pl.BlockSpec((2 * bw, cols), lambda r: (0, r)),
                   pl.BlockSpec((2 * bw, cols), lambda r: (0, r)),
                   pl.BlockSpec((cols, LANES), lambda r: (r, 0))),
        compiler_params=_params(("parallel",), blocks, 4 * cols * bw * 4),
        name="weights_prep",
    )(w_in_t)


def _merge_kernel(oa_ref, ob_ref, oc_ref, w_ref, ga_ref, gb_ref, gc_ref, y_ref):
    y = None
    for n, (o_ref, g_ref) in enumerate(((oa_ref, ga_ref), (ob_ref, gb_ref), (oc_ref, gc_ref))):
        gate = _sigmoid(g_ref[...].astype(F32))
        term = gate * jnp.dot(o_ref[...], w_ref[n], preferred_element_type=F32)
        y = term if y is None else y + term
    y_ref[...] = y.astype(y_ref.dtype)


def _merge(o_a, o_b, o_c, w_branch, u, gate_off, tm, tn):
    t, bw = o_a.shape
    d = w_branch.shape[2]
    nj = d // tn
    g0 = gate_off // tn
    branch_spec = pl.BlockSpec((tm, bw), lambda i, j: (i, 0))
    gate_specs = [pl.BlockSpec((tm, tn), functools.partial(lambda i, j, n: (i, g0 + n * nj + j), n=n))
                  for n in range(N_BRANCH)]
    blocks = 3 * tm * bw * 2 + N_BRANCH * bw * tn * 2 + 3 * tm * tn * 2 + tm * tn * 2
    return pl.pallas_call(
        _merge_kernel,
        out_shape=jax.ShapeDtypeStruct((t, d), BF16),
        grid=(t // tm, nj),
        in_specs=[branch_spec, branch_spec, branch_spec,
                  pl.BlockSpec((N_BRANCH, bw, tn), lambda i, j: (0, 0, j))] + gate_specs,
        out_specs=pl.BlockSpec((tm, tn), lambda i, j: (i, j)),
        compiler_params=_params(("parallel", "parallel"), blocks, 3 * tm * tn * 4),
        name="gated_merge",
    )(o_a, o_b, o_c, w_branch, u, u, u)


def _chunk_indicator(pos0, rows, first_lane):
    chunk = (pos0 + lax.broadcasted_iota(jnp.int32, (rows, LANES), 0)) // CHUNK
    lane = lax.broadcasted_iota(jnp.int32, (rows, LANES), 1)
    return jnp.where(lane - first_lane == chunk, 1.0, 0.0)


def _chunk_mask_rows(pos0, cols, first_row):
    chunk = (pos0 + lax.broadcasted_iota(jnp.int32, (LANES, cols), 1)) // CHUNK
    row = lax.broadcasted_iota(jnp.int32, (LANES, cols), 0)
    return jnp.where((row >= first_row) & (row - first_row > chunk), MASKED, 0.0)


def _flash_loop(qt_ref, k_ref, vt_ref, s_refs, acc_ref, l_ref, *, tk, ratio):
    n_q, _, cols = acc_ref.shape

    def scores(qi, j):
        k = k_ref[pl.ds(pl.multiple_of(j * tk, tk), tk), :]
        return jnp.dot(k, qt_ref[0, 0, qi], preferred_element_type=F32)

    def step(cur, carry):
        nxt = (cur + 1) % N_BUF
        qi, j, m, l = carry
        last = j == qi // ratio
        qi_next = jnp.where(last, jnp.minimum(qi + 1, n_q - 1), qi)
        j_next = jnp.where(last, 0, j + 1)
        s_refs[nxt][...] = scores(qi_next, j_next)
        s = s_refs[cur][...]
        m_new = jnp.maximum(m, jnp.max(s, axis=0, keepdims=True))
        alpha = jnp.exp2(m - m_new)
        p = jnp.exp2(s - m_new)
        pb = p.astype(BF16)
        l_new = alpha * l + jnp.sum(p, axis=0, keepdims=True)
        half_cols = cols // 2
        for c in range(2):
            cs = slice(c * half_cols, (c + 1) * half_cols)
            acc = alpha[:, cs] * acc_ref[qi, :, cs] + jnp.dot(vt_ref[0, 0, j], pb[:, cs], preferred_element_type=F32)
            acc_ref[qi, :, cs] = acc
        l_ref[qi] = l_new
        return qi_next, j_next, jnp.where(last, MASKED, m_new), l_new

    def steps(count, carry):
        for i in range(count):
            carry = step(i % N_BUF, carry)
        return carry

    zero = jnp.int32(0)
    s_refs[0][...] = scores(zero, zero)
    row = jnp.zeros((1, cols), F32)
    acc_ref[...] = jnp.zeros(acc_ref.shape, F32)
    carry = (zero, zero, row + MASKED, row)
    n_steps = sum(qi // ratio + 1 for qi in range(n_q))
    carry = lax.fori_loop(0, n_steps // STEPS_PER_ITERATION, lambda _, c: steps(STEPS_PER_ITERATION, c), carry)
    steps(n_steps % STEPS_PER_ITERATION, carry)


def _rotate_rows(y, half, rope_t):
    cos, sin_lo, sin_hi = rope_t[:LANES], rope_t[LANES:2 * LANES], rope_t[2 * LANES:]
    up = jnp.concatenate([y[half:], y[:half]], axis=0)
    down = jnp.concatenate([y[-half:], y[:-half]], axis=0)
    return y * cos + up * sin_lo + down * sin_hi


def _silu_gate_store(o_ref, z_ref, rows, y):
    z = z_ref[rows, :].astype(F32)
    o_ref[rows, :] = (y * (z * _sigmoid(z))).astype(o_ref.dtype)


def _prep_a_kernel(cq_ref, ckv_ref, h_ref, wkr_ref, gcq_ref, gckv_ref, wuqt_ref, wk_ref, wvt_ref, gq_ref, gk_ref,
                   rope_ref, ropet_ref, qt_ref, k_ref, vt_ref, *, heads, scale, n_seq):
    rows = cq_ref.shape[0]
    pos0 = (pl.program_id(0) % n_seq) * rows
    rope = rope_ref[...]
    rope_t = ropet_ref[...]
    cos, sin_lo, sin_hi = rope[:, :LANES], rope[:, LANES:2 * LANES], rope[:, 2 * LANES:]

    def rotate(y):
        return y * cos + pltpu.roll(y, LANES - A_ROPE // 2, 1) * sin_lo + pltpu.roll(y, A_ROPE // 2, 1) * sin_hi

    cq = cq_ref[...].astype(F32)
    cq = _rms(cq, gcq_ref[...], cq.shape[-1])
    ckv = ckv_ref[...].astype(F32)
    ckv = _rms(ckv, gckv_ref[...], ckv.shape[-1])
    qat = jnp.dot(wuqt_ref[...], cq.T.astype(BF16), preferred_element_type=F32)
    vat = jnp.dot(wvt_ref[...], ckv.T.astype(BF16), preferred_element_type=F32)
    kna = jnp.dot(ckv.astype(BF16), wk_ref[...], preferred_element_type=F32)
    gq = gq_ref[...]
    gk = gk_ref[...]
    kr = jnp.dot(h_ref[...], wkr_ref[...], preferred_element_type=F32)
    k_rope = (rotate(_rms(kr, gk[:, LANES:], A_ROPE)) + _chunk_indicator(pos0, rows, A_ROPE)).astype(BF16)
    mask_rows = _chunk_mask_rows(pos0, rows, A_ROPE)
    hw = 2 * LANES
    for h in range(heads):
        xn = qat[h * hw:h * hw + LANES]
        q_nope = xn * lax.rsqrt(jnp.sum(xn * xn, axis=0, keepdims=True) * (1.0 / A_NOPE) + EPS) * gq[:LANES]
        xr = qat[h * hw + LANES:(h + 1) * hw]
        q_rope = xr * lax.rsqrt(jnp.sum(xr * xr, axis=0, keepdims=True) * (1.0 / A_ROPE) + EPS) * gq[LANES:]
        q_rope = _rotate_rows(q_rope, A_ROPE // 2, rope_t)
        qt_ref[0, h, 0, :LANES, :] = (q_nope * scale).astype(BF16)
        qt_ref[0, h, 0, LANES:, :] = (q_rope * scale + mask_rows).astype(BF16)
        k_nope = _rms(kna[:, h * LANES:(h + 1) * LANES], gk[:, :LANES], A_NOPE)
        k_ref[:, h * hw:h * hw + LANES] = k_nope.astype(BF16)
        k_ref[:, h * hw + LANES:(h + 1) * hw] = k_rope
        vt_ref[0, h, 0] = vat[h * LANES:(h + 1) * LANES].astype(BF16)


def _prep_a(u, h, w_kr, off_cq, off_ckv, g_cq, g_ckv, w_uqt, w_k, w_vt, g_q, g_k, rope, rope_t, batch, seq, heads,
            t_rows):
    t = u.shape[0]
    q_rank, kv_rank = w_uqt.shape[1], w_k.shape[0]
    hw = 2 * LANES
    n_seq = seq // t_rows
    assert seq // CHUNK <= LANES - A_ROPE
    scale = (A_NOPE + A_ROPE) ** -0.5 * LOG2E
    row = lambda i: (i, 0)
    const = lambda i: (0, 0)
    tile = lambda i: (i // n_seq, 0, i % n_seq, 0, 0)
    d = h.shape[1]
    blocks = (t_rows * (q_rank + kv_rank + d) * 2 + d * LANES * 2 + (q_rank * hw + 2 * kv_rank * LANES) * heads * 2
              + 2 * t_rows * 3 * LANES * 4 + t_rows * heads * (2 * hw + LANES) * 2)
    return pl.pallas_call(
        functools.partial(_prep_a_kernel, heads=heads, scale=scale, n_seq=n_seq),
        out_shape=(jax.ShapeDtypeStruct((batch, heads, n_seq, hw, t_rows), BF16),
                   jax.ShapeDtypeStruct((t, heads * hw), BF16),
                   jax.ShapeDtypeStruct((batch, heads, n_seq, A_V, t_rows), BF16)),
        grid=(t // t_rows,),
        in_specs=[
            pl.BlockSpec((t_rows, q_rank), lambda i: (i, off_cq // q_rank)),
            pl.BlockSpec((t_rows, kv_rank), lambda i: (i, off_ckv // kv_rank)),
            pl.BlockSpec((t_rows, d), row),
            pl.BlockSpec((d, LANES), const),
            pl.BlockSpec((1, q_rank), const),
            pl.BlockSpec((1, kv_rank), const),
            pl.BlockSpec((heads * hw, q_rank), const),
            pl.BlockSpec((kv_rank, heads * LANES), const),
            pl.BlockSpec((heads * LANES, kv_rank), const),
            pl.BlockSpec((hw, 1), const),
            pl.BlockSpec((1, hw), const),
            pl.BlockSpec((t_rows, 3 * LANES), lambda i: (i % n_seq, 0)),
            pl.BlockSpec((3 * LANES, t_rows), lambda i: (0, i % n_seq)),
        ],
        out_specs=(pl.BlockSpec((1, heads, 1, hw, t_rows), tile),
                   pl.BlockSpec((t_rows, heads * hw), row),
                   pl.BlockSpec((1, heads, 1, A_V, t_rows), tile)),
        compiler_params=_params(("parallel",), blocks, 4 * t_rows * heads * hw * 4),
        name="mla_prep",
    )(u, u, h, w_kr, g_cq, g_ckv, w_uqt, w_k, w_vt, g_q, g_k, rope, rope_t)


def _attn_a_kernel(qt_ref, k_ref, vt_ref, z_ref, o_ref, *scratch, t):
    s_refs, (acc_ref, l_ref) = scratch[:N_BUF], scratch[N_BUF:]
    _flash_loop(qt_ref, k_ref, vt_ref, s_refs, acc_ref, l_ref, tk=t, ratio=1)
    for qi in range(acc_ref.shape[0]):
        _silu_gate_store(o_ref, z_ref, slice(qi * t, (qi + 1) * t), (acc_ref[qi] * (1.0 / l_ref[qi])).T)


def _attn_a(qt, k, vt, u, z_off, batch, seq, heads, t):
    n_tiles = seq // t
    dk = k.shape[1] // heads
    zb = z_off // LANES
    seq_head = lambda b, h: (b, h)
    tiles = lambda b, h: (b, h, 0, 0, 0)
    blocks = 2 * seq * dk * 2 + 3 * seq * LANES * 2
    scratch = N_BUF * t * t * 4 + n_tiles * A_V * t * 4
    return pl.pallas_call(
        functools.partial(_attn_a_kernel, t=t),
        out_shape=jax.ShapeDtypeStruct((batch * seq, heads * LANES), BF16),
        grid=(batch, heads),
        in_specs=[
            pl.BlockSpec((1, 1, n_tiles, dk, t), tiles),
            pl.BlockSpec((seq, dk), seq_head),
            pl.BlockSpec((1, 1, n_tiles, A_V, t), tiles),
            pl.BlockSpec((seq, LANES), lambda b, h: (b, zb + h)),
        ],
        out_specs=pl.BlockSpec((seq, LANES), seq_head),
        scratch_shapes=([pltpu.VMEM((t, t), F32)] * N_BUF
                        + [pltpu.VMEM((n_tiles, A_V, t), F32), pltpu.VMEM((n_tiles, 1, t), F32)]),
        compiler_params=_params(("parallel", "parallel"), blocks, scratch + 6 * t * t * 4),
        name="mla_attention",
    )(qt, k, vt, u)


def _proj_t_b_kernel(wt_ref, ht_ref, gq_ref, ropet_ref, qt_ref, vt_ref, *, heads, scale, n_seq, tq):
    cols = ht_ref.shape[1]
    pos0 = (pl.program_id(0) % n_seq) * cols
    ht = ht_ref[...]
    group = PROJ_HEAD_GROUP * LANES
    rope_t = ropet_ref[...]
    gq = gq_ref[...]
    first_map = lax.broadcasted_iota(jnp.int32, (LANES, 1), 0) < B_DK
    mask_rows = _chunk_mask_rows(pos0, cols, 0).astype(BF16)
    for h in range(heads):
        if h % PROJ_HEAD_GROUP == 0:
            res_q = jnp.dot(wt_ref[h * LANES:h * LANES + group], ht, preferred_element_type=F32)
            res_v = jnp.dot(wt_ref[(heads + h) * LANES:(heads + h) * LANES + group], ht,
                            preferred_element_type=F32)
        g = h % PROJ_HEAD_GROUP
        x = res_q[g * LANES:(g + 1) * LANES]
        xsq = x * x
        ss_lo = jnp.sum(xsq[:B_DK], axis=0, keepdims=True)
        ss_hi = jnp.sum(xsq[B_DK:], axis=0, keepdims=True)
        y = x * lax.rsqrt(jnp.where(first_map, ss_lo, ss_hi) * (1.0 / B_DK) + EPS) * gq
        q = _rotate_rows(y, B_ROT // 2, rope_t) * scale
        q_lo = jnp.where(first_map, q, 0.0).astype(BF16)
        q_hi = jnp.where(first_map, 0.0, q).astype(BF16)
        for a in range(cols // tq):
            cs = slice(a * tq, (a + 1) * tq)
            qt_ref[0, h, a, :LANES, :tq] = q_lo[:, cs]
            qt_ref[0, h, a, :LANES, tq:] = q_hi[:, cs]
            qt_ref[0, h, a, LANES:, :tq] = mask_rows[:, cs]
            qt_ref[0, h, a, LANES:, tq:] = mask_rows[:, cs]
        vt_ref[0, h, 0] = res_v[g * LANES:(g + 1) * LANES].astype(BF16)


def _proj_t_b(w_t, h_t, g_q, rope_t, batch, seq, heads, tq, tk):
    d, t = h_t.shape
    n_seq = seq // tk
    assert seq // CHUNK <= LANES
    tile = lambda i: (i // n_seq, 0, i % n_seq, 0, 0)
    blocks = w_t.size * 2 + d * tk * 2 + 3 * LANES * tk * 4 + heads * tk * (4 * LANES + B_DV) * 2
    return pl.pallas_call(
        functools.partial(_proj_t_b_kernel, heads=heads, scale=B_DK ** -0.5 * LOG2E, n_seq=n_seq, tq=tq),
        out_shape=(jax.ShapeDtypeStruct((batch, heads, seq // tq, 2 * LANES, 2 * tq), BF16),
                   jax.ShapeDtypeStruct((batch, heads, n_seq, B_DV, tk), BF16)),
        grid=(t // tk,),
        in_specs=[
            pl.BlockSpec(w_t.shape, lambda i: (0, 0)),
            pl.BlockSpec((d, tk), lambda i: (0, i)),
            pl.BlockSpec((LANES, 1), lambda i: (0, 0)),
            pl.BlockSpec((3 * LANES, tk), lambda i: (0, i % n_seq)),
        ],
        out_specs=(pl.BlockSpec((1, heads, tk // tq, 2 * LANES, 2 * tq), tile),
                   pl.BlockSpec((1, heads, 1, B_DV, tk), tile)),
        compiler_params=_params(("parallel",), blocks, 3 * w_t.shape[0] * tk * 4),
        name="diff_proj_qv",
    )(w_t, h_t, g_q, rope_t)


def _prep_b_kernel(k_ref, gk_ref, rope_ref, ko_ref, *, heads, n_seq):
    rows = k_ref.shape[0]
    pos0 = (pl.program_id(0) % n_seq) * rows
    rope = rope_ref[...]
    cos, sin_lo, sin_hi = rope[:, :LANES], rope[:, LANES:2 * LANES], rope[:, 2 * LANES:]
    first_map = lax.broadcasted_iota(jnp.int32, (1, LANES), 1) < B_DK
    gk = gk_ref[...]
    indicator = _chunk_indicator(pos0, rows, 0).astype(BF16)
    for h in range(heads):
        x = k_ref[:, h * LANES:(h + 1) * LANES].astype(F32)
        xsq = x * x
        ss_lo = jnp.sum(jnp.where(first_map, xsq, 0.0), axis=-1, keepdims=True)
        ss_hi = jnp.sum(jnp.where(first_map, 0.0, xsq), axis=-1, keepdims=True)
        y = x * lax.rsqrt(jnp.where(first_map, ss_lo, ss_hi) * (1.0 / B_DK) + EPS) * gk
        y = y * cos + pltpu.roll(y, LANES - B_ROT // 2, 1) * sin_lo + pltpu.roll(y, B_ROT // 2, 1) * sin_hi
        ko_ref[:, 2 * h * LANES:(2 * h + 1) * LANES] = y.astype(BF16)
        ko_ref[:, (2 * h + 1) * LANES:(2 * h + 2) * LANES] = indicator


def _prep_b(u, off_k, g_k, rope, seq, heads, t_rows):
    t = u.shape[0]
    w = heads * LANES
    n_seq = seq // t_rows
    return pl.pallas_call(
        functools.partial(_prep_b_kernel, heads=heads, n_seq=n_seq),
        out_shape=jax.ShapeDtypeStruct((t, 2 * w), BF16),
        grid=(t // t_rows,),
        in_specs=[
            pl.BlockSpec((t_rows, w), lambda i: (i, off_k // w)),
            pl.BlockSpec((1, LANES), lambda i: (0, 0)),
            pl.BlockSpec((t_rows, 3 * LANES), lambda i: (i % n_seq, 0)),
        ],
        out_specs=pl.BlockSpec((t_rows, 2 * w), lambda i: (i, 0)),
        compiler_params=_params(("parallel",), 3 * t_rows * w * 2 + t_rows * 3 * LANES * 4, 8 * t_rows * LANES * 4),
        name="diff_prep_k",
    )(u, g_k, rope)


def _attn_b_kernel(qt_ref, k_ref, vt_ref, z_ref, lam_ref, gsub_ref, o_ref, *scratch, tq, tk, lam_init):
    s_refs, (acc_ref, l_ref) = scratch[:N_BUF], scratch[N_BUF:]
    _flash_loop(qt_ref, k_ref, vt_ref, s_refs, acc_ref, l_ref, tk=tk, ratio=tk // tq)
    lf = lam_ref[...]
    lam = (jnp.exp(jnp.sum(lf[0:1] * lf[1:2], axis=-1, keepdims=True))
           - jnp.exp(jnp.sum(lf[2:3] * lf[3:4], axis=-1, keepdims=True)) + lam_init)
    gsub = gsub_ref[...]
    for qi in range(acc_ref.shape[0]):
        o = acc_ref[qi] * (1.0 / l_ref[qi])
        a = o[:, :tq] - lam * o[:, tq:]
        ss = jnp.sum(a * a, axis=0, keepdims=True) * (1.0 / B_DV)
        y = a * lax.rsqrt(ss + EPS) * gsub * (1.0 - lam_init)
        _silu_gate_store(o_ref, z_ref, slice(qi * tq, (qi + 1) * tq), y.T)


def _attn_b(qt, k, vt, u, z_off, lam, g_sub, lam_init, batch, seq, heads, tq, tk):
    n_q, n_k = seq // tq, seq // tk
    zb = z_off // LANES
    cols = 2 * tq
    seq_head = lambda b, h: (b, h)
    tiles = lambda b, h: (b, h, 0, 0, 0)
    blocks = 2 * seq * 2 * LANES * 2 + seq * 2 * LANES * 2 + 3 * seq * LANES * 2
    scratch = N_BUF * tk * cols * 4 + n_q * B_DV * cols * 4
    return pl.pallas_call(
        functools.partial(_attn_b_kernel, tq=tq, tk=tk, lam_init=lam_init),
        out_shape=jax.ShapeDtypeStruct((batch * seq, heads * LANES), BF16),
        grid=(batch, heads),
        in_specs=[
            pl.BlockSpec((1, 1, n_q, 2 * LANES, cols), tiles),
            pl.BlockSpec((seq, 2 * LANES), seq_head),
            pl.BlockSpec((1, 1, n_k, B_DV, tk), tiles),
            pl.BlockSpec((seq, LANES), lambda b, h: (b, zb + h)),
            pl.BlockSpec(lam.shape, lambda b, h: (0, 0)),
            pl.BlockSpec((B_DV, 1), lambda b, h: (0, 0)),
        ],
        out_specs=pl.BlockSpec((seq, LANES), seq_head),
        scratch_shapes=([pltpu.VMEM((tk, cols), F32)] * N_BUF
                        + [pltpu.VMEM((n_q, B_DV, cols), F32), pltpu.VMEM((n_q, 1, cols), F32)]),
        compiler_params=_params(("parallel", "parallel"), blocks, scratch + 6 * tk * cols * 4),
        name="diff_attention",
    )(qt, k, vt, u, lam, g_sub)


C_PAD_SLOT = 64


def _proj_t_c_kernel(wt_ref, ht_ref, gq_ref, qt_ref, vt_ref, *, heads, scale, tq):
    cols = ht_ref.shape[1]
    r = pl.program_id(1)
    is_pad = r == 0
    pos0 = jnp.maximum(r - 1, 0) * cols
    ht = ht_ref[...]
    group = PROJ_HEAD_GROUP * LANES
    gq = gq_ref[...]
    q_chunk = (pos0 + lax.broadcasted_iota(jnp.int32, (LANES, cols), 1)) // CHUNK
    slot = lax.broadcasted_iota(jnp.int32, (LANES, cols), 0)
    out_of_band = ((slot < C_PAD_SLOT) & ((slot > q_chunk) | (slot < q_chunk - C_LEFT_CHUNKS))) | (slot == C_PAD_SLOT)
    mask_rows = jnp.where(out_of_band, MASKED, 0.0).astype(BF16)
    for h in range(heads):
        if h % PROJ_HEAD_GROUP == 0:
            res_q = jnp.dot(wt_ref[h * LANES:h * LANES + group], ht, preferred_element_type=F32)
            res_v = jnp.dot(wt_ref[(heads + h) * LANES:(heads + h) * LANES + group], ht,
                            preferred_element_type=F32)
        g = h % PROJ_HEAD_GROUP
        x = res_q[g * LANES:(g + 1) * LANES]
        q = x * lax.rsqrt(jnp.sum(x * x, axis=0, keepdims=True) * (1.0 / C_DH) + EPS) * gq * scale
        q = q.astype(BF16)
        v = res_v[g * LANES:(g + 1) * LANES].astype(BF16)
        v = jnp.where(is_pad, jnp.zeros_like(v), v)
        for a in range(cols // tq):
            cs = slice(a * tq, (a + 1) * tq)
            qt_ref[0, h, a, :LANES, :] = q[:, cs]
            qt_ref[0, h, a, LANES:, :] = mask_rows[:, cs]
            vt_ref[0, h, a] = v[:, cs]


def _proj_t_c(w_t, h_t, g_q, batch, seq, heads, pad, tq):
    d, t = h_t.shape
    n_seq = seq // pad
    per = pad // tq
    assert seq // CHUNK <= C_PAD_SLOT
    blocks = w_t.size * 2 + d * pad * 2 + heads * pad * 3 * LANES * 2
    return pl.pallas_call(
        functools.partial(_proj_t_c_kernel, heads=heads, scale=C_DH ** -0.5 * LOG2E, tq=tq),
        out_shape=(jax.ShapeDtypeStruct((batch, heads, seq // tq, 2 * LANES, tq), BF16),
                   jax.ShapeDtypeStruct((batch, heads, (seq + pad) // tq, C_DH, tq), BF16)),
        grid=(batch, n_seq + 1),
        in_specs=[
            pl.BlockSpec(w_t.shape, lambda b, r: (0, 0)),
            pl.BlockSpec((d, pad), lambda b, r: (0, b * n_seq + jnp.maximum(r - 1, 0))),
            pl.BlockSpec((LANES, 1), lambda b, r: (0, 0)),
        ],
        out_specs=(pl.BlockSpec((1, heads, per, 2 * LANES, tq), lambda b, r: (b, 0, jnp.maximum(r - 1, 0), 0, 0)),
                   pl.BlockSpec((1, heads, per, C_DH, tq), lambda b, r: (b, 0, r, 0, 0))),
        compiler_params=_params(("parallel", "arbitrary"), blocks, 3 * w_t.shape[0] * pad * 4),
        name="band_proj_qv",
    )(w_t, h_t, g_q)


def _prep_c_kernel(k_ref, gk_ref, ko_ref, *, heads):
    rows = k_ref.shape[0]
    r = pl.program_id(1)
    is_pad = r == 0
    pos0 = jnp.maximum(r - 1, 0) * rows
    gk = gk_ref[...]
    lane = lax.broadcasted_iota(jnp.int32, (rows, LANES), 1)
    indicator = jnp.where(is_pad, jnp.where(lane == C_PAD_SLOT, 1.0, 0.0), _chunk_indicator(pos0, rows, 0))
    indicator = indicator.astype(BF16)
    for h in range(heads):
        kn = _rms(k_ref[:, h * LANES:(h + 1) * LANES].astype(F32), gk, C_DH).astype(BF16)
        ko_ref[:, 2 * h * LANES:(2 * h + 1) * LANES] = jnp.where(is_pad, jnp.zeros_like(kn), kn)
        ko_ref[:, (2 * h + 1) * LANES:(2 * h + 2) * LANES] = indicator


def _prep_c(u, off_k, g_k, batch, seq, heads, pad):
    w = heads * LANES
    n_seq = seq // pad
    return pl.pallas_call(
        functools.partial(_prep_c_kernel, heads=heads),
        out_shape=jax.ShapeDtypeStruct((batch * (seq + pad), 2 * w), BF16),
        grid=(batch, n_seq + 1),
        in_specs=[
            pl.BlockSpec((pad, w), lambda b, r: (b * n_seq + jnp.maximum(r - 1, 0), off_k // w)),
            pl.BlockSpec((1, LANES), lambda b, r: (0, 0)),
        ],
        out_specs=pl.BlockSpec((pad, 2 * w), lambda b, r: (b * (n_seq + 1) + r, 0)),
        compiler_params=_params(("parallel", "parallel"), 3 * pad * w * 2, 4 * pad * LANES * 4),
        name="band_prep_k",
    )(u, g_k)


def _attn_c_kernel(qt_ref, k_ref, vt_ref, z_ref, rel_ref, o_ref, *scratch, tq, tw):
    n_q = qt_ref.shape[2]
    s_refs, p_refs, (bias_ref,) = scratch[:N_BUF], scratch[N_BUF:2 * N_BUF], scratch[2 * N_BUF:]
    width = rel_ref.shape[-1]
    toeplitz = pltpu.roll(jnp.broadcast_to(rel_ref[0], (tq, width)), 0, 1, stride=1, stride_axis=0)
    bias_ref[...] = (toeplitz[:, :tw] * LOG2E).T

    def scores(i):
        return jnp.dot(k_ref[i * tq:i * tq + tw, :], qt_ref[0, 0, i], preferred_element_type=F32) + bias_ref[...]

    s_refs[0][...] = scores(0)
    for i in range(n_q + 1):
        cur, nxt, prv = i % N_BUF, (i + 1) % N_BUF, (i - 1) % N_BUF
        if i + 1 < n_q:
            s_refs[nxt][...] = scores(i + 1)
        if i < n_q:
            s = s_refs[cur][...]
            p = jnp.exp2(s - jnp.max(s, axis=0, keepdims=True))
            p_refs[cur][...] = p.astype(BF16)
            l = jnp.sum(p, axis=0, keepdims=True)
        if i > 0:
            acc = None
            for a in range(tw // tq):
                part = jnp.dot(vt_ref[0, 0, i - 1 + a], p_refs[prv][a * tq:(a + 1) * tq, :],
                               preferred_element_type=F32)
                acc = part if acc is None else acc + part
            _silu_gate_store(o_ref, z_ref, slice((i - 1) * tq, i * tq), (acc * (1.0 / l_prev)).T)
        l_prev = l


def _attn_c(qt, k, vt, u, z_off, rel_rows, batch, seq, heads, tq, pad):
    n_q = seq // tq
    tw = tq + pad
    zb = z_off // LANES
    tiles = lambda b, h: (b, h, 0, 0, 0)
    blocks = seq * 2 * LANES * 2 + (seq + pad) * 3 * LANES * 2 + 2 * seq * LANES * 2
    return pl.pallas_call(
        functools.partial(_attn_c_kernel, tq=tq, tw=tw),
        out_shape=jax.ShapeDtypeStruct((batch * seq, heads * LANES), BF16),
        grid=(batch, heads),
        in_specs=[
            pl.BlockSpec((1, 1, n_q, 2 * LANES, tq), tiles),
            pl.BlockSpec((seq + pad, 2 * LANES), lambda b, h: (b, h)),
            pl.BlockSpec((1, 1, (seq + pad) // tq, C_DH, tq), tiles),
            pl.BlockSpec((seq, LANES), lambda b, h: (b, zb + h)),
            pl.BlockSpec((1, 1, rel_rows.shape[-1]), lambda b, h: (h, 0, 0)),
        ],
        out_specs=pl.BlockSpec((seq, LANES), lambda b, h: (b, h)),
        scratch_shapes=([pltpu.VMEM((tw, tq), F32)] * N_BUF + [pltpu.VMEM((tw, tq), BF16)] * N_BUF
                        + [pltpu.VMEM((tw, tq), F32)]),
        compiler_params=_params(("parallel", "parallel"), blocks, 12 * tw * tq * 4),
        name="band_attention",
    )(qt, k, vt, u, rel_rows)


def _rope_table(seq, dim, theta, group):
    half = dim // 2
    inv = 1.0 / (jnp.float32(theta) ** (jnp.arange(0, dim, 2, dtype=F32) / dim))
    ang = jnp.arange(seq, dtype=F32)[:, None] * inv[None, :]
    cos, sin = jnp.cos(ang), jnp.sin(ang)
    lane = jnp.arange(LANES) % group
    idx = lane % half
    in_lo = (lane < half)[None, :]
    in_hi = ((lane >= half) & (lane < dim))[None, :]
    c = jnp.where(in_lo | in_hi, cos[:, idx], 1.0 if group < LANES else 0.0)
    s_lo = jnp.where(in_lo, -sin[:, idx], 0.0)
    s_hi = jnp.where(in_hi, sin[:, idx], 0.0)
    return jnp.concatenate([c, s_lo, s_hi], axis=1)


def _band_rel_rows(rel_bias, tq, pad):
    width = pl.next_power_of_2(2 * tq + pad)
    e = jnp.arange(width)
    e = jnp.where(e < tq + pad, e, e - width)
    rel = jnp.clip(pad - e, -(CHUNK - 1), C_REL_MAX) + (CHUNK - 1)
    return rel_bias.astype(F32)[:, None, rel]


def _row(v):
    return v.astype(F32).reshape(1, -1)


def _pad_lanes(v, width):
    return jnp.pad(v, ((0, 0),) * (v.ndim - 1) + ((0, width - v.shape[-1]),))


def _layer(x2, layer_idx, batch, seq, rope_a, rope_b, g_pre, w_main, w_t_b, w_t_c, w_kr, a_g_cq, a_g_ckv, a_w_uq,
           a_w_ukv, a_g_q, a_g_k, b_g_q, b_g_k, b_lam, b_g_sub, c_g_q, c_g_k, c_rel_bias, w_branch, w_out):
    d = x2.shape[1]
    bw = w_branch.shape[1]
    q_rank, kv_rank = a_w_uq.shape[0], a_w_ukv.shape[0]
    a_heads, b_heads, c_heads = bw // A_V, bw // B_DV, bw // C_DH

    o_az, o_bk, o_bz, o_ck, o_cz, o_g = (n * bw for n in range(6))
    o_acq = o_g + N_BRANCH * d
    o_ackv = o_acq + q_rank

    h, h_t = _rmsnorm(x2, _row(g_pre))
    u = _matmul(h, w_main, BF16, 1024, MAIN_COLUMN_TILE, "in_proj")

    hw = 2 * LANES
    t_a = min(512, seq)
    w_uq = _pad_lanes(a_w_uq.reshape(q_rank, a_heads, A_NOPE + A_ROPE), hw).reshape(q_rank, a_heads * hw)
    w_ukv = a_w_ukv.reshape(kv_rank, a_heads, A_NOPE + A_V)
    w_k = w_ukv[:, :, :A_NOPE].reshape(kv_rank, a_heads * A_NOPE).astype(BF16)
    w_vt = w_ukv[:, :, A_NOPE:].reshape(kv_rank, a_heads * A_V).T.astype(BF16)
    g_q = jnp.concatenate([_row(a_g_q[:A_NOPE]), _pad_lanes(_row(a_g_q[A_NOPE:]), LANES)], axis=1)
    g_k = jnp.concatenate([_row(a_g_k[:A_NOPE]), _pad_lanes(_row(a_g_k[A_NOPE:]), LANES)], axis=1)
    qa, ka, vta = _prep_a(u, h, w_kr, o_acq, o_ackv, _row(a_g_cq), _row(a_g_ckv), w_uq.T.astype(BF16), w_k, w_vt,
                          g_q.T, g_k, rope_a, rope_a.T, batch, seq, a_heads, t_a)
    o_a = _attn_a(qa, ka, vta, u, o_az, batch, seq, a_heads, t_a)

    tq_b, tk_b = min(256, seq), min(512, seq)
    qb, vtb = _proj_t_b(w_t_b, h_t, jnp.tile(_row(b_g_q), (1, 2)).T, rope_b.T, batch, seq, b_heads, tq_b, tk_b)
    kb = _prep_b(u, o_bk, jnp.tile(_row(b_g_k), (1, 2)), rope_b, seq, b_heads, tk_b)
    lam_init = 0.8 - 0.6 * math.exp(-0.3 * layer_idx)
    o_b = _attn_b(qb, kb, vtb, u, o_bz, b_lam.astype(F32), b_g_sub.astype(F32).reshape(B_DV, 1), lam_init,
                  batch, seq, b_heads, tq_b, tk_b)

    pad = C_LEFT_CHUNKS * CHUNK
    tq_c = min(256, seq)
    qc, vtc = _proj_t_c(w_t_c, h_t, _row(c_g_q).T, batch, seq, c_heads, pad, tq_c)
    kc = _prep_c(u, o_ck, _row(c_g_k), batch, seq, c_heads, pad)
    o_c = _attn_c(qc, kc, vtc, u, o_cz, _band_rel_rows(c_rel_bias, tq_c, pad), batch, seq, c_heads, tq_c, pad)

    y = _merge(o_a, o_b, o_c, w_branch.astype(BF16), u, o_g, 1024, 1024)
    return _matmul_residual(y, w_out.astype(BF16), x2, 1024, 1024)


def kernel(x, g_pre, w_in, a_g_cq, a_g_ckv, a_w_uq, a_w_ukv, a_g_q, a_g_k, b_g_q, b_g_k, b_lam, b_g_sub,
           c_g_q, c_g_k, c_rel_bias, w_branch, w_out):
    batch, seq, d = x.shape
    rope_a = _rope_table(seq, A_ROPE, A_ROPE_THETA, LANES)
    rope_b = _rope_table(seq, B_ROT, B_ROPE_THETA, B_DK)
    x2 = x.reshape(batch * seq, d)
    q_rank, kv_rank, bw = a_w_uq.shape[1], a_w_ukv.shape[1], w_branch.shape[2]
    n_used = 5 * bw + N_BRANCH * d + q_rank + kv_rank
    n_main = -(-n_used // MAIN_COLUMN_TILE) * MAIN_COLUMN_TILE
    w_in_t = jnp.swapaxes(w_in, 1, 2)
    for l in range(g_pre.shape[0]):
        w_main, w_t_b, w_t_c, w_kr = _weights(w_in_t, l, q_rank, kv_rank, bw, N_BRANCH * d, n_main)
        x2 = _layer(x2, l, batch, seq, rope_a, rope_b, g_pre[l], w_main, w_t_b, w_t_c, w_kr, a_g_cq[l],
                    a_g_ckv[l], a_w_uq[l], a_w_ukv[l], a_g_q[l], a_g_k[l], b_g_q[l], b_g_k[l], b_lam[l],
                    b_g_sub[l], c_g_q[l], c_g_k[l], c_rel_bias[l], w_branch[l], w_out[l])
    return x2.reshape(batch, seq, d)
```

```python
import functools
import math

import jax
import jax.numpy as jnp
from jax import lax
from jax.experimental import pallas as pl
from jax.experimental.pallas import tpu as pltpu

F32 = jnp.float32
BF16 = jnp.bfloat16

EPS = 1e-6
CHUNK = 64
MASKED = -1e30
LOG2E = math.log2(math.e)

A_NOPE, A_ROPE, A_V = 128, 64, 128
A_ROPE_THETA = 10000.0
B_DK, B_DV = 64, 128
B_ROT = B_DK // 4
B_ROPE_THETA = 500000.0
C_DH = 128
C_LEFT_CHUNKS = 8
C_REL_MAX = 128
N_BRANCH = 3

PROJ_HEAD_GROUP = 2
N_BUF = 3
MAIN_COLUMN_TILE = 2048
STEPS_PER_ITERATION = 24
LANES = 128
V7X_VMEM_BUDGET = 56 * 2**20


def _params(semantics, block_bytes, temp_bytes=0):
    need = 2 * block_bytes + temp_bytes + (4 << 20)
    return pltpu.CompilerParams(
        dimension_semantics=semantics,
        vmem_limit_bytes=int(min(max(need, 16 << 20), V7X_VMEM_BUDGET)),
    )


def _sigmoid(z):
    return 1.0 / (1.0 + jnp.exp(-z))


def _rms(x, g, n):
    ss = jnp.sum(x * x, axis=-1, keepdims=True) * (1.0 / n)
    return x * lax.rsqrt(ss + EPS) * g


def _rmsnorm_kernel(x_ref, g_ref, ot_ref):
    x = x_ref[...]
    ot_ref[...] = _rms(x, g_ref[...], x.shape[-1]).T.astype(ot_ref.dtype)


def _rmsnorm(x, g, tm=512):
    t, d = x.shape
    return pl.pallas_call(
        _rmsnorm_kernel,
        out_shape=jax.ShapeDtypeStruct((d, t), BF16),
        grid=(t // tm,),
        in_specs=[pl.BlockSpec((tm, d), lambda i: (i, 0)), pl.BlockSpec((1, d), lambda i: (0, 0))],
        out_specs=pl.BlockSpec((d, tm), lambda i: (0, i)),
        compiler_params=_params(("parallel",), tm * d * 6, tm * d * 12),
        name="pre_rmsnorm",
    )(x, g)


def _mm_t_kernel(at_ref, b_ref, o_ref):
    o_ref[...] = lax.dot_general(at_ref[...], b_ref[...], (((0,), (0,)), ((), ())),
                                 preferred_element_type=F32).astype(o_ref.dtype)


def _matmul_t(a_t, b, out_dtype, tm, tn, name):
    k, m = a_t.shape
    n = b.shape[1]
    blocks = tm * k * 2 + k * tn * 2 + tm * tn * jnp.dtype(out_dtype).itemsize
    return pl.pallas_call(
        _mm_t_kernel,
        out_shape=jax.ShapeDtypeStruct((m, n), out_dtype),
        grid=(m // tm, n // tn),
        in_specs=[pl.BlockSpec((k, tm), lambda i, j: (0, i)), pl.BlockSpec((k, tn), lambda i, j: (0, j))],
        out_specs=pl.BlockSpec((tm, tn), lambda i, j: (i, j)),
        compiler_params=_params(("parallel", "parallel"), blocks, tm * tn * 4),
        name=name,
    )(a_t, b)


def _mm_res_kernel(a_ref, b_ref, x_ref, o_ref):
    o_ref[...] = x_ref[...] + jnp.dot(a_ref[...], b_ref[...], preferred_element_type=F32)


def _matmul_residual(a, b, x, tm, tn):
    m, k = a.shape
    n = b.shape[1]
    blocks = tm * k * 2 + k * tn * 2 + 2 * tm * tn * 4
    return pl.pallas_call(
        _mm_res_kernel,
        out_shape=jax.ShapeDtypeStruct((m, n), F32),
        grid=(m // tm, n // tn),
        in_specs=[
            pl.BlockSpec((tm, k), lambda i, j: (i, 0)),
            pl.BlockSpec((k, tn), lambda i, j: (0, j)),
            pl.BlockSpec((tm, tn), lambda i, j: (i, j)),
        ],
        out_specs=pl.BlockSpec((tm, tn), lambda i, j: (i, j)),
        compiler_params=_params(("parallel", "parallel"), blocks, tm * tn * 4),
        name="out_proj_residual",
    )(a, b, x)


def _weights_kernel(wt_ref, main_ref, tb_ref, tc_ref, kr_ref, *, q_rank, kv_rank, bw, n_gate):
    s_kr = q_rank + kv_rank
    s_az = s_kr + A_ROPE

    def piece(n):
        return wt_ref[0, s_az + n * bw:s_az + (n + 1) * bw, :]

    for slot, n in enumerate((0, 2, 4, 6, 8)):
        main_ref[:, slot * bw:(slot + 1) * bw] = piece(n).T.astype(BF16)
    o_g = 5 * bw
    for g in range(n_gate // bw):
        main_ref[:, o_g + g * bw:o_g + (g + 1) * bw] = piece(9 + g).T.astype(BF16)
    main_ref[:, o_g + n_gate:o_g + n_gate + s_kr] = wt_ref[0, :s_kr, :].T.astype(BF16)
    n_pad = main_ref.shape[1] - (o_g + n_gate + s_kr)
    main_ref[:, o_g + n_gate + s_kr:] = jnp.zeros((main_ref.shape[0], n_pad), BF16)
    kr_ref[:, :A_ROPE] = wt_ref[0, s_kr:s_az, :].T.astype(BF16)
    kr_ref[:, A_ROPE:] = jnp.zeros((kr_ref.shape[0], LANES - A_ROPE), BF16)
    tb_ref[:bw, :] = piece(1).astype(BF16)
    tb_ref[bw:, :] = piece(3).astype(BF16)
    tc_ref[:bw, :] = piece(5).astype(BF16)
    tc_ref[bw:, :] = piece(7).astype(BF16)


def _weights(w_in_t, layer, q_rank, kv_rank, bw, n_gate, n_main, cols=128):
    _, d_in, d = w_in_t.shape
    blocks = cols * d_in * 4 + cols * (n_main + LANES + 4 * bw) * 2
    return pl.pallas_call(
        functools.partial(_weights_kernel, q_rank=q_rank, kv_rank=kv_rank, bw=bw, n_gate=n_gate),
        out_shape=(jax.ShapeDtypeStruct((d, n_main), BF16),
                   jax.ShapeDtypeStruct((2 * bw, d), BF16),
                   jax.ShapeDtypeStruct((2 * bw, d), BF16),
                   jax.ShapeDtypeStruct((d, LANES), BF16)),
        grid=(d // cols,),
        in_specs=[pl.BlockSpec((1, d_in, cols), lambda r: (layer, 0, r))],
        out_specs=(pl.BlockSpec((cols, n_main), lambda r: (r, 0)),
                   pl.BlockSpec((2 * bw, cols), lambda r: (0, r)),
                   pl.BlockSpec((2 * bw, cols), lambda r: (0, r)),
                   pl.BlockSpec((cols, LANES), lambda r: (r, 0))),
        compiler_params=_params(("parallel",), blocks, 4 * cols * bw * 4),
        name="weights_prep",
    )(w_in_t)


def _merge_kernel(oa_ref, ob_ref, oc_ref, w_ref, ga_ref, gb_ref, gc_ref, y_ref):
    y = None
    for n, (o_ref, g_ref) in enumerate(((oa_ref, ga_ref), (ob_ref, gb_ref), (oc_ref, gc_ref))):
        gate = _sigmoid(g_ref[...].astype(F32))
        term = gate * jnp.dot(o_ref[...], w_ref[n], preferred_element_type=F32)
        y = term if y is None else y + term
    y_ref[...] = y.astype(y_ref.dtype)


def _merge(o_a, o_b, o_c, w_branch, u, gate_off, tm, tn):
    t, bw = o_a.shape
    d = w_branch.shape[2]
    nj = d // tn
    g0 = gate_off // tn
    branch_spec = pl.BlockSpec((tm, bw), lambda i, j: (i, 0))
    gate_specs = [pl.BlockSpec((tm, tn), functools.partial(lambda i, j, n: (i, g0 + n * nj + j), n=n))
                  for n in range(N_BRANCH)]
    blocks = 3 * tm * bw * 2 + N_BRANCH * bw * tn * 2 + 3 * tm * tn * 2 + tm * tn * 2
    return pl.pallas_call(
        _merge_kernel,
        out_shape=jax.ShapeDtypeStruct((t, d), BF16),
        grid=(t // tm, nj),
        in_specs=[branch_spec, branch_spec, branch_spec,
                  pl.BlockSpec((N_BRANCH, bw, tn), lambda i, j: (0, 0, j))] + gate_specs,
        out_specs=pl.BlockSpec((tm, tn), lambda i, j: (i, j)),
        compiler_params=_params(("parallel", "parallel"), blocks, 3 * tm * tn * 4),
        name="gated_merge",
    )(o_a, o_b, o_c, w_branch, u, u, u)


def _chunk_indicator(pos0, rows, first_lane):
    chunk = (pos0 + lax.broadcasted_iota(jnp.int32, (rows, LANES), 0)) // CHUNK
    lane = lax.broadcasted_iota(jnp.int32, (rows, LANES), 1)
    return jnp.where(lane - first_lane == chunk, 1.0, 0.0)


def _chunk_mask_rows(pos0, cols, first_row):
    chunk = (pos0 + lax.broadcasted_iota(jnp.int32, (LANES, cols), 1)) // CHUNK
    row = lax.broadcasted_iota(jnp.int32, (LANES, cols), 0)
    return jnp.where((row >= first_row) & (row - first_row > chunk), MASKED, 0.0)


def _flash_loop(qt_ref, k_ref, vt_ref, s_refs, p_refs, acc_ref, l_ref, *, tk, ratio):
    n_q, _, cols = acc_ref.shape

    def scores(qi, j):
        k = k_ref[pl.ds(pl.multiple_of(j * tk, tk), tk), :]
        return jnp.dot(k, qt_ref[0, 0, qi], preferred_element_type=F32)

    def step(cur, carry):
        nxt, prv = (cur + 1) % N_BUF, (cur - 1) % N_BUF
        qi, j, m, l, alpha_prev, qi_prev, j_prev = carry
        last = j == qi // ratio
        qi_next = jnp.where(last, jnp.minimum(qi + 1, n_q - 1), qi)
        j_next = jnp.where(last, 0, j + 1)
        s_refs[nxt][...] = scores(qi_next, j_next)
        s = s_refs[cur][...]
        m_new = jnp.maximum(m, jnp.max(s, axis=0, keepdims=True))
        alpha = jnp.exp2(m - m_new)
        p = jnp.exp2(s - m_new)
        p_refs[cur][...] = p.astype(BF16)
        l_new = alpha * l + jnp.sum(p, axis=0, keepdims=True)
        half_cols = cols // 2
        for c in range(2):
            cs = slice(c * half_cols, (c + 1) * half_cols)
            acc = alpha_prev[:, cs] * acc_ref[qi_prev, :, cs] + jnp.dot(
                vt_ref[0, 0, j_prev], p_refs[prv][:, cs], preferred_element_type=F32)
            acc_ref[qi_prev, :, cs] = acc
        l_ref[qi_prev] = l
        return qi_next, j_next, jnp.where(last, MASKED, m_new), l_new, alpha, qi, j

    def steps(count, carry):
        for i in range(count):
            carry = step(i % N_BUF, carry)
        return carry

    zero = jnp.int32(0)
    s_refs[0][...] = scores(zero, zero)
    p_refs[N_BUF - 1][...] = jnp.zeros(p_refs[0].shape, BF16)
    row = jnp.zeros((1, cols), F32)
    acc_ref[...] = jnp.zeros(acc_ref.shape, F32)
    carry = (zero, zero, row + MASKED, row, row, zero, zero)
    n_steps = sum(qi // ratio + 1 for qi in range(n_q)) + 1
    carry = lax.fori_loop(0, n_steps // STEPS_PER_ITERATION, lambda _, c: steps(STEPS_PER_ITERATION, c), carry)
    steps(n_steps % STEPS_PER_ITERATION, carry)


def _rotate_rows(y, half, rope_t):
    cos, sin_lo, sin_hi = rope_t[:LANES], rope_t[LANES:2 * LANES], rope_t[2 * LANES:]
    up = jnp.concatenate([y[half:], y[:half]], axis=0)
    down = jnp.concatenate([y[-half:], y[:-half]], axis=0)
    return y * cos + up * sin_lo + down * sin_hi


def _silu_gate_store(o_ref, z_ref, rows, y):
    z = z_ref[rows, :].astype(F32)
    o_ref[rows, :] = (y * (z * _sigmoid(z))).astype(o_ref.dtype)


def _prep_a_kernel(cq_ref, ckv_ref, ht_ref, wkr_ref, gcq_ref, gckv_ref, wuqt_ref, wk_ref, wvt_ref, gq_ref, gk_ref,
                   rope_ref, ropet_ref, qt_ref, k_ref, vt_ref, *, heads, scale, n_seq):
    rows = cq_ref.shape[0]
    pos0 = (pl.program_id(0) % n_seq) * rows
    rope = rope_ref[...]
    rope_t = ropet_ref[...]
    cos, sin_lo, sin_hi = rope[:, :LANES], rope[:, LANES:2 * LANES], rope[:, 2 * LANES:]

    def rotate(y):
        return y * cos + pltpu.roll(y, LANES - A_ROPE // 2, 1) * sin_lo + pltpu.roll(y, A_ROPE // 2, 1) * sin_hi

    cq = cq_ref[...].astype(F32)
    cq = _rms(cq, gcq_ref[...], cq.shape[-1])
    ckv = ckv_ref[...].astype(F32)
    ckv = _rms(ckv, gckv_ref[...], ckv.shape[-1])
    qat = jnp.dot(wuqt_ref[...], cq.T.astype(BF16), preferred_element_type=F32)
    vat = jnp.dot(wvt_ref[...], ckv.T.astype(BF16), preferred_element_type=F32)
    kna = jnp.dot(ckv.astype(BF16), wk_ref[...], preferred_element_type=F32)
    gq = gq_ref[...]
    gk = gk_ref[...]
    kr = lax.dot_general(ht_ref[...], wkr_ref[...], (((0,), (0,)), ((), ())), preferred_element_type=F32)
    k_rope = (rotate(_rms(kr, gk[:, LANES:], A_ROPE)) + _chunk_indicator(pos0, rows, A_ROPE)).astype(BF16)
    mask_rows = _chunk_mask_rows(pos0, rows, A_ROPE)
    hw = 2 * LANES
    for h in range(heads):
        xn = qat[h * hw:h * hw + LANES]
        q_nope = xn * lax.rsqrt(jnp.sum(xn * xn, axis=0, keepdims=True) * (1.0 / A_NOPE) + EPS) * gq[:LANES]
        xr = qat[h * hw + LANES:(h + 1) * hw]
        q_rope = xr * lax.rsqrt(jnp.sum(xr * xr, axis=0, keepdims=True) * (1.0 / A_ROPE) + EPS) * gq[LANES:]
        q_rope = _rotate_rows(q_rope, A_ROPE // 2, rope_t)
        qt_ref[0, h, 0, :LANES, :] = (q_nope * scale).astype(BF16)
        qt_ref[0, h, 0, LANES:, :] = (q_rope * scale + mask_rows).astype(BF16)
        k_nope = _rms(kna[:, h * LANES:(h + 1) * LANES], gk[:, :LANES], A_NOPE)
        k_ref[:, h * hw:h * hw + LANES] = k_nope.astype(BF16)
        k_ref[:, h * hw + LANES:(h + 1) * hw] = k_rope
        vt_ref[0, h, 0] = vat[h * LANES:(h + 1) * LANES].astype(BF16)


def _prep_a(u, h_t, w_kr, off_cq, off_ckv, g_cq, g_ckv, w_uqt, w_k, w_vt, g_q, g_k, rope, rope_t, batch, seq, heads,
            t_rows):
    t = u.shape[0]
    q_rank, kv_rank = w_uqt.shape[1], w_k.shape[0]
    hw = 2 * LANES
    n_seq = seq // t_rows
    assert seq // CHUNK <= LANES - A_ROPE
    scale = (A_NOPE + A_ROPE) ** -0.5 * LOG2E
    row = lambda i: (i, 0)
    const = lambda i: (0, 0)
    tile = lambda i: (i // n_seq, 0, i % n_seq, 0, 0)
    d = h_t.shape[0]
    blocks = (t_rows * (q_rank + kv_rank + d) * 2 + d * LANES * 2 + (q_rank * hw + 2 * kv_rank * LANES) * heads * 2
              + 2 * t_rows * 3 * LANES * 4 + t_rows * heads * (2 * hw + LANES) * 2)
    return pl.pallas_call(
        functools.partial(_prep_a_kernel, heads=heads, scale=scale, n_seq=n_seq),
        out_shape=(jax.ShapeDtypeStruct((batch, heads, n_seq, hw, t_rows), BF16),
                   jax.ShapeDtypeStruct((t, heads * hw), BF16),
                   jax.ShapeDtypeStruct((batch, heads, n_seq, A_V, t_rows), BF16)),
        grid=(t // t_rows,),
        in_specs=[
            pl.BlockSpec((t_rows, q_rank), lambda i: (i, off_cq // q_rank)),
            pl.BlockSpec((t_rows, kv_rank), lambda i: (i, off_ckv // kv_rank)),
            pl.BlockSpec((d, t_rows), lambda i: (0, i)),
            pl.BlockSpec((d, LANES), const),
            pl.BlockSpec((1, q_rank), const),
            pl.BlockSpec((1, kv_rank), const),
            pl.BlockSpec((heads * hw, q_rank), const),
            pl.BlockSpec((kv_rank, heads * LANES), const),
            pl.BlockSpec((heads * LANES, kv_rank), const),
            pl.BlockSpec((hw, 1), const),
            pl.BlockSpec((1, hw), const),
            pl.BlockSpec((t_rows, 3 * LANES), lambda i: (i % n_seq, 0)),
            pl.BlockSpec((3 * LANES, t_rows), lambda i: (0, i % n_seq)),
        ],
        out_specs=(pl.BlockSpec((1, heads, 1, hw, t_rows), tile),
                   pl.BlockSpec((t_rows, heads * hw), row),
                   pl.BlockSpec((1, heads, 1, A_V, t_rows), tile)),
        compiler_params=_params(("parallel",), blocks, 4 * t_rows * heads * hw * 4),
        name="mla_prep",
    )(u, u, h_t, w_kr, g_cq, g_ckv, w_uqt, w_k, w_vt, g_q, g_k, rope, rope_t)


def _attn_a_kernel(qt_ref, k_ref, vt_ref, z_ref, o_ref, *scratch, t):
    s_refs, p_refs, (acc_ref, l_ref) = scratch[:N_BUF], scratch[N_BUF:2 * N_BUF], scratch[2 * N_BUF:]
    _flash_loop(qt_ref, k_ref, vt_ref, s_refs, p_refs, acc_ref, l_ref, tk=t, ratio=1)
    for qi in range(acc_ref.shape[0]):
        _silu_gate_store(o_ref, z_ref, slice(qi * t, (qi + 1) * t), (acc_ref[qi] * (1.0 / l_ref[qi])).T)


def _attn_a(qt, k, vt, u, z_off, batch, seq, heads, t):
    n_tiles = seq // t
    dk = k.shape[1] // heads
    zb = z_off // LANES
    seq_head = lambda b, h: (b, h)
    tiles = lambda b, h: (b, h, 0, 0, 0)
    blocks = 2 * seq * dk * 2 + 3 * seq * LANES * 2
    scratch = N_BUF * t * t * 6 + n_tiles * A_V * t * 4
    return pl.pallas_call(
        functools.partial(_attn_a_kernel, t=t),
        out_shape=jax.ShapeDtypeStruct((batch * seq, heads * LANES), BF16),
        grid=(batch, heads),
        in_specs=[
            pl.BlockSpec((1, 1, n_tiles, dk, t), tiles),
            pl.BlockSpec((seq, dk), seq_head),
            pl.BlockSpec((1, 1, n_tiles, A_V, t), tiles),
            pl.BlockSpec((seq, LANES), lambda b, h: (b, zb + h)),
        ],
        out_specs=pl.BlockSpec((seq, LANES), seq_head),
        scratch_shapes=([pltpu.VMEM((t, t), F32)] * N_BUF + [pltpu.VMEM((t, t), BF16)] * N_BUF
                        + [pltpu.VMEM((n_tiles, A_V, t), F32), pltpu.VMEM((n_tiles, 1, t), F32)]),
        compiler_params=_params(("parallel", "parallel"), blocks, scratch + 6 * t * t * 4),
        name="mla_attention",
    )(qt, k, vt, u)


def _proj_t_b_kernel(wt_ref, ht_ref, gq_ref, ropet_ref, qt_ref, vt_ref, *, heads, scale, n_seq, tq):
    cols = ht_ref.shape[1]
    pos0 = (pl.program_id(0) % n_seq) * cols
    ht = ht_ref[...]
    group = PROJ_HEAD_GROUP * LANES
    rope_t = ropet_ref[...]
    gq = gq_ref[...]
    first_map = lax.broadcasted_iota(jnp.int32, (LANES, 1), 0) < B_DK
    mask_rows = _chunk_mask_rows(pos0, cols, 0).astype(BF16)
    for h in range(heads):
        if h % PROJ_HEAD_GROUP == 0:
            res_q = jnp.dot(wt_ref[h * LANES:h * LANES + group], ht, preferred_element_type=F32)
            res_v = jnp.dot(wt_ref[(heads + h) * LANES:(heads + h) * LANES + group], ht,
                            preferred_element_type=F32)
        g = h % PROJ_HEAD_GROUP
        x = res_q[g * LANES:(g + 1) * LANES]
        xsq = x * x
        ss_lo = jnp.sum(xsq[:B_DK], axis=0, keepdims=True)
        ss_hi = jnp.sum(xsq[B_DK:], axis=0, keepdims=True)
        y = x * lax.rsqrt(jnp.where(first_map, ss_lo, ss_hi) * (1.0 / B_DK) + EPS) * gq
        q = _rotate_rows(y, B_ROT // 2, rope_t) * scale
        q_lo = jnp.where(first_map, q, 0.0).astype(BF16)
        q_hi = jnp.where(first_map, 0.0, q).astype(BF16)
        for a in range(cols // tq):
            cs = slice(a * tq, (a + 1) * tq)
            qt_ref[0, h, a, :LANES, :tq] = q_lo[:, cs]
            qt_ref[0, h, a, :LANES, tq:] = q_hi[:, cs]
            qt_ref[0, h, a, LANES:, :tq] = mask_rows[:, cs]
            qt_ref[0, h, a, LANES:, tq:] = mask_rows[:, cs]
        vt_ref[0, h, 0] = res_v[g * LANES:(g + 1) * LANES].astype(BF16)


def _proj_t_b(w_t, h_t, g_q, rope_t, batch, seq, heads, tq, tk):
    d, t = h_t.shape
    n_seq = seq // tk
    assert seq // CHUNK <= LANES
    tile = lambda i: (i // n_seq, 0, i % n_seq, 0, 0)
    blocks = w_t.size * 2 + d * tk * 2 + 3 * LANES * tk * 4 + heads * tk * (4 * LANES + B_DV) * 2
    return pl.pallas_call(
        functools.partial(_proj_t_b_kernel, heads=heads, scale=B_DK ** -0.5 * LOG2E, n_seq=n_seq, tq=tq),
        out_shape=(jax.ShapeDtypeStruct((batch, heads, seq // tq, 2 * LANES, 2 * tq), BF16),
                   jax.ShapeDtypeStruct((batch, heads, n_seq, B_DV, tk), BF16)),
        grid=(t // tk,),
        in_specs=[
            pl.BlockSpec(w_t.shape, lambda i: (0, 0)),
            pl.BlockSpec((d, tk), lambda i: (0, i)),
            pl.BlockSpec((LANES, 1), lambda i: (0, 0)),
            pl.BlockSpec((3 * LANES, tk), lambda i: (0, i % n_seq)),
        ],
        out_specs=(pl.BlockSpec((1, heads, tk // tq, 2 * LANES, 2 * tq), tile),
                   pl.BlockSpec((1, heads, 1, B_DV, tk), tile)),
        compiler_params=_params(("parallel",), blocks, 3 * w_t.shape[0] * tk * 4),
        name="diff_proj_qv",
    )(w_t, h_t, g_q, rope_t)


def _prep_b_kernel(k_ref, gk_ref, rope_ref, ko_ref, *, heads, n_seq):
    rows = k_ref.shape[0]
    pos0 = (pl.program_id(0) % n_seq) * rows
    rope = rope_ref[...]
    cos, sin_lo, sin_hi = rope[:, :LANES], rope[:, LANES:2 * LANES], rope[:, 2 * LANES:]
    first_map = lax.broadcasted_iota(jnp.int32, (1, LANES), 1) < B_DK
    gk = gk_ref[...]
    indicator = _chunk_indicator(pos0, rows, 0).astype(BF16)
    for h in range(heads):
        x = k_ref[:, h * LANES:(h + 1) * LANES].astype(F32)
        xsq = x * x
        ss_lo = jnp.sum(jnp.where(first_map, xsq, 0.0), axis=-1, keepdims=True)
        ss_hi = jnp.sum(jnp.where(first_map, 0.0, xsq), axis=-1, keepdims=True)
        y = x * lax.rsqrt(jnp.where(first_map, ss_lo, ss_hi) * (1.0 / B_DK) + EPS) * gk
        y = y * cos + pltpu.roll(y, LANES - B_ROT // 2, 1) * sin_lo + pltpu.roll(y, B_ROT // 2, 1) * sin_hi
        ko_ref[:, 2 * h * LANES:(2 * h + 1) * LANES] = y.astype(BF16)
        ko_ref[:, (2 * h + 1) * LANES:(2 * h + 2) * LANES] = indicator


def _prep_b(u, off_k, g_k, rope, seq, heads, t_rows):
    t = u.shape[0]
    w = heads * LANES
    n_seq = seq // t_rows
    return pl.pallas_call(
        functools.partial(_prep_b_kernel, heads=heads, n_seq=n_seq),
        out_shape=jax.ShapeDtypeStruct((t, 2 * w), BF16),
        grid=(t // t_rows,),
        in_specs=[
            pl.BlockSpec((t_rows, w), lambda i: (i, off_k // w)),
            pl.BlockSpec((1, LANES), lambda i: (0, 0)),
            pl.BlockSpec((t_rows, 3 * LANES), lambda i: (i % n_seq, 0)),
        ],
        out_specs=pl.BlockSpec((t_rows, 2 * w), lambda i: (i, 0)),
        compiler_params=_params(("parallel",), 3 * t_rows * w * 2 + t_rows * 3 * LANES * 4, 8 * t_rows * LANES * 4),
        name="diff_prep_k",
    )(u, g_k, rope)


def _attn_b_kernel(qt_ref, k_ref, vt_ref, z_ref, lam_ref, gsub_ref, o_ref, *scratch, tq, tk, lam_init):
    s_refs, p_refs, (acc_ref, l_ref) = scratch[:N_BUF], scratch[N_BUF:2 * N_BUF], scratch[2 * N_BUF:]
    _flash_loop(qt_ref, k_ref, vt_ref, s_refs, p_refs, acc_ref, l_ref, tk=tk, ratio=tk // tq)
    lf = lam_ref[...]
    lam = (jnp.exp(jnp.sum(lf[0:1] * lf[1:2], axis=-1, keepdims=True))
           - jnp.exp(jnp.sum(lf[2:3] * lf[3:4], axis=-1, keepdims=True)) + lam_init)
    gsub = gsub_ref[...]
    for qi in range(acc_ref.shape[0]):
        o = acc_ref[qi] * (1.0 / l_ref[qi])
        a = o[:, :tq] - lam * o[:, tq:]
        ss = jnp.sum(a * a, axis=0, keepdims=True) * (1.0 / B_DV)
        y = a * lax.rsqrt(ss + EPS) * gsub * (1.0 - lam_init)
        _silu_gate_store(o_ref, z_ref, slice(qi * tq, (qi + 1) * tq), y.T)


def _attn_b(qt, k, vt, u, z_off, lam, g_sub, lam_init, batch, seq, heads, tq, tk):
    n_q, n_k = seq // tq, seq // tk
    zb = z_off // LANES
    cols = 2 * tq
    seq_head = lambda b, h: (b, h)
    tiles = lambda b, h: (b, h, 0, 0, 0)
    blocks = 2 * seq * 2 * LANES * 2 + seq * 2 * LANES * 2 + 3 * seq * LANES * 2
    scratch = N_BUF * tk * cols * 6 + n_q * B_DV * cols * 4
    return pl.pallas_call(
        functools.partial(_attn_b_kernel, tq=tq, tk=tk, lam_init=lam_init),
        out_shape=jax.ShapeDtypeStruct((batch * seq, heads * LANES), BF16),
        grid=(batch, heads),
        in_specs=[
            pl.BlockSpec((1, 1, n_q, 2 * LANES, cols), tiles),
            pl.BlockSpec((seq, 2 * LANES), seq_head),
            pl.BlockSpec((1, 1, n_k, B_DV, tk), tiles),
            pl.BlockSpec((seq, LANES), lambda b, h: (b, zb + h)),
            pl.BlockSpec(lam.shape, lambda b, h: (0, 0)),
            pl.BlockSpec((B_DV, 1), lambda b, h: (0, 0)),
        ],
        out_specs=pl.BlockSpec((seq, LANES), seq_head),
        scratch_shapes=([pltpu.VMEM((tk, cols), F32)] * N_BUF + [pltpu.VMEM((tk, cols), BF16)] * N_BUF
                        + [pltpu.VMEM((n_q, B_DV, cols), F32), pltpu.VMEM((n_q, 1, cols), F32)]),
        compiler_params=_params(("parallel", "parallel"), blocks, scratch + 6 * tk * cols * 4),
        name="diff_attention",
    )(qt, k, vt, u, lam, g_sub)


C_PAD_SLOT = 64


def _proj_t_c_kernel(wt_ref, ht_ref, gq_ref, qt_ref, vt_ref, *, heads, scale, tq):
    cols = ht_ref.shape[1]
    r = pl.program_id(1)
    is_pad = r == 0
    pos0 = jnp.maximum(r - 1, 0) * cols
    ht = ht_ref[...]
    group = PROJ_HEAD_GROUP * LANES
    gq = gq_ref[...]
    q_chunk = (pos0 + lax.broadcasted_iota(jnp.int32, (LANES, cols), 1)) // CHUNK
    slot = lax.broadcasted_iota(jnp.int32, (LANES, cols), 0)
    out_of_band = ((slot < C_PAD_SLOT) & ((slot > q_chunk) | (slot < q_chunk - C_LEFT_CHUNKS))) | (slot == C_PAD_SLOT)
    mask_rows = jnp.where(out_of_band, MASKED, 0.0).astype(BF16)
    for h in range(heads):
        if h % PROJ_HEAD_GROUP == 0:
            res_q = jnp.dot(wt_ref[h * LANES:h * LANES + group], ht, preferred_element_type=F32)
            res_v = jnp.dot(wt_ref[(heads + h) * LANES:(heads + h) * LANES + group], ht,
                            preferred_element_type=F32)
        g = h % PROJ_HEAD_GROUP
        x = res_q[g * LANES:(g + 1) * LANES]
        q = x * lax.rsqrt(jnp.sum(x * x, axis=0, keepdims=True) * (1.0 / C_DH) + EPS) * gq * scale
        q = q.astype(BF16)
        v = res_v[g * LANES:(g + 1) * LANES].astype(BF16)
        v = jnp.where(is_pad, jnp.zeros_like(v), v)
        for a in range(cols // tq):
            cs = slice(a * tq, (a + 1) * tq)
            qt_ref[0, h, a, :LANES, :] = q[:, cs]
            qt_ref[0, h, a, LANES:, :] = mask_rows[:, cs]
            vt_ref[0, h, a] = v[:, cs]


def _proj_t_c(w_t, h_t, g_q, batch, seq, heads, pad, tq):
    d, t = h_t.shape
    n_seq = seq // pad
    per = pad // tq
    assert seq // CHUNK <= C_PAD_SLOT
    blocks = w_t.size * 2 + d * pad * 2 + heads * pad * 3 * LANES * 2
    return pl.pallas_call(
        functools.partial(_proj_t_c_kernel, heads=heads, scale=C_DH ** -0.5 * LOG2E, tq=tq),
        out_shape=(jax.ShapeDtypeStruct((batch, heads, seq // tq, 2 * LANES, tq), BF16),
                   jax.ShapeDtypeStruct((batch, heads, (seq + pad) // tq, C_DH, tq), BF16)),
        grid=(batch, n_seq + 1),
        in_specs=[
            pl.BlockSpec(w_t.shape, lambda b, r: (0, 0)),
            pl.BlockSpec((d, pad), lambda b, r: (0, b * n_seq + jnp.maximum(r - 1, 0))),
            pl.BlockSpec((LANES, 1), lambda b, r: (0, 0)),
        ],
        out_specs=(pl.BlockSpec((1, heads, per, 2 * LANES, tq), lambda b, r: (b, 0, jnp.maximum(r - 1, 0), 0, 0)),
                   pl.BlockSpec((1, heads, per, C_DH, tq), lambda b, r: (b, 0, r, 0, 0))),
        compiler_params=_params(("parallel", "arbitrary"), blocks, 3 * w_t.shape[0] * pad * 4),
        name="band_proj_qv",
    )(w_t, h_t, g_q)


def _prep_c_kernel(k_ref, gk_ref, ko_ref, *, heads):
    rows = k_ref.shape[0]
    r = pl.program_id(1)
    is_pad = r == 0
    pos0 = jnp.maximum(r - 1, 0) * rows
    gk = gk_ref[...]
    lane = lax.broadcasted_iota(jnp.int32, (rows, LANES), 1)
    indicator = jnp.where(is_pad, jnp.where(lane == C_PAD_SLOT, 1.0, 0.0), _chunk_indicator(pos0, rows, 0))
    indicator = indicator.astype(BF16)
    for h in range(heads):
        kn = _rms(k_ref[:, h * LANES:(h + 1) * LANES].astype(F32), gk, C_DH).astype(BF16)
        ko_ref[:, 2 * h * LANES:(2 * h + 1) * LANES] = jnp.where(is_pad, jnp.zeros_like(kn), kn)
        ko_ref[:, (2 * h + 1) * LANES:(2 * h + 2) * LANES] = indicator


def _prep_c(u, off_k, g_k, batch, seq, heads, pad):
    w = heads * LANES
    n_seq = seq // pad
    return pl.pallas_call(
        functools.partial(_prep_c_kernel, heads=heads),
        out_shape=jax.ShapeDtypeStruct((batch * (seq + pad), 2 * w), BF16),
        grid=(batch, n_seq + 1),
        in_specs=[
            pl.BlockSpec((pad, w), lambda b, r: (b * n_seq + jnp.maximum(r - 1, 0), off_k // w)),
            pl.BlockSpec((1, LANES), lambda b, r: (0, 0)),
        ],
        out_specs=pl.BlockSpec((pad, 2 * w), lambda b, r: (b * (n_seq + 1) + r, 0)),
        compiler_params=_params(("parallel", "parallel"), 3 * pad * w * 2, 4 * pad * LANES * 4),
        name="band_prep_k",
    )(u, g_k)


def _attn_c_kernel(qt_ref, k_ref, vt_ref, z_ref, rel_ref, o_ref, *scratch, tq, tw):
    n_q = qt_ref.shape[2]
    s_refs, p_refs, (bias_ref,) = scratch[:N_BUF], scratch[N_BUF:2 * N_BUF], scratch[2 * N_BUF:]
    width = rel_ref.shape[-1]
    toeplitz = pltpu.roll(jnp.broadcast_to(rel_ref[0], (tq, width)), 0, 1, stride=1, stride_axis=0)
    bias_ref[...] = (toeplitz[:, :tw] * LOG2E).T

    def scores(i):
        return jnp.dot(k_ref[i * tq:i * tq + tw, :], qt_ref[0, 0, i], preferred_element_type=F32) + bias_ref[...]

    s_refs[0][...] = scores(0)
    for i in range(n_q + 1):
        cur, nxt, prv = i % N_BUF, (i + 1) % N_BUF, (i - 1) % N_BUF
        if i + 1 < n_q:
            s_refs[nxt][...] = scores(i + 1)
        if i < n_q:
            s = s_refs[cur][...]
            p = jnp.exp2(s - jnp.max(s, axis=0, keepdims=True))
            p_refs[cur][...] = p.astype(BF16)
            l = jnp.sum(p, axis=0, keepdims=True)
        if i > 0:
            acc = None
            for a in range(tw // tq):
                part = jnp.dot(vt_ref[0, 0, i - 1 + a], p_refs[prv][a * tq:(a + 1) * tq, :],
                               preferred_element_type=F32)
                acc = part if acc is None else acc + part
            _silu_gate_store(o_ref, z_ref, slice((i - 1) * tq, i * tq), (acc * (1.0 / l_prev)).T)
        l_prev = l


def _attn_c(qt, k, vt, u, z_off, rel_rows, batch, seq, heads, tq, pad):
    n_q = seq // tq
    tw = tq + pad
    zb = z_off // LANES
    tiles = lambda b, h: (b, h, 0, 0, 0)
    blocks = seq * 2 * LANES * 2 + (seq + pad) * 3 * LANES * 2 + 2 * seq * LANES * 2
    return pl.pallas_call(
        functools.partial(_attn_c_kernel, tq=tq, tw=tw),
        out_shape=jax.ShapeDtypeStruct((batch * seq, heads * LANES), BF16),
        grid=(batch, heads),
        in_specs=[
            pl.BlockSpec((1, 1, n_q, 2 * LANES, tq), tiles),
            pl.BlockSpec((seq + pad, 2 * LANES), lambda b, h: (b, h)),
            pl.BlockSpec((1, 1, (seq + pad) // tq, C_DH, tq), tiles),
            pl.BlockSpec((seq, LANES), lambda b, h: (b, zb + h)),
            pl.BlockSpec((1, 1, rel_rows.shape[-1]), lambda b, h: (h, 0, 0)),
        ],
        out_specs=pl.BlockSpec((seq, LANES), lambda b, h: (b, h)),
        scratch_shapes=([pltpu.VMEM((tw, tq), F32)] * N_BUF + [pltpu.VMEM((tw, tq), BF16)] * N_BUF
                        + [pltpu.VMEM((tw, tq), F32)]),
        compiler_params=_params(("parallel", "parallel"), blocks, 12 * tw * tq * 4),
        name="band_attention",
    )(qt, k, vt, u, rel_rows)


def _rope_table(seq, dim, theta, group):
    half = dim // 2
    inv = 1.0 / (jnp.float32(theta) ** (jnp.arange(0, dim, 2, dtype=F32) / dim))
    ang = jnp.arange(seq, dtype=F32)[:, None] * inv[None, :]
    cos, sin = jnp.cos(ang), jnp.sin(ang)
    lane = jnp.arange(LANES) % group
    idx = lane % half
    in_lo = (lane < half)[None, :]
    in_hi = ((lane >= half) & (lane < dim))[None, :]
    c = jnp.where(in_lo | in_hi, cos[:, idx], 1.0 if group < LANES else 0.0)
    s_lo = jnp.where(in_lo, -sin[:, idx], 0.0)
    s_hi = jnp.where(in_hi, sin[:, idx], 0.0)
    return jnp.concatenate([c, s_lo, s_hi], axis=1)


def _band_rel_rows(rel_bias, tq, pad):
    width = pl.next_power_of_2(2 * tq + pad)
    e = jnp.arange(width)
    e = jnp.where(e < tq + pad, e, e - width)
    rel = jnp.clip(pad - e, -(CHUNK - 1), C_REL_MAX) + (CHUNK - 1)
    return rel_bias.astype(F32)[:, None, rel]


def _row(v):
    return v.astype(F32).reshape(1, -1)


def _pad_lanes(v, width):
    return jnp.pad(v, ((0, 0),) * (v.ndim - 1) + ((0, width - v.shape[-1]),))


def _layer(x2, layer_idx, batch, seq, rope_a, rope_b, g_pre, w_main, w_t_b, w_t_c, w_kr, a_g_cq, a_g_ckv, a_w_uq,
           a_w_ukv, a_g_q, a_g_k, b_g_q, b_g_k, b_lam, b_g_sub, c_g_q, c_g_k, c_rel_bias, w_branch, w_out):
    d = x2.shape[1]
    bw = w_branch.shape[1]
    q_rank, kv_rank = a_w_uq.shape[0], a_w_ukv.shape[0]
    a_heads, b_heads, c_heads = bw // A_V, bw // B_DV, bw // C_DH

    o_az, o_bk, o_bz, o_ck, o_cz, o_g = (n * bw for n in range(6))
    o_acq = o_g + N_BRANCH * d
    o_ackv = o_acq + q_rank

    h_t = _rmsnorm(x2, _row(g_pre))
    u = _matmul_t(h_t, w_main, BF16, 1024, MAIN_COLUMN_TILE, "in_proj")

    hw = 2 * LANES
    t_a = min(512, seq)
    w_uq = _pad_lanes(a_w_uq.reshape(q_rank, a_heads, A_NOPE + A_ROPE), hw).reshape(q_rank, a_heads * hw)
    w_ukv = a_w_ukv.reshape(kv_rank, a_heads, A_NOPE + A_V)
    w_k = w_ukv[:, :, :A_NOPE].reshape(kv_rank, a_heads * A_NOPE).astype(BF16)
    w_vt = w_ukv[:, :, A_NOPE:].reshape(kv_rank, a_heads * A_V).T.astype(BF16)
    g_q = jnp.concatenate([_row(a_g_q[:A_NOPE]), _pad_lanes(_row(a_g_q[A_NOPE:]), LANES)], axis=1)
    g_k = jnp.concatenate([_row(a_g_k[:A_NOPE]), _pad_lanes(_row(a_g_k[A_NOPE:]), LANES)], axis=1)
    qa, ka, vta = _prep_a(u, h_t, w_kr, o_acq, o_ackv, _row(a_g_cq), _row(a_g_ckv), w_uq.T.astype(BF16), w_k, w_vt,
                          g_q.T, g_k, rope_a, rope_a.T, batch, seq, a_heads, t_a)
    o_a = _attn_a(qa, ka, vta, u, o_az, batch, seq, a_heads, t_a)

    tq_b, tk_b = min(256, seq), min(512, seq)
    qb, vtb = _proj_t_b(w_t_b, h_t, jnp.tile(_row(b_g_q), (1, 2)).T, rope_b.T, batch, seq, b_heads, tq_b, tk_b)
    kb = _prep_b(u, o_bk, jnp.tile(_row(b_g_k), (1, 2)), rope_b, seq, b_heads, tk_b)
    lam_init = 0.8 - 0.6 * math.exp(-0.3 * layer_idx)
    o_b = _attn_b(qb, kb, vtb, u, o_bz, b_lam.astype(F32), b_g_sub.astype(F32).reshape(B_DV, 1), lam_init,
                  batch, seq, b_heads, tq_b, tk_b)

    pad = C_LEFT_CHUNKS * CHUNK
    tq_c = min(256, seq)
    qc, vtc = _proj_t_c(w_t_c, h_t, _row(c_g_q).T, batch, seq, c_heads, pad, tq_c)
    kc = _prep_c(u, o_ck, _row(c_g_k), batch, seq, c_heads, pad)
    o_c = _attn_c(qc, kc, vtc, u, o_cz, _band_rel_rows(c_rel_bias, tq_c, pad), batch, seq, c_heads, tq_c, pad)

    y = _merge(o_a, o_b, o_c, w_branch.astype(BF16), u, o_g, 1024, 1024)
    return _matmul_residual(y, w_out.astype(BF16), x2, 1024, 1024)


def kernel(x, g_pre, w_in, a_g_cq, a_g_ckv, a_w_uq, a_w_ukv, a_g_q, a_g_k, b_g_q, b_g_k, b_lam, b_g_sub,
           c_g_q, c_g_k, c_rel_bias, w_branch, w_out):
    batch, seq, d = x.shape
    rope_a = _rope_table(seq, A_ROPE, A_ROPE_THETA, LANES)
    rope_b = _rope_table(seq, B_ROT, B_ROPE_THETA, B_DK)
    x2 = x.reshape(batch * seq, d)
    q_rank, kv_rank, bw = a_w_uq.shape[1], a_w_ukv.shape[1], w_branch.shape[2]
    n_used = 5 * bw + N_BRANCH * d + q_rank + kv_rank
    n_main = -(-n_used // MAIN_COLUMN_TILE) * MAIN_COLUMN_TILE
    w_in_t = jnp.swapaxes(w_in, 1, 2)
    for l in range(g_pre.shape[0]):
        w_main, w_t_b, w_t_c, w_kr = _weights(w_in_t, l, q_rank, kv_rank, bw, N_BRANCH * d, n_main)
        x2 = _layer(x2, l, batch, seq, rope_a, rope_b, g_pre[l], w_main, w_t_b, w_t_c, w_kr, a_g_cq[l],
                    a_g_ckv[l], a_w_uq[l], a_w_ukv[l], a_g_q[l], a_g_k[l], b_g_q[l], b_g_k[l], b_lam[l],
                    b_g_sub[l], c_g_q[l], c_g_k[l], c_rel_bias[l], w_branch[l], w_out[l])
    return x2.reshape(batch, seq, d)
```

```python
import functools
import math

import jax
import jax.numpy as jnp
from jax import lax
from jax.experimental import pallas as pl
from jax.experimental.pallas import tpu as pltpu

F32 = jnp.float32
BF16 = jnp.bfloat16

EPS = 1e-6
CHUNK = 64
MASKED = -1e30
LOG2E = math.log2(math.e)

A_NOPE, A_ROPE, A_V = 128, 64, 128
A_ROPE_THETA = 10000.0
B_DK, B_DV = 64, 128
B_ROT = B_DK // 4
B_ROPE_THETA = 500000.0
C_DH = 128
C_LEFT_CHUNKS = 8
C_REL_MAX = 128
N_BRANCH = 3

PROJ_HEAD_GROUP = 2
N_BUF = 3
MAIN_COLUMN_TILE = 1024
STEPS_PER_ITERATION = 24
LANES = 128
V7X_VMEM_BUDGET = 56 * 2**20


def _params(semantics, block_bytes, temp_bytes=0):
    need = 2 * block_bytes + temp_bytes + (4 << 20)
    return pltpu.CompilerParams(
        dimension_semantics=semantics,
        vmem_limit_bytes=int(min(max(need, 16 << 20), V7X_VMEM_BUDGET)),
    )


def _sigmoid(z):
    return 1.0 / (1.0 + jnp.exp(-z))


def _rms(x, g, n):
    ss = jnp.sum(x * x, axis=-1, keepdims=True) * (1.0 / n)
    return x * lax.rsqrt(ss + EPS) * g


def _rmsnorm_kernel(x_ref, g_ref, ot_ref):
    x = x_ref[...]
    ot_ref[...] = _rms(x, g_ref[...], x.shape[-1]).T.astype(ot_ref.dtype)


def _rmsnorm(x, g, tm=512):
    t, d = x.shape
    return pl.pallas_call(
        _rmsnorm_kernel,
        out_shape=jax.ShapeDtypeStruct((d, t), BF16),
        grid=(t // tm,),
        in_specs=[pl.BlockSpec((tm, d), lambda i: (i, 0)), pl.BlockSpec((1, d), lambda i: (0, 0))],
        out_specs=pl.BlockSpec((d, tm), lambda i: (0, i)),
        compiler_params=_params(("parallel",), tm * d * 6, tm * d * 12),
        name="pre_rmsnorm",
    )(x, g)


def _mm_t_kernel(at_ref, b_ref, o_ref):
    o_ref[...] = lax.dot_general(at_ref[...], b_ref[...], (((0,), (0,)), ((), ())),
                                 preferred_element_type=F32).astype(o_ref.dtype)


def _matmul_t(a_t, b, out_dtype, tm, tn, name):
    k, m = a_t.shape
    n = b.shape[1]
    blocks = tm * k * 2 + k * tn * 2 + tm * tn * jnp.dtype(out_dtype).itemsize
    return pl.pallas_call(
        _mm_t_kernel,
        out_shape=jax.ShapeDtypeStruct((m, n), out_dtype),
        grid=(m // tm, n // tn),
        in_specs=[pl.BlockSpec((k, tm), lambda i, j: (0, i)), pl.BlockSpec((k, tn), lambda i, j: (0, j))],
        out_specs=pl.BlockSpec((tm, tn), lambda i, j: (i, j)),
        compiler_params=_params(("parallel", "parallel"), blocks, tm * tn * 4),
        name=name,
    )(a_t, b)


def _mm_res_kernel(a_ref, b_ref, x_ref, o_ref):
    o_ref[...] = x_ref[...] + jnp.dot(a_ref[...], b_ref[...], preferred_element_type=F32)


def _matmul_residual(a, b, x, tm, tn):
    m, k = a.shape
    n = b.shape[1]
    blocks = tm * k * 2 + k * tn * 2 + 2 * tm * tn * 4
    return pl.pallas_call(
        _mm_res_kernel,
        out_shape=jax.ShapeDtypeStruct((m, n), F32),
        grid=(m // tm, n // tn),
        in_specs=[
            pl.BlockSpec((tm, k), lambda i, j: (i, 0)),
            pl.BlockSpec((k, tn), lambda i, j: (0, j)),
            pl.BlockSpec((tm, tn), lambda i, j: (i, j)),
        ],
        out_specs=pl.BlockSpec((tm, tn), lambda i, j: (i, j)),
        compiler_params=_params(("parallel", "parallel"), blocks, tm * tn * 4),
        name="out_proj_residual",
    )(a, b, x)


def _weights_kernel(wt_ref, main_ref, tb_ref, tc_ref, kr_ref, *, q_rank, kv_rank, bw, n_gate):
    s_kr = q_rank + kv_rank
    s_az = s_kr + A_ROPE

    def piece(n):
        return wt_ref[0, s_az + n * bw:s_az + (n + 1) * bw, :]

    for slot, n in enumerate((0, 4, 6, 8)):
        main_ref[:, slot * bw:(slot + 1) * bw] = piece(n).T.astype(BF16)
    o_g = 4 * bw
    for g in range(n_gate // bw):
        main_ref[:, o_g + g * bw:o_g + (g + 1) * bw] = piece(9 + g).T.astype(BF16)
    main_ref[:, o_g + n_gate:o_g + n_gate + s_kr] = wt_ref[0, :s_kr, :].T.astype(BF16)
    n_pad = main_ref.shape[1] - (o_g + n_gate + s_kr)
    main_ref[:, o_g + n_gate + s_kr:] = jnp.zeros((main_ref.shape[0], n_pad), BF16)
    kr_ref[:, :A_ROPE] = wt_ref[0, s_kr:s_az, :].T.astype(BF16)
    kr_ref[:, A_ROPE:] = jnp.zeros((kr_ref.shape[0], LANES - A_ROPE), BF16)
    tb_ref[:bw, :] = piece(1).astype(BF16)
    tb_ref[bw:2 * bw, :] = piece(3).astype(BF16)
    tb_ref[2 * bw:, :] = piece(2).astype(BF16)
    tc_ref[:bw, :] = piece(5).astype(BF16)
    tc_ref[bw:, :] = piece(7).astype(BF16)


def _weights(w_in_t, layer, q_rank, kv_rank, bw, n_gate, n_main, cols=128):
    _, d_in, d = w_in_t.shape
    blocks = cols * d_in * 4 + cols * (n_main + LANES + 5 * bw) * 2
    return pl.pallas_call(
        functools.partial(_weights_kernel, q_rank=q_rank, kv_rank=kv_rank, bw=bw, n_gate=n_gate),
        out_shape=(jax.ShapeDtypeStruct((d, n_main), BF16),
                   jax.ShapeDtypeStruct((3 * bw, d), BF16),
                   jax.ShapeDtypeStruct((2 * bw, d), BF16),
                   jax.ShapeDtypeStruct((d, LANES), BF16)),
        grid=(d // cols,),
        in_specs=[pl.BlockSpec((1, d_in, cols), lambda r: (layer, 0, r))],
        out_specs=(pl.BlockSpec((cols, n_main), lambda r: (r, 0)),
                   pl.BlockSpec((3 * bw, cols), lambda r: (0, r)),
                   pl.BlockSpec((2 * bw, cols), lambda r: (0, r)),
                   pl.BlockSpec((cols, LANES), lambda r: (r, 0))),
        compiler_params=_params(("parallel",), blocks, 4 * cols * bw * 4),
        name="weights_prep",
    )(w_in_t)


def _merge_kernel(oa_ref, ob_ref, oc_ref, w_ref, ga_ref, gb_ref, gc_ref, y_ref):
    y = None
    for n, (o_ref, g_ref) in enumerate(((oa_ref, ga_ref), (ob_ref, gb_ref), (oc_ref, gc_ref))):
        gate = _sigmoid(g_ref[...].astype(F32))
        term = gate * jnp.dot(o_ref[...], w_ref[n], preferred_element_type=F32)
        y = term if y is None else y + term
    y_ref[...] = y.astype(y_ref.dtype)


def _merge(o_a, o_b, o_c, w_branch, u, gate_off, tm, tn):
    t, bw = o_a.shape
    d = w_branch.shape[2]
    nj = d // tn
    g0 = gate_off // tn
    branch_spec = pl.BlockSpec((tm, bw), lambda i, j: (i, 0))
    gate_specs = [pl.BlockSpec((tm, tn), functools.partial(lambda i, j, n: (i, g0 + n * nj + j), n=n))
                  for n in range(N_BRANCH)]
    blocks = 3 * tm * bw * 2 + N_BRANCH * bw * tn * 2 + 3 * tm * tn * 2 + tm * tn * 2
    return pl.pallas_call(
        _merge_kernel,
        out_shape=jax.ShapeDtypeStruct((t, d), BF16),
        grid=(t // tm, nj),
        in_specs=[branch_spec, branch_spec, branch_spec,
                  pl.BlockSpec((N_BRANCH, bw, tn), lambda i, j: (0, 0, j))] + gate_specs,
        out_specs=pl.BlockSpec((tm, tn), lambda i, j: (i, j)),
        compiler_params=_params(("parallel", "parallel"), blocks, 3 * tm * tn * 4),
        name="gated_merge",
    )(o_a, o_b, o_c, w_branch, u, u, u)


def _chunk_indicator(pos0, rows, first_lane):
    chunk = (pos0 + lax.broadcasted_iota(jnp.int32, (rows, LANES), 0)) // CHUNK
    lane = lax.broadcasted_iota(jnp.int32, (rows, LANES), 1)
    return jnp.where(lane - first_lane == chunk, 1.0, 0.0)


def _chunk_mask_rows(pos0, cols, first_row):
    chunk = (pos0 + lax.broadcasted_iota(jnp.int32, (LANES, cols), 1)) // CHUNK
    row = lax.broadcasted_iota(jnp.int32, (LANES, cols), 0)
    return jnp.where((row >= first_row) & (row - first_row > chunk), MASKED, 0.0)


def _flash_loop(qt_ref, k_ref, vt_ref, s_refs, p_refs, acc_ref, l_ref, *, tk, ratio):
    n_q, _, cols = acc_ref.shape

    def scores(qi, j):
        if len(k_ref.shape) == 5:
            return lax.dot_general(k_ref[0, 0, j], qt_ref[0, 0, qi], (((0,), (0,)), ((), ())),
                                   preferred_element_type=F32)
        k = k_ref[pl.ds(pl.multiple_of(j * tk, tk), tk), :]
        return jnp.dot(k, qt_ref[0, 0, qi], preferred_element_type=F32)

    def step(cur, carry):
        nxt, prv = (cur + 1) % N_BUF, (cur - 1) % N_BUF
        qi, j, m, l, alpha_prev, qi_prev, j_prev = carry
        last = j == qi // ratio
        qi_next = jnp.where(last, jnp.minimum(qi + 1, n_q - 1), qi)
        j_next = jnp.where(last, 0, j + 1)
        s_refs[nxt][...] = scores(qi_next, j_next)
        s = s_refs[cur][...]
        m_new = jnp.maximum(m, jnp.max(s, axis=0, keepdims=True))
        alpha = jnp.exp2(m - m_new)
        p = jnp.exp2(s - m_new)
        p_refs[cur][...] = p.astype(BF16)
        l_new = alpha * l + jnp.sum(p, axis=0, keepdims=True)
        half_cols = cols // 2
        for c in range(2):
            cs = slice(c * half_cols, (c + 1) * half_cols)
            acc = alpha_prev[:, cs] * acc_ref[qi_prev, :, cs] + jnp.dot(
                vt_ref[0, 0, j_prev], p_refs[prv][:, cs], preferred_element_type=F32)
            acc_ref[qi_prev, :, cs] = acc
        l_ref[qi_prev] = l
        return qi_next, j_next, jnp.where(last, MASKED, m_new), l_new, alpha, qi, j

    def steps(count, carry):
        for i in range(count):
            carry = step(i % N_BUF, carry)
        return carry

    zero = jnp.int32(0)
    s_refs[0][...] = scores(zero, zero)
    p_refs[N_BUF - 1][...] = jnp.zeros(p_refs[0].shape, BF16)
    row = jnp.zeros((1, cols), F32)
    acc_ref[...] = jnp.zeros(acc_ref.shape, F32)
    carry = (zero, zero, row + MASKED, row, row, zero, zero)
    n_steps = sum(qi // ratio + 1 for qi in range(n_q)) + 1
    carry = lax.fori_loop(0, n_steps // STEPS_PER_ITERATION, lambda _, c: steps(STEPS_PER_ITERATION, c), carry)
    steps(n_steps % STEPS_PER_ITERATION, carry)


def _rotate_rows(y, half, rope_t):
    cos, sin_lo, sin_hi = rope_t[:LANES], rope_t[LANES:2 * LANES], rope_t[2 * LANES:]
    up = jnp.concatenate([y[half:], y[:half]], axis=0)
    down = jnp.concatenate([y[-half:], y[:-half]], axis=0)
    return y * cos + up * sin_lo + down * sin_hi


def _silu_gate_store(o_ref, z_ref, rows, y):
    z = z_ref[rows, :].astype(F32)
    o_ref[rows, :] = (y * (z * _sigmoid(z))).astype(o_ref.dtype)


def _prep_a_kernel(cq_ref, ckv_ref, ht_ref, wkr_ref, gcq_ref, gckv_ref, wuqt_ref, wk_ref, wvt_ref, gq_ref, gk_ref,
                   rope_ref, ropet_ref, qt_ref, k_ref, vt_ref, *, heads, scale, n_seq):
    rows = cq_ref.shape[0]
    pos0 = (pl.program_id(0) % n_seq) * rows
    rope = rope_ref[...]
    rope_t = ropet_ref[...]
    cos, sin_lo, sin_hi = rope[:, :LANES], rope[:, LANES:2 * LANES], rope[:, 2 * LANES:]

    def rotate(y):
        return y * cos + pltpu.roll(y, LANES - A_ROPE // 2, 1) * sin_lo + pltpu.roll(y, A_ROPE // 2, 1) * sin_hi

    cq = cq_ref[...].astype(F32)
    cq = _rms(cq, gcq_ref[...], cq.shape[-1])
    ckv = ckv_ref[...].astype(F32)
    ckv = _rms(ckv, gckv_ref[...], ckv.shape[-1])
    qat = jnp.dot(wuqt_ref[...], cq.T.astype(BF16), preferred_element_type=F32)
    vat = jnp.dot(wvt_ref[...], ckv.T.astype(BF16), preferred_element_type=F32)
    kna = jnp.dot(ckv.astype(BF16), wk_ref[...], preferred_element_type=F32)
    gq = gq_ref[...]
    gk = gk_ref[...]
    kr = lax.dot_general(ht_ref[...], wkr_ref[...], (((0,), (0,)), ((), ())), preferred_element_type=F32)
    k_rope = (rotate(_rms(kr, gk[:, LANES:], A_ROPE)) + _chunk_indicator(pos0, rows, A_ROPE)).astype(BF16)
    mask_rows = _chunk_mask_rows(pos0, rows, A_ROPE)
    hw = 2 * LANES
    for h in range(heads):
        xn = qat[h * hw:h * hw + LANES]
        q_nope = xn * lax.rsqrt(jnp.sum(xn * xn, axis=0, keepdims=True) * (1.0 / A_NOPE) + EPS) * gq[:LANES]
        xr = qat[h * hw + LANES:(h + 1) * hw]
        q_rope = xr * lax.rsqrt(jnp.sum(xr * xr, axis=0, keepdims=True) * (1.0 / A_ROPE) + EPS) * gq[LANES:]
        q_rope = _rotate_rows(q_rope, A_ROPE // 2, rope_t)
        qt_ref[0, h, 0, :LANES, :] = (q_nope * scale).astype(BF16)
        qt_ref[0, h, 0, LANES:, :] = (q_rope * scale + mask_rows).astype(BF16)
        k_nope = _rms(kna[:, h * LANES:(h + 1) * LANES], gk[:, :LANES], A_NOPE)
        k_ref[:, h * hw:h * hw + LANES] = k_nope.astype(BF16)
        k_ref[:, h * hw + LANES:(h + 1) * hw] = k_rope
        vt_ref[0, h, 0] = vat[h * LANES:(h + 1) * LANES].astype(BF16)


def _prep_a(u, h_t, w_kr, off_cq, off_ckv, g_cq, g_ckv, w_uqt, w_k, w_vt, g_q, g_k, rope, rope_t, batch, seq, heads,
            t_rows):
    t = u.shape[0]
    q_rank, kv_rank = w_uqt.shape[1], w_k.shape[0]
    hw = 2 * LANES
    n_seq = seq // t_rows
    assert seq // CHUNK <= LANES - A_ROPE
    scale = (A_NOPE + A_ROPE) ** -0.5 * LOG2E
    row = lambda i: (i, 0)
    const = lambda i: (0, 0)
    tile = lambda i: (i // n_seq, 0, i % n_seq, 0, 0)
    d = h_t.shape[0]
    blocks = (t_rows * (q_rank + kv_rank + d) * 2 + d * LANES * 2 + (q_rank * hw + 2 * kv_rank * LANES) * heads * 2
              + 2 * t_rows * 3 * LANES * 4 + t_rows * heads * (2 * hw + LANES) * 2)
    return pl.pallas_call(
        functools.partial(_prep_a_kernel, heads=heads, scale=scale, n_seq=n_seq),
        out_shape=(jax.ShapeDtypeStruct((batch, heads, n_seq, hw, t_rows), BF16),
                   jax.ShapeDtypeStruct((t, heads * hw), BF16),
                   jax.ShapeDtypeStruct((batch, heads, n_seq, A_V, t_rows), BF16)),
        grid=(t // t_rows,),
        in_specs=[
            pl.BlockSpec((t_rows, q_rank), lambda i: (i, off_cq // q_rank)),
            pl.BlockSpec((t_rows, kv_rank), lambda i: (i, off_ckv // kv_rank)),
            pl.BlockSpec((d, t_rows), lambda i: (0, i)),
            pl.BlockSpec((d, LANES), const),
            pl.BlockSpec((1, q_rank), const),
            pl.BlockSpec((1, kv_rank), const),
            pl.BlockSpec((heads * hw, q_rank), const),
            pl.BlockSpec((kv_rank, heads * LANES), const),
            pl.BlockSpec((heads * LANES, kv_rank), const),
            pl.BlockSpec((hw, 1), const),
            pl.BlockSpec((1, hw), const),
            pl.BlockSpec((t_rows, 3 * LANES), lambda i: (i % n_seq, 0)),
            pl.BlockSpec((3 * LANES, t_rows), lambda i: (0, i % n_seq)),
        ],
        out_specs=(pl.BlockSpec((1, heads, 1, hw, t_rows), tile),
                   pl.BlockSpec((t_rows, heads * hw), row),
                   pl.BlockSpec((1, heads, 1, A_V, t_rows), tile)),
        compiler_params=_params(("parallel",), blocks, 4 * t_rows * heads * hw * 4),
        name="mla_prep",
    )(u, u, h_t, w_kr, g_cq, g_ckv, w_uqt, w_k, w_vt, g_q, g_k, rope, rope_t)


def _attn_a_kernel(qt_ref, k_ref, vt_ref, z_ref, o_ref, *scratch, t):
    s_refs, p_refs, (acc_ref, l_ref) = scratch[:N_BUF], scratch[N_BUF:2 * N_BUF], scratch[2 * N_BUF:]
    _flash_loop(qt_ref, k_ref, vt_ref, s_refs, p_refs, acc_ref, l_ref, tk=t, ratio=1)
    for qi in range(acc_ref.shape[0]):
        _silu_gate_store(o_ref, z_ref, slice(qi * t, (qi + 1) * t), (acc_ref[qi] * (1.0 / l_ref[qi])).T)


def _attn_a(qt, k, vt, u, z_off, batch, seq, heads, t):
    n_tiles = seq // t
    dk = k.shape[1] // heads
    zb = z_off // LANES
    seq_head = lambda b, h: (b, h)
    tiles = lambda b, h: (b, h, 0, 0, 0)
    blocks = 2 * seq * dk * 2 + 3 * seq * LANES * 2
    scratch = N_BUF * t * t * 6 + n_tiles * A_V * t * 4
    return pl.pallas_call(
        functools.partial(_attn_a_kernel, t=t),
        out_shape=jax.ShapeDtypeStruct((batch * seq, heads * LANES), BF16),
        grid=(batch, heads),
        in_specs=[
            pl.BlockSpec((1, 1, n_tiles, dk, t), tiles),
            pl.BlockSpec((seq, dk), seq_head),
            pl.BlockSpec((1, 1, n_tiles, A_V, t), tiles),
            pl.BlockSpec((seq, LANES), lambda b, h: (b, zb + h)),
        ],
        out_specs=pl.BlockSpec((seq, LANES), seq_head),
        scratch_shapes=([pltpu.VMEM((t, t), F32)] * N_BUF + [pltpu.VMEM((t, t), BF16)] * N_BUF
                        + [pltpu.VMEM((n_tiles, A_V, t), F32), pltpu.VMEM((n_tiles, 1, t), F32)]),
        compiler_params=_params(("parallel", "parallel"), blocks, scratch + 6 * t * t * 4),
        name="mla_attention",
    )(qt, k, vt, u)


def _proj_t_b_kernel(wt_ref, ht_ref, gq_ref, gk_ref, ropet_ref, qt_ref, vt_ref, kt_ref, *, heads, scale, n_seq, tq):
    cols = ht_ref.shape[1]
    pos0 = (pl.program_id(0) % n_seq) * cols
    ht = ht_ref[...]
    group = PROJ_HEAD_GROUP * LANES
    rope_t = ropet_ref[...]
    gq = gq_ref[...]
    first_map = lax.broadcasted_iota(jnp.int32, (LANES, 1), 0) < B_DK
    mask_rows = _chunk_mask_rows(pos0, cols, 0).astype(BF16)
    for h in range(heads):
        if h % PROJ_HEAD_GROUP == 0:
            res_q = jnp.dot(wt_ref[h * LANES:h * LANES + group], ht, preferred_element_type=F32)
            res_v = jnp.dot(wt_ref[(heads + h) * LANES:(heads + h) * LANES + group], ht,
                            preferred_element_type=F32)
        g = h % PROJ_HEAD_GROUP
        x = res_q[g * LANES:(g + 1) * LANES]
        xsq = x * x
        ss_lo = jnp.sum(xsq[:B_DK], axis=0, keepdims=True)
        ss_hi = jnp.sum(xsq[B_DK:], axis=0, keepdims=True)
        y = x * lax.rsqrt(jnp.where(first_map, ss_lo, ss_hi) * (1.0 / B_DK) + EPS) * gq
        q = _rotate_rows(y, B_ROT // 2, rope_t) * scale
        q_lo = jnp.where(first_map, q, 0.0).astype(BF16)
        q_hi = jnp.where(first_map, 0.0, q).astype(BF16)
        for a in range(cols // tq):
            cs = slice(a * tq, (a + 1) * tq)
            qt_ref[0, h, a, :LANES, :tq] = q_lo[:, cs]
            qt_ref[0, h, a, :LANES, tq:] = q_hi[:, cs]
            qt_ref[0, h, a, LANES:, :tq] = mask_rows[:, cs]
            qt_ref[0, h, a, LANES:, tq:] = mask_rows[:, cs]
        vt_ref[0, h, 0] = res_v[g * LANES:(g + 1) * LANES].astype(BF16)
    gk = gk_ref[...]
    chunk = (pos0 + lax.broadcasted_iota(jnp.int32, (LANES, cols), 1)) // CHUNK
    slot = lax.broadcasted_iota(jnp.int32, (LANES, cols), 0)
    indicator = jnp.where(slot == chunk, 1.0, 0.0).astype(BF16)
    for h in range(heads):
        if h % PROJ_HEAD_GROUP == 0:
            res_k = jnp.dot(wt_ref[(2 * heads + h) * LANES:(2 * heads + h) * LANES + group], ht,
                            preferred_element_type=F32)
        g = h % PROJ_HEAD_GROUP
        x = res_k[g * LANES:(g + 1) * LANES]
        xsq = x * x
        ss_lo = jnp.sum(xsq[:B_DK], axis=0, keepdims=True)
        ss_hi = jnp.sum(xsq[B_DK:], axis=0, keepdims=True)
        y = x * lax.rsqrt(jnp.where(first_map, ss_lo, ss_hi) * (1.0 / B_DK) + EPS) * gk
        kt_ref[0, h, 0, :LANES, :] = _rotate_rows(y, B_ROT // 2, rope_t).astype(BF16)
        kt_ref[0, h, 0, LANES:, :] = indicator


def _proj_t_b(w_t, h_t, g_q, g_k, rope_t, batch, seq, heads, tq, tk):
    d, t = h_t.shape
    n_seq = seq // tk
    assert seq // CHUNK <= LANES
    tile = lambda i: (i // n_seq, 0, i % n_seq, 0, 0)
    blocks = w_t.size * 2 + d * tk * 2 + 3 * LANES * tk * 4 + heads * tk * (6 * LANES + B_DV) * 2
    return pl.pallas_call(
        functools.partial(_proj_t_b_kernel, heads=heads, scale=B_DK ** -0.5 * LOG2E, n_seq=n_seq, tq=tq),
        out_shape=(jax.ShapeDtypeStruct((batch, heads, seq // tq, 2 * LANES, 2 * tq), BF16),
                   jax.ShapeDtypeStruct((batch, heads, n_seq, B_DV, tk), BF16),
                   jax.ShapeDtypeStruct((batch, heads, n_seq, 2 * LANES, tk), BF16)),
        grid=(t // tk,),
        in_specs=[
            pl.BlockSpec(w_t.shape, lambda i: (0, 0)),
            pl.BlockSpec((d, tk), lambda i: (0, i)),
            pl.BlockSpec((LANES, 1), lambda i: (0, 0)),
            pl.BlockSpec((LANES, 1), lambda i: (0, 0)),
            pl.BlockSpec((3 * LANES, tk), lambda i: (0, i % n_seq)),
        ],
        out_specs=(pl.BlockSpec((1, heads, tk // tq, 2 * LANES, 2 * tq), tile),
                   pl.BlockSpec((1, heads, 1, B_DV, tk), tile),
                   pl.BlockSpec((1, heads, 1, 2 * LANES, tk), tile)),
        compiler_params=_params(("parallel",), blocks, 2 * w_t.shape[0] * tk * 4),
        name="diff_proj_qkv",
    )(w_t, h_t, g_q, g_k, rope_t)


def _attn_b_kernel(qt_ref, k_ref, vt_ref, z_ref, lam_ref, gsub_ref, o_ref, *scratch, tq, tk, lam_init):
    s_refs, p_refs, (acc_ref, l_ref) = scratch[:N_BUF], scratch[N_BUF:2 * N_BUF], scratch[2 * N_BUF:]
    _flash_loop(qt_ref, k_ref, vt_ref, s_refs, p_refs, acc_ref, l_ref, tk=tk, ratio=tk // tq)
    lf = lam_ref[...]
    lam = (jnp.exp(jnp.sum(lf[0:1] * lf[1:2], axis=-1, keepdims=True))
           - jnp.exp(jnp.sum(lf[2:3] * lf[3:4], axis=-1, keepdims=True)) + lam_init)
    gsub = gsub_ref[...]
    for qi in range(acc_ref.shape[0]):
        o = acc_ref[qi] * (1.0 / l_ref[qi])
        a = o[:, :tq] - lam * o[:, tq:]
        ss = jnp.sum(a * a, axis=0, keepdims=True) * (1.0 / B_DV)
        y = a * lax.rsqrt(ss + EPS) * gsub * (1.0 - lam_init)
        _silu_gate_store(o_ref, z_ref, slice(qi * tq, (qi + 1) * tq), y.T)


def _attn_b(qt, k, vt, u, z_off, lam, g_sub, lam_init, batch, seq, heads, tq, tk):
    n_q, n_k = seq // tq, seq // tk
    zb = z_off // LANES
    cols = 2 * tq
    seq_head = lambda b, h: (b, h)
    tiles = lambda b, h: (b, h, 0, 0, 0)
    blocks = 2 * seq * 2 * LANES * 2 + seq * 2 * LANES * 2 + 3 * seq * LANES * 2
    scratch = N_BUF * tk * cols * 6 + n_q * B_DV * cols * 4
    return pl.pallas_call(
        functools.partial(_attn_b_kernel, tq=tq, tk=tk, lam_init=lam_init),
        out_shape=jax.ShapeDtypeStruct((batch * seq, heads * LANES), BF16),
        grid=(batch, heads),
        in_specs=[
            pl.BlockSpec((1, 1, n_q, 2 * LANES, cols), tiles),
            pl.BlockSpec((1, 1, n_k, 2 * LANES, tk), tiles),
            pl.BlockSpec((1, 1, n_k, B_DV, tk), tiles),
            pl.BlockSpec((seq, LANES), lambda b, h: (b, zb + h)),
            pl.BlockSpec(lam.shape, lambda b, h: (0, 0)),
            pl.BlockSpec((B_DV, 1), lambda b, h: (0, 0)),
        ],
        out_specs=pl.BlockSpec((seq, LANES), seq_head),
        scratch_shapes=([pltpu.VMEM((tk, cols), F32)] * N_BUF + [pltpu.VMEM((tk, cols), BF16)] * N_BUF
                        + [pltpu.VMEM((n_q, B_DV, cols), F32), pltpu.VMEM((n_q, 1, cols), F32)]),
        compiler_params=_params(("parallel", "parallel"), blocks, scratch + 6 * tk * cols * 4),
        name="diff_attention",
    )(qt, k, vt, u, lam, g_sub)


C_PAD_SLOT = 64


def _proj_t_c_kernel(wt_ref, ht_ref, gq_ref, qt_ref, vt_ref, *, heads, scale, tq):
    cols = ht_ref.shape[1]
    r = pl.program_id(1)
    is_pad = r == 0
    pos0 = jnp.maximum(r - 1, 0) * cols
    ht = ht_ref[...]
    group = PROJ_HEAD_GROUP * LANES
    gq = gq_ref[...]
    q_chunk = (pos0 + lax.broadcasted_iota(jnp.int32, (LANES, cols), 1)) // CHUNK
    slot = lax.broadcasted_iota(jnp.int32, (LANES, cols), 0)
    out_of_band = ((slot < C_PAD_SLOT) & ((slot > q_chunk) | (slot < q_chunk - C_LEFT_CHUNKS))) | (slot == C_PAD_SLOT)
    mask_rows = jnp.where(out_of_band, MASKED, 0.0).astype(BF16)
    for h in range(heads):
        if h % PROJ_HEAD_GROUP == 0:
            res_q = jnp.dot(wt_ref[h * LANES:h * LANES + group], ht, preferred_element_type=F32)
            res_v = jnp.dot(wt_ref[(heads + h) * LANES:(heads + h) * LANES + group], ht,
                            preferred_element_type=F32)
        g = h % PROJ_HEAD_GROUP
        x = res_q[g * LANES:(g + 1) * LANES]
        q = x * lax.rsqrt(jnp.sum(x * x, axis=0, keepdims=True) * (1.0 / C_DH) + EPS) * gq * scale
        q = q.astype(BF16)
        v = res_v[g * LANES:(g + 1) * LANES].astype(BF16)
        v = jnp.where(is_pad, jnp.zeros_like(v), v)
        for a in range(cols // tq):
            cs = slice(a * tq, (a + 1) * tq)
            qt_ref[0, h, a, :LANES, :] = q[:, cs]
            qt_ref[0, h, a, LANES:, :] = mask_rows[:, cs]
            vt_ref[0, h, a] = v[:, cs]


def _proj_t_c(w_t, h_t, g_q, batch, seq, heads, pad, tq):
    d, t = h_t.shape
    n_seq = seq // pad
    per = pad // tq
    assert seq // CHUNK <= C_PAD_SLOT
    blocks = w_t.size * 2 + d * pad * 2 + heads * pad * 3 * LANES * 2
    return pl.pallas_call(
        functools.partial(_proj_t_c_kernel, heads=heads, scale=C_DH ** -0.5 * LOG2E, tq=tq),
        out_shape=(jax.ShapeDtypeStruct((batch, heads, seq // tq, 2 * LANES, tq), BF16),
                   jax.ShapeDtypeStruct((batch, heads, (seq + pad) // tq, C_DH, tq), BF16)),
        grid=(batch, n_seq + 1),
        in_specs=[
            pl.BlockSpec(w_t.shape, lambda b, r: (0, 0)),
            pl.BlockSpec((d, pad), lambda b, r: (0, b * n_seq + jnp.maximum(r - 1, 0))),
            pl.BlockSpec((LANES, 1), lambda b, r: (0, 0)),
        ],
        out_specs=(pl.BlockSpec((1, heads, per, 2 * LANES, tq), lambda b, r: (b, 0, jnp.maximum(r - 1, 0), 0, 0)),
                   pl.BlockSpec((1, heads, per, C_DH, tq), lambda b, r: (b, 0, r, 0, 0))),
        compiler_params=_params(("parallel", "arbitrary"), blocks, 3 * w_t.shape[0] * pad * 4),
        name="band_proj_qv",
    )(w_t, h_t, g_q)


def _prep_c_kernel(k_ref, gk_ref, ko_ref, *, heads):
    rows = k_ref.shape[0]
    r = pl.program_id(1)
    is_pad = r == 0
    pos0 = jnp.maximum(r - 1, 0) * rows
    gk = gk_ref[...]
    lane = lax.broadcasted_iota(jnp.int32, (rows, LANES), 1)
    indicator = jnp.where(is_pad, jnp.where(lane == C_PAD_SLOT, 1.0, 0.0), _chunk_indicator(pos0, rows, 0))
    indicator = indicator.astype(BF16)
    for h in range(heads):
        kn = _rms(k_ref[:, h * LANES:(h + 1) * LANES].astype(F32), gk, C_DH).astype(BF16)
        ko_ref[:, 2 * h * LANES:(2 * h + 1) * LANES] = jnp.where(is_pad, jnp.zeros_like(kn), kn)
        ko_ref[:, (2 * h + 1) * LANES:(2 * h + 2) * LANES] = indicator


def _prep_c(u, off_k, g_k, batch, seq, heads, pad):
    w = heads * LANES
    n_seq = seq // pad
    return pl.pallas_call(
        functools.partial(_prep_c_kernel, heads=heads),
        out_shape=jax.ShapeDtypeStruct((batch * (seq + pad), 2 * w), BF16),
        grid=(batch, n_seq + 1),
        in_specs=[
            pl.BlockSpec((pad, w), lambda b, r: (b * n_seq + jnp.maximum(r - 1, 0), off_k // w)),
            pl.BlockSpec((1, LANES), lambda b, r: (0, 0)),
        ],
        out_specs=pl.BlockSpec((pad, 2 * w), lambda b, r: (b * (n_seq + 1) + r, 0)),
        compiler_params=_params(("parallel", "parallel"), 3 * pad * w * 2, 4 * pad * LANES * 4),
        name="band_prep_k",
    )(u, g_k)


def _attn_c_kernel(qt_ref, k_ref, vt_ref, z_ref, rel_ref, o_ref, *scratch, tq, tw):
    n_q = qt_ref.shape[2]
    s_refs, p_refs, (bias_ref,) = scratch[:N_BUF], scratch[N_BUF:2 * N_BUF], scratch[2 * N_BUF:]
    width = rel_ref.shape[-1]
    toeplitz = pltpu.roll(jnp.broadcast_to(rel_ref[0], (tq, width)), 0, 1, stride=1, stride_axis=0)
    bias_ref[...] = (toeplitz[:, :tw] * LOG2E).T

    def scores(i):
        return jnp.dot(k_ref[i * tq:i * tq + tw, :], qt_ref[0, 0, i], preferred_element_type=F32) + bias_ref[...]

    s_refs[0][...] = scores(0)
    for i in range(n_q + 1):
        cur, nxt, prv = i % N_BUF, (i + 1) % N_BUF, (i - 1) % N_BUF
        if i + 1 < n_q:
            s_refs[nxt][...] = scores(i + 1)
        if i < n_q:
            s = s_refs[cur][...]
            p = jnp.exp2(s - jnp.max(s, axis=0, keepdims=True))
            p_refs[cur][...] = p.astype(BF16)
            l = jnp.sum(p, axis=0, keepdims=True)
        if i > 0:
            acc = None
            for a in range(tw // tq):
                part = jnp.dot(vt_ref[0, 0, i - 1 + a], p_refs[prv][a * tq:(a + 1) * tq, :],
                               preferred_element_type=F32)
                acc = part if acc is None else acc + part
            _silu_gate_store(o_ref, z_ref, slice((i - 1) * tq, i * tq), (acc * (1.0 / l_prev)).T)
        l_prev = l


def _attn_c(qt, k, vt, u, z_off, rel_rows, batch, seq, heads, tq, pad):
    n_q = seq // tq
    tw = tq + pad
    zb = z_off // LANES
    tiles = lambda b, h: (b, h, 0, 0, 0)
    blocks = seq * 2 * LANES * 2 + (seq + pad) * 3 * LANES * 2 + 2 * seq * LANES * 2
    return pl.pallas_call(
        functools.partial(_attn_c_kernel, tq=tq, tw=tw),
        out_shape=jax.ShapeDtypeStruct((batch * seq, heads * LANES), BF16),
        grid=(batch, heads),
        in_specs=[
            pl.BlockSpec((1, 1, n_q, 2 * LANES, tq), tiles),
            pl.BlockSpec((seq + pad, 2 * LANES), lambda b, h: (b, h)),
            pl.BlockSpec((1, 1, (seq + pad) // tq, C_DH, tq), tiles),
            pl.BlockSpec((seq, LANES), lambda b, h: (b, zb + h)),
            pl.BlockSpec((1, 1, rel_rows.shape[-1]), lambda b, h: (h, 0, 0)),
        ],
        out_specs=pl.BlockSpec((seq, LANES), lambda b, h: (b, h)),
        scratch_shapes=([pltpu.VMEM((tw, tq), F32)] * N_BUF + [pltpu.VMEM((tw, tq), BF16)] * N_BUF
                        + [pltpu.VMEM((tw, tq), F32)]),
        compiler_params=_params(("parallel", "parallel"), blocks, 12 * tw * tq * 4),
        name="band_attention",
    )(qt, k, vt, u, rel_rows)


def _rope_table(seq, dim, theta, group):
    half = dim // 2
    inv = 1.0 / (jnp.float32(theta) ** (jnp.arange(0, dim, 2, dtype=F32) / dim))
    ang = jnp.arange(seq, dtype=F32)[:, None] * inv[None, :]
    cos, sin = jnp.cos(ang), jnp.sin(ang)
    lane = jnp.arange(LANES) % group
    idx = lane % half
    in_lo = (lane < half)[None, :]
    in_hi = ((lane >= half) & (lane < dim))[None, :]
    c = jnp.where(in_lo | in_hi, cos[:, idx], 1.0 if group < LANES else 0.0)
    s_lo = jnp.where(in_lo, -sin[:, idx], 0.0)
    s_hi = jnp.where(in_hi, sin[:, idx], 0.0)
    return jnp.concatenate([c, s_lo, s_hi], axis=1)


def _band_rel_rows(rel_bias, tq, pad):
    width = pl.next_power_of_2(2 * tq + pad)
    e = jnp.arange(width)
    e = jnp.where(e < tq + pad, e, e - width)
    rel = jnp.clip(pad - e, -(CHUNK - 1), C_REL_MAX) + (CHUNK - 1)
    return rel_bias.astype(F32)[:, None, rel]


def _row(v):
    return v.astype(F32).reshape(1, -1)


def _pad_lanes(v, width):
    return jnp.pad(v, ((0, 0),) * (v.ndim - 1) + ((0, width - v.shape[-1]),))


def _layer(x2, layer_idx, batch, seq, rope_a, rope_b, g_pre, w_main, w_t_b, w_t_c, w_kr, a_g_cq, a_g_ckv, a_w_uq,
           a_w_ukv, a_g_q, a_g_k, b_g_q, b_g_k, b_lam, b_g_sub, c_g_q, c_g_k, c_rel_bias, w_branch, w_out):
    d = x2.shape[1]
    bw = w_branch.shape[1]
    q_rank, kv_rank = a_w_uq.shape[0], a_w_ukv.shape[0]
    a_heads, b_heads, c_heads = bw // A_V, bw // B_DV, bw // C_DH

    o_az, o_bz, o_ck, o_cz, o_g = (n * bw for n in range(5))
    o_acq = o_g + N_BRANCH * d
    o_ackv = o_acq + q_rank

    h_t = _rmsnorm(x2, _row(g_pre))
    u = _matmul_t(h_t, w_main, BF16, 1024, MAIN_COLUMN_TILE, "in_proj")

    hw = 2 * LANES
    t_a = min(512, seq)
    w_uq = _pad_lanes(a_w_uq.reshape(q_rank, a_heads, A_NOPE + A_ROPE), hw).reshape(q_rank, a_heads * hw)
    w_ukv = a_w_ukv.reshape(kv_rank, a_heads, A_NOPE + A_V)
    w_k = w_ukv[:, :, :A_NOPE].reshape(kv_rank, a_heads * A_NOPE).astype(BF16)
    w_vt = w_ukv[:, :, A_NOPE:].reshape(kv_rank, a_heads * A_V).T.astype(BF16)
    g_q = jnp.concatenate([_row(a_g_q[:A_NOPE]), _pad_lanes(_row(a_g_q[A_NOPE:]), LANES)], axis=1)
    g_k = jnp.concatenate([_row(a_g_k[:A_NOPE]), _pad_lanes(_row(a_g_k[A_NOPE:]), LANES)], axis=1)
    qa, ka, vta = _prep_a(u, h_t, w_kr, o_acq, o_ackv, _row(a_g_cq), _row(a_g_ckv), w_uq.T.astype(BF16), w_k, w_vt,
                          g_q.T, g_k, rope_a, rope_a.T, batch, seq, a_heads, t_a)
    o_a = _attn_a(qa, ka, vta, u, o_az, batch, seq, a_heads, t_a)

    tq_b, tk_b = min(256, seq), min(512, seq)
    qb, vtb, kb = _proj_t_b(w_t_b, h_t, jnp.tile(_row(b_g_q), (1, 2)).T, jnp.tile(_row(b_g_k), (1, 2)).T, rope_b.T,
                            batch, seq, b_heads, tq_b, tk_b)
    lam_init = 0.8 - 0.6 * math.exp(-0.3 * layer_idx)
    o_b = _attn_b(qb, kb, vtb, u, o_bz, b_lam.astype(F32), b_g_sub.astype(F32).reshape(B_DV, 1), lam_init,
                  batch, seq, b_heads, tq_b, tk_b)

    pad = C_LEFT_CHUNKS * CHUNK
    tq_c = min(256, seq)
    qc, vtc = _proj_t_c(w_t_c, h_t, _row(c_g_q).T, batch, seq, c_heads, pad, tq_c)
    kc = _prep_c(u, o_ck, _row(c_g_k), batch, seq, c_heads, pad)
    o_c = _attn_c(qc, kc, vtc, u, o_cz, _band_rel_rows(c_rel_bias, tq_c, pad), batch, seq, c_heads, tq_c, pad)

    y = _merge(o_a, o_b, o_c, w_branch.astype(BF16), u, o_g, 1024, 1024)
    return _matmul_residual(y, w_out.astype(BF16), x2, 1024, 1024)


def kernel(x, g_pre, w_in, a_g_cq, a_g_ckv, a_w_uq, a_w_ukv, a_g_q, a_g_k, b_g_q, b_g_k, b_lam, b_g_sub,
           c_g_q, c_g_k, c_rel_bias, w_branch, w_out):
    batch, seq, d = x.shape
    rope_a = _rope_table(seq, A_ROPE, A_ROPE_THETA, LANES)
    rope_b = _rope_table(seq, B_ROT, B_ROPE_THETA, B_DK)
    x2 = x.reshape(batch * seq, d)
    q_rank, kv_rank, bw = a_w_uq.shape[1], a_w_ukv.shape[1], w_branch.shape[2]
    n_used = 4 * bw + N_BRANCH * d + q_rank + kv_rank
    n_main = -(-n_used // MAIN_COLUMN_TILE) * MAIN_COLUMN_TILE
    w_in_t = jnp.swapaxes(w_in, 1, 2)
    for l in range(g_pre.shape[0]):
        w_main, w_t_b, w_t_c, w_kr = _weights(w_in_t, l, q_rank, kv_rank, bw, N_BRANCH * d, n_main)
        x2 = _layer(x2, l, batch, seq, rope_a, rope_b, g_pre[l], w_main, w_t_b, w_t_c, w_kr, a_g_cq[l],
                    a_g_ckv[l], a_w_uq[l], a_w_ukv[l], a_g_q[l], a_g_k[l], b_g_q[l], b_g_k[l], b_lam[l],
                    b_g_sub[l], c_g_q[l], c_g_k[l], c_rel_bias[l], w_branch[l], w_out[l])
    return x2.reshape(batch, seq, d)
```
